```python
import math
import jax, jax.numpy as jnp
from jax import lax
import numpy as np

D_MODEL = 2048
BATCH = 2
SEQ = 4096
DEPTH = 1
DEC_BATCH = 32
DEC_SEQ = 4
PAST_LEN = 8192
PAGE_SIZE = 128

H_A = D_MODEL // 256
DH_A = 64
DV_A = 2 * DH_A
H_B = 4
DK_B = D_MODEL // (4 * H_B)
DV_B = D_MODEL // (2 * H_B)
GATE_RANK = 16
GATE_TAU = 16.0
GLA_CHUNK = 64
NUM_BUCKETS = 32
MAX_DISTANCE = 128
D_FF = 11 * D_MODEL // 4
CONV_W = 3
ATTN_Q_BLOCK = 128
EPS = 1e-6
NEG_INF = -1e30

IN_WIDTHS = (H_A * 2 * DH_A, H_A * 2 * DH_A, H_A * DV_A,
             H_B * DK_B, H_B * DK_B, H_B * DV_B, H_B * DV_B,
             GATE_RANK, D_MODEL, D_MODEL)
IN_TOTAL = sum(IN_WIDTHS)
IN_SPLITS = tuple(int(s) for s in np.cumsum(IN_WIDTHS)[:-1])

kernel_name = 'diffattn_gla_gated_merge_convffn_decode_step'


def lambda_init(layer):
    return 0.8 - 0.6 * math.exp(-0.3 * layer)


def rms_norm(x, g):
    xf = x.astype(jnp.float32)
    y = xf * lax.rsqrt(jnp.mean(xf * xf, axis=-1, keepdims=True) + EPS) * g.astype(jnp.float32)
    return y.astype(x.dtype)


def t5_bucket(rel):
    n = jnp.maximum(rel, 0)
    max_exact = NUM_BUCKETS // 2
    nf = jnp.maximum(n, 1).astype(jnp.float32)
    large = max_exact + (jnp.log(nf / max_exact) / math.log(MAX_DISTANCE / max_exact)
                         * (NUM_BUCKETS - max_exact)).astype(jnp.int32)
    large = jnp.minimum(large, NUM_BUCKETS - 1)
    return jnp.where(n < max_exact, n, large)


def diff_attn_queries(q, q_pos, segments, lam, rel_bias):
    qf = q.astype(jnp.float32) * DH_A ** -0.5
    scores = []
    for k, v, k_pos in segments:
        kf = k.reshape(k.shape[:3] + (2, DH_A)).astype(jnp.float32)
        s = jnp.einsum('bqhmd,bkhmd->bhmqk', qf, kf)
        rel = q_pos[:, None] - k_pos[None, :]
        bias = jnp.transpose(rel_bias[t5_bucket(rel)], (2, 0, 1)).astype(jnp.float32)
        scores.append(jnp.where(rel >= 0, s + bias[None, :, None], NEG_INF))
    p = jax.nn.softmax(jnp.concatenate(scores, axis=-1), axis=-1)
    p = p[:, :, 0] - lam * p[:, :, 1]
    outs = []
    off = 0
    for k, v, k_pos in segments:
        tk = k_pos.shape[0]
        outs.append(jnp.einsum('bhqk,bkhd->bqhd', p[..., off:off + tk], v.astype(jnp.float32)))
        off += tk
    o = outs[0]
    for extra in outs[1:]:
        o = o + extra
    return o


def gla_chunked(q, k, v, log_a, s0):
    B, T = q.shape[0], q.shape[1]
    C = min(GLA_CHUNK, T)
    n = -(-T // C)
    pad = n * C - T

    def blocks(a):
        a = a.astype(jnp.float32)
        if pad:
            a = jnp.pad(a, ((0, 0), (0, pad), (0, 0), (0, 0)))
        return a.reshape((B, n, C) + a.shape[2:]).transpose(1, 0, 2, 3, 4)

    causal = jnp.tril(jnp.ones((C, C), dtype=bool))

    def step(S, inp):
        qc, kc, vc, gc = inp
        b = jnp.cumsum(gc, axis=1)
        o_inter = jnp.einsum('bthk,bhkv->bthv', qc * jnp.exp(b), S)
        diff = b[:, :, None] - b[:, None, :]
        decay = jnp.where(causal[None, :, :, None, None], jnp.exp(jnp.minimum(diff, 0.0)), 0.0)
        att = jnp.einsum('btshk,bthk,bshk->bhts', decay, qc, kc)
        o_intra = jnp.einsum('bhts,bshv->bthv', att, vc)
        b_last = b[:, -1]
        S_new = S * jnp.exp(b_last)[..., None] + jnp.einsum(
            'bshk,bshv->bhkv', kc * jnp.exp(b_last[:, None] - b), vc)
        return S_new, o_inter + o_intra

    s_fin, o = lax.scan(step, s0.astype(jnp.float32),
                        (blocks(q), blocks(k), blocks(v), blocks(log_a)))
    o = o.transpose(1, 0, 2, 3, 4).reshape(B, n * C, H_B, DV_B)[:, :T]
    return o, s_fin


def trunk_layer(x, q_pos, past, gla_s0, conv_ctx, rel_bias, lam_init, p):
    B, T, _ = x.shape
    f32 = jnp.float32
    h = rms_norm(x, p['g_pre_mix'])
    z = h @ p['w_in']
    qa, ka, va, qb, kb, vb, rb, lrb, ga, gb = jnp.split(z, IN_SPLITS, axis=-1)

    qa = qa.reshape(B, T, H_A, 2, DH_A)
    ka = ka.reshape(B, T, H_A, 2 * DH_A)
    va = va.reshape(B, T, H_A, DV_A)
    lam = (jnp.exp(jnp.sum(p['lambda_q1'].astype(f32) * p['lambda_k1'].astype(f32)))
           - jnp.exp(jnp.sum(p['lambda_q2'].astype(f32) * p['lambda_k2'].astype(f32)))
           + lam_init)
    segments = [(ka, va, q_pos)] if past is None else [past, (ka, va, q_pos)]
    if T > ATTN_Q_BLOCK and T % ATTN_Q_BLOCK == 0:
        nb = T // ATTN_Q_BLOCK
        qblk = qa.reshape(B, nb, ATTN_Q_BLOCK, H_A, 2, DH_A).transpose(1, 0, 2, 3, 4, 5)
        pblk = q_pos.reshape(nb, ATTN_Q_BLOCK)
        oa = lax.map(lambda a: diff_attn_queries(a[0], a[1], segments, lam, rel_bias), (qblk, pblk))
        oa = oa.transpose(1, 0, 2, 3, 4).reshape(B, T, H_A, DV_A)
    else:
        oa = diff_attn_queries(qa, q_pos, segments, lam, rel_bias)
    oa = (rms_norm(oa, p['g_head_diff']) * (1.0 - lam_init)).reshape(B, T, H_A * DV_A)

    qb = qb.reshape(B, T, H_B, DK_B) * DK_B ** -0.5
    kb = kb.reshape(B, T, H_B, DK_B)
    vb = vb.reshape(B, T, H_B, DV_B)
    log_a = jax.nn.log_sigmoid((lrb @ p['w_gla_gate'] + p['b_gla_gate']).astype(f32)) / GATE_TAU
    log_a = log_a.reshape(B, T, H_B, DK_B)
    if gla_s0 is None:
        gla_s0 = jnp.zeros((B, H_B, DK_B, DV_B), f32)
    ob, s_new = gla_chunked(qb, kb, vb, log_a, gla_s0)
    ob = rms_norm(ob, p['g_head_gla']) * jax.nn.silu(rb.reshape(B, T, H_B, DV_B).astype(f32))
    ob = ob.reshape(B, T, H_B * DV_B)

    m = (jax.nn.sigmoid(ga.astype(f32)) * (oa @ p['w_proj_diff'])
         + jax.nn.sigmoid(gb.astype(f32)) * (ob @ p['w_proj_gla']))
    mix = (m @ p['w_out']).astype(x.dtype)
    x1 = x + rms_norm(mix, p['g_post_mix'])

    u = rms_norm(x1, p['g_pre_ffn']) @ p['w_up']
    if conv_ctx is None:
        conv_ctx = jnp.zeros((B, CONV_W - 1, 2 * D_FF), u.dtype)
    up = jnp.concatenate([conv_ctx.astype(u.dtype), u], axis=1)
    c = p['b_conv'] + p['w_conv'][0] * up[:, 0:T]
    for j in range(1, CONV_W):
        c = c + p['w_conv'][j] * up[:, j:j + T]
    a = jax.nn.gelu(c[..., :D_FF], approximate=True) * c[..., D_FF:]
    f = (a @ p['w_down']).astype(x.dtype)
    y = x1 + rms_norm(f, p['g_post_ffn'])
    new_conv = up[:, T:]
    return y, ka, va, s_new.astype(x.dtype), new_conv.astype(x.dtype)


def setup_inputs(seed: int = 0) -> dict:
    key = jax.random.key(seed)
    ks = jax.random.split(key, 32)
    nrm = jax.random.normal
    n_pages = PAST_LEN // PAGE_SIZE
    n_used = DEC_BATCH * n_pages
    n_pool = n_used + max(1, n_used // 4)
    page_table = jax.random.permutation(ks[4], n_pool)[:n_used].reshape(DEC_BATCH, n_pages).astype(jnp.int32)

    def gain(k, n):
        return 1.0 + 0.02 * nrm(k, (DEPTH, n), jnp.float32)

    return {
        'x_prompt': nrm(ks[0], (BATCH, SEQ, D_MODEL), jnp.float32),
        'x_sample': nrm(ks[1], (DEC_BATCH, DEC_SEQ, D_MODEL), jnp.float32),
        'cache_k': nrm(ks[2], (DEPTH, n_pool, PAGE_SIZE, H_A, 2 * DH_A), jnp.float32),
        'cache_v': nrm(ks[3], (DEPTH, n_pool, PAGE_SIZE, H_A, DV_A), jnp.float32),
        'page_table': page_table,
        'state_gla': 2.0 * nrm(ks[5], (DEPTH, DEC_BATCH, H_B, DK_B, DV_B), jnp.float32),
        'state_conv': nrm(ks[6], (DEPTH, DEC_BATCH, CONV_W - 1, 2 * D_FF), jnp.float32),
        'rel_bias': 0.2 * nrm(ks[7], (NUM_BUCKETS, H_A), jnp.float32),
        'g_pre_mix': gain(ks[8], D_MODEL),
        'w_in': nrm(ks[9], (DEPTH, D_MODEL, IN_TOTAL), jnp.float32) * D_MODEL ** -0.5,
        'lambda_q1': 0.1 * nrm(ks[10], (DEPTH, DH_A), jnp.float32),
        'lambda_k1': 0.1 * nrm(ks[11], (DEPTH, DH_A), jnp.float32),
        'lambda_q2': 0.1 * nrm(ks[12], (DEPTH, DH_A), jnp.float32),
        'lambda_k2': 0.1 * nrm(ks[13], (DEPTH, DH_A), jnp.float32),
        'g_head_diff': gain(ks[14], DV_A),
        'w_gla_gate': nrm(ks[15], (DEPTH, GATE_RANK, H_B * DK_B), jnp.float32) * GATE_RANK ** -0.5,
        'b_gla_gate': 0.1 * nrm(ks[16], (DEPTH, H_B * DK_B), jnp.float32),
        'g_head_gla': gain(ks[17], DV_B),
        'w_proj_diff': nrm(ks[18], (DEPTH, H_A * DV_A, D_MODEL), jnp.float32) * (H_A * DV_A) ** -0.5,
        'w_proj_gla': nrm(ks[19], (DEPTH, H_B * DV_B, D_MODEL), jnp.float32) * (H_B * DV_B) ** -0.5,
        'w_out': nrm(ks[20], (DEPTH, D_MODEL, D_MODEL), jnp.float32) * D_MODEL ** -0.5,
        'g_post_mix': gain(ks[21], D_MODEL),
        'g_pre_ffn': gain(ks[22], D_MODEL),
        'w_up': nrm(ks[23], (DEPTH, D_MODEL, 2 * D_FF), jnp.float32) * D_MODEL ** -0.5,
        'w_conv': nrm(ks[24], (DEPTH, CONV_W, 2 * D_FF), jnp.float32) * CONV_W ** -0.5,
        'b_conv': 0.02 * nrm(ks[25], (DEPTH, 2 * D_FF), jnp.float32),
        'w_down': nrm(ks[26], (DEPTH, D_FF, D_MODEL), jnp.float32) * D_FF ** -0.5,
        'g_post_ffn': gain(ks[27], D_MODEL),
    }


def reference(x_prompt, x_sample, cache_k, cache_v, page_table, state_gla, state_conv, rel_bias,
              g_pre_mix, w_in, lambda_q1, lambda_k1, lambda_q2, lambda_k2, g_head_diff,
              w_gla_gate, b_gla_gate, g_head_gla, w_proj_diff, w_proj_gla, w_out, g_post_mix,
              g_pre_ffn, w_up, w_conv, b_conv, w_down, g_post_ffn):
    t_prompt = x_prompt.shape[1]
    n_dec, t_sample = x_sample.shape[0], x_sample.shape[1]
    past_len = page_table.shape[1] * PAGE_SIZE
    pos_prompt = jnp.arange(t_prompt, dtype=jnp.int32)
    pos_past = jnp.arange(past_len, dtype=jnp.int32)
    pos_sample = past_len + jnp.arange(t_sample, dtype=jnp.int32)

    yp, ys = x_prompt, x_sample
    kp_l, vp_l, sp_l, cp_l, ks_l, vs_l, ss_l, cs_l = [], [], [], [], [], [], [], []
    for l in range(DEPTH):
        p = {
            'g_pre_mix': g_pre_mix[l], 'w_in': w_in[l],
            'lambda_q1': lambda_q1[l], 'lambda_k1': lambda_k1[l],
            'lambda_q2': lambda_q2[l], 'lambda_k2': lambda_k2[l],
            'g_head_diff': g_head_diff[l], 'w_gla_gate': w_gla_gate[l], 'b_gla_gate': b_gla_gate[l],
            'g_head_gla': g_head_gla[l], 'w_proj_diff': w_proj_diff[l], 'w_proj_gla': w_proj_gla[l],
            'w_out': w_out[l], 'g_post_mix': g_post_mix[l], 'g_pre_ffn': g_pre_ffn[l],
            'w_up': w_up[l], 'w_conv': w_conv[l], 'b_conv': b_conv[l], 'w_down': w_down[l],
            'g_post_ffn': g_post_ffn[l],
        }
        lam0 = lambda_init(l)
        yp, kp, vp, sp, cp = trunk_layer(yp, pos_prompt, None, None, None, rel_bias, lam0, p)
        past_k = cache_k[l][page_table].reshape(n_dec, past_len, H_A, 2 * DH_A)
        past_v = cache_v[l][page_table].reshape(n_dec, past_len, H_A, DV_A)
        ys, ksm, vsm, ssm, csm = trunk_layer(ys, pos_sample, (past_k, past_v, pos_past),
                                             state_gla[l], state_conv[l], rel_bias, lam0, p)
        kp_l.append(kp); vp_l.append(vp); sp_l.append(sp); cp_l.append(cp)
        ks_l.append(ksm); vs_l.append(vsm); ss_l.append(ssm); cs_l.append(csm)

    k_prompt = jnp.stack(kp_l)
    v_prompt = jnp.stack(vp_l)
    gla_prompt = jnp.stack(sp_l)
    conv_prompt = jnp.stack(cp_l)
    k_sample = jnp.stack(ks_l)
    v_sample = jnp.stack(vs_l)
    gla_sample = jnp.stack(ss_l)
    conv_sample = jnp.stack(cs_l)
    return (yp, ys, k_prompt, v_prompt, gla_prompt, conv_prompt, k_sample, v_sample, gla_sample, conv_sample)
```

```python
import functools
import math

import numpy as np
import jax
import jax.numpy as jnp
from jax import lax
from jax.experimental import pallas as pl
from jax.experimental.pallas import tpu as pltpu

F32 = jnp.float32
BF16 = jnp.bfloat16

DH_A = 64
DV_A = 128
DK_B = 128
DV_B = 256
H_B = 4
GATE_RANK = 16
GATE_TAU = 16.0
NUM_BUCKETS = 32
MAX_DISTANCE = 128
CONV_W = 3
PAGE_SIZE = 128
EPS = 1e-6
NEG_INF = -1e30

LANES = 128
SUBLANES_BF16 = 16
VMEM_LIMIT_BYTES = 56 * 1024 * 1024

GLA_CHUNK = 64
GLA_SUB = 16


def _lambda_init(layer):
    return 0.8 - 0.6 * math.exp(-0.3 * layer)


def _cparams(*sem):
    return pltpu.CompilerParams(dimension_semantics=sem, vmem_limit_bytes=VMEM_LIMIT_BYTES)


def _const_spec(shape):
    nd = len(shape)
    return pl.BlockSpec(shape, lambda *_: (0,) * nd, pipeline_mode=pl.Buffered(1))


def _rms(xf, g):
    return xf * lax.rsqrt(jnp.mean(xf * xf, axis=-1, keepdims=True) + EPS) * g


def _dot(a, b):
    return jnp.dot(a, b, preferred_element_type=F32)


def _dot_nt(a, b):
    return lax.dot_general(a, b, (((1,), (1,)), ((), ())), preferred_element_type=F32)


def _dot_tn(a, b):
    return lax.dot_general(a, b, (((0,), (0,)), ((), ())), preferred_element_type=F32)


def _split_bf16(x):
    hi = x.astype(BF16)
    lo = (x - hi.astype(F32)).astype(BF16)
    return hi, lo


def _t5_bucket_np(rel):
    n = np.maximum(rel, 0)
    max_exact = NUM_BUCKETS // 2
    nf = np.maximum(n, 1).astype(np.float32)
    large = max_exact + (np.log(nf / max_exact) / math.log(MAX_DISTANCE / max_exact)
                         * (NUM_BUCKETS - max_exact)).astype(np.int32)
    large = np.minimum(large, NUM_BUCKETS - 1)
    return np.where(n < max_exact, n, large).astype(np.int32)


def _bias_tables_kernel(relb_ref, bk_toe_ref, bk_last_ref, bk_new_ref, lq1_ref, lk1_ref, lq2_ref, lk2_ref,
                        toe_ref, sb_last_ref, sb_new_ref, lam_ref, *, n_heads, lam_init):
    far = NUM_BUCKETS - 1

    def lookup(bk, h):
        out = jnp.full(bk.shape, NEG_INF, F32)
        for b in range(NUM_BUCKETS):
            out = jnp.where(bk == b, relb_ref[b, h] - relb_ref[far, h], out)
        return out

    for h in range(n_heads):
        toe_ref[0, h] = lookup(bk_toe_ref[0], h)
        toe_ref[1, h] = lookup(bk_toe_ref[1], h)
    row_h = lax.broadcasted_iota(jnp.int32, bk_last_ref.shape, 0) % n_heads
    last = jnp.zeros(bk_last_ref.shape, F32)
    new = jnp.zeros(bk_new_ref.shape, F32)
    for h in range(n_heads):
        last = jnp.where(row_h == h, lookup(bk_last_ref[...], h), last)
        new = jnp.where(row_h == h, lookup(bk_new_ref[...], h), new)
    sb_last_ref[...] = last
    sb_new_ref[...] = new
    d1 = jnp.sum(lq1_ref[...] * lk1_ref[...], axis=-1, keepdims=True)
    d2 = jnp.sum(lq2_ref[...] * lk2_ref[...], axis=-1, keepdims=True)
    lam = jnp.exp(d1) - jnp.exp(d2) + lam_init
    lam_ref[...] = jnp.broadcast_to(lam, lam_ref.shape)


def _bias_tables(rel_bias, lq1, lk1, lq2, lk2, *, n_heads, t_dec, lam_init):
    i = np.arange(LANES)[:, None]
    j = np.arange(LANES)[None, :]
    bk_toe = np.stack([np.where(i >= j, _t5_bucket_np(i - j), -1),
                       _t5_bucket_np(LANES + i - j)]).astype(np.int32)
    rows = 2 * t_dec * n_heads
    t = ((np.arange(rows) // n_heads) % t_dec)[:, None]
    c = np.arange(LANES)[None, :]
    bk_last = _t5_bucket_np(PAGE_SIZE + t - c).astype(np.int32)
    bk_new = np.where((c <= t) & (c < t_dec), _t5_bucket_np(t - c), -1).astype(np.int32)
    vec = lambda a: a.reshape(1, -1).astype(F32)
    smem = pl.BlockSpec(memory_space=pltpu.SMEM)
    return pl.pallas_call(
        functools.partial(_bias_tables_kernel, n_heads=n_heads, lam_init=lam_init),
        out_shape=(jax.ShapeDtypeStruct((2, n_heads, LANES, LANES), F32),
                   jax.ShapeDtypeStruct((rows, LANES), F32),
                   jax.ShapeDtypeStruct((rows, LANES), F32),
                   jax.ShapeDtypeStruct((8, LANES), F32)),
        in_specs=[smem] + [pl.BlockSpec(memory_space=pltpu.VMEM)] * 7,
        name="bias_tables",
    )(rel_bias.astype(F32), jnp.asarray(bk_toe), jnp.asarray(bk_last), jnp.asarray(bk_new),
      vec(lq1), vec(lk1), vec(lq2), vec(lk2))


def _attn_proj_kernel(x_ref, g_ref, w_ref, wgh_ref, wgl_ref, bg_ref,
                      h_ref, qkv_ref, k32_ref, v32_ref, loga_ref, *, n_attn):
    h = _rms(x_ref[...], g_ref[...]).astype(BF16)
    h_ref[...] = h
    q = _dot(h, w_ref[:, 0:n_attn]) * DH_A ** -0.5
    qkv_ref[:, 0:n_attn] = q.astype(BF16)
    k = _dot(h, w_ref[:, n_attn:2 * n_attn])
    k32_ref[...] = k
    qkv_ref[:, n_attn:2 * n_attn] = k.astype(BF16)
    v = _dot(h, w_ref[:, 2 * n_attn:3 * n_attn])
    v32_ref[...] = v
    qkv_ref[:, 2 * n_attn:3 * n_attn] = v.astype(BF16)
    lr = _dot(h, w_ref[:, 3 * n_attn:3 * n_attn + LANES])
    lr_hi, lr_lo = _split_bf16(lr)
    pre = (_dot(lr_hi, wgh_ref[...]) + _dot(lr_hi, wgl_ref[...]) + _dot(lr_lo, wgh_ref[...])
           + bg_ref[...])
    log_sig = jnp.minimum(pre, 0.0) - jnp.log1p(jnp.exp(-jnp.abs(pre)))
    loga_ref[...] = log_sig / GATE_TAU


def _attn_proj(x, g, w, wgh, wgl, bg, *, tm):
    m, d = x.shape
    n_attn = (w.shape[1] - LANES) // 3
    n_gate = wgh.shape[1]
    row = lambda n: pl.BlockSpec((tm, n), lambda i: (i, 0))
    return pl.pallas_call(
        functools.partial(_attn_proj_kernel, n_attn=n_attn),
        grid=(m // tm,),
        in_specs=[row(d), _const_spec(g.shape), _const_spec(w.shape), _const_spec(wgh.shape),
                  _const_spec(wgl.shape), _const_spec(bg.shape)],
        out_specs=(row(d), row(3 * n_attn), row(n_attn), row(n_attn), row(n_gate)),
        out_shape=(jax.ShapeDtypeStruct((m, d), BF16),
                   jax.ShapeDtypeStruct((m, 3 * n_attn), BF16),
                   jax.ShapeDtypeStruct((m, n_attn), F32),
                   jax.ShapeDtypeStruct((m, n_attn), F32),
                   jax.ShapeDtypeStruct((m, n_gate), F32)),
        compiler_params=_cparams("parallel"),
        name="attn_proj",
    )(x, g, w, wgh, wgl, bg)


def _mix_proj_kernel(h_ref, w_ref, zf_ref, zb_ref, *, n_qk):
    j = pl.program_id(1)
    acc = _dot(h_ref[...], w_ref[...])

    @pl.when(j == 0)
    def _():
        lane = lax.broadcasted_iota(jnp.int32, (1, acc.shape[1]), 1)
        zf_ref[...] = acc * jnp.where(lane < n_qk, DK_B ** -0.5, 1.0)

    @pl.when(j == 1)
    def _():
        zb_ref[...] = acc.astype(BF16)

    @pl.when(j == 2)
    def _():
        zb_ref[...] = (acc * jax.nn.sigmoid(acc)).astype(BF16)

    @pl.when(j >= 3)
    def _():
        zb_ref[...] = jax.nn.sigmoid(acc).astype(BF16)


def _mix_proj(h, w, *, tm, tn):
    m, d = h.shape
    n = w.shape[1]
    nj = n // tn
    return pl.pallas_call(
        functools.partial(_mix_proj_kernel, n_qk=H_B * DK_B),
        grid=(m // tm, nj),
        in_specs=[pl.BlockSpec((tm, d), lambda i, j: (i, 0)),
                  pl.BlockSpec((d, tn), lambda i, j: (0, j))],
        out_specs=(pl.BlockSpec((tm, tn), lambda i, j: (i, 0)),
                   pl.BlockSpec((tm, tn), lambda i, j: (i, jnp.maximum(j - 1, 0)))),
        out_shape=(jax.ShapeDtypeStruct((m, tn), F32),
                   jax.ShapeDtypeStruct((m, n - tn), BF16)),
        compiler_params=_cparams("parallel", "arbitrary"),
        name="mix_proj",
    )(h, w)


def _softmax_step(s, v, m_scr, l_scr, acc_scr):
    m_prev = m_scr[...]
    m_new = jnp.maximum(m_prev, jnp.max(s, axis=-1, keepdims=True))
    alpha = jnp.exp(m_prev - m_new)
    p = jnp.exp(s - m_new)
    l_scr[...] = alpha * l_scr[...] + jnp.sum(p, axis=-1, keepdims=True)
    acc_scr[...] = alpha * acc_scr[...] + _dot(p.astype(BF16), v)
    m_scr[...] = m_new


def _head_norm(o, g, lam_init):
    return o * lax.rsqrt(jnp.mean(o * o, axis=-1, keepdims=True) + EPS) * g * (1.0 - lam_init)


def _attn_prompt_kernel(lam_ref, q_ref, k_ref, v_ref, toe_ref, g_ref, o_ref,
                        bd_scr, bs_scr, q2_scr, m_scr, l_scr, acc_scr, *, tq, lam_init):
    qi = pl.program_id(2)
    nb = tq // LANES

    @pl.when(qi == 0)
    def _():
        for r in range(nb):
            for c in range(nb):
                blk = (slice(r * LANES, (r + 1) * LANES), slice(c * LANES, (c + 1) * LANES))
                for scr, d in ((bd_scr, (r - c) * LANES), (bs_scr, tq + (r - c) * LANES)):
                    if d < 0:
                        scr[blk] = jnp.full((LANES, LANES), NEG_INF, F32)
                    elif d == 0:
                        scr[blk] = toe_ref[0]
                    elif d == LANES:
                        scr[blk] = toe_ref[1]
                    else:
                        scr[blk] = jnp.zeros((LANES, LANES), F32)

    q = q_ref[...]
    lane = lax.broadcasted_iota(jnp.int32, q.shape, 1)
    zero = jnp.zeros_like(q)
    q2_scr[0:tq] = jnp.where(lane < DH_A, q, zero)
    q2_scr[tq:2 * tq] = jnp.where(lane >= DH_A, q, zero)
    m_scr[...] = jnp.full(m_scr.shape, NEG_INF, F32)
    l_scr[...] = jnp.zeros(l_scr.shape, F32)
    acc_scr[...] = jnp.zeros(acc_scr.shape, F32)

    def kv_tile(j, bias):
        off = pl.multiple_of(j * tq, tq)
        s = _dot_nt(q2_scr[...], k_ref[pl.ds(off, tq), :])
        if bias is not None:
            s = jnp.concatenate([s[0:tq] + bias, s[tq:2 * tq] + bias], axis=0)
        _softmax_step(s, v_ref[pl.ds(off, tq), :], m_scr, l_scr, acc_scr)

    def far_tile(j, carry):
        kv_tile(j, None)
        return carry

    lax.fori_loop(0, jnp.maximum(qi - 1, 0), far_tile, 0)

    @pl.when(qi >= 1)
    def _():
        kv_tile(qi - 1, bs_scr[...])

    kv_tile(qi, bd_scr[...])

    on = acc_scr[...] / l_scr[...]
    lam = lam_ref[0:1, 0:1]
    o = on[0:tq] - lam * on[tq:2 * tq]
    o_ref[...] = _head_norm(o, g_ref[...], lam_init).astype(BF16)


def _attn_prompt(qkv, toe, lam, g_head, *, batch, seq, n_heads, tq, lam_init):
    nq = seq // tq
    koff = n_heads
    voff = 2 * n_heads
    return pl.pallas_call(
        functools.partial(_attn_prompt_kernel, tq=tq, lam_init=lam_init),
        grid=(batch, n_heads, nq),
        in_specs=[pl.BlockSpec(lam.shape, lambda b, h, i: (0, 0)),
                  pl.BlockSpec((tq, LANES), lambda b, h, i: (b * nq + i, h)),
                  pl.BlockSpec((seq, LANES), lambda b, h, i: (b, koff + h)),
                  pl.BlockSpec((seq, LANES), lambda b, h, i: (b, voff + h)),
                  pl.BlockSpec((2, None, LANES, LANES), lambda b, h, i: (0, h, 0, 0)),
                  pl.BlockSpec(g_head.shape, lambda b, h, i: (0, 0))],
        out_specs=pl.BlockSpec((tq, LANES), lambda b, h, i: (b * nq + i, h)),
        out_shape=jax.ShapeDtypeStruct((batch * seq, n_heads * LANES), BF16),
        scratch_shapes=[pltpu.VMEM((tq, tq), F32), pltpu.VMEM((tq, tq), F32),
                        pltpu.VMEM((2 * tq, LANES), BF16),
                        pltpu.VMEM((2 * tq, 1), F32), pltpu.VMEM((2 * tq, 1), F32),
                        pltpu.VMEM((2 * tq, LANES), F32)],
        compiler_params=_cparams("parallel", "parallel", "arbitrary"),
        name="attn_prompt",
    )(lam, qkv, qkv, qkv, toe, g_head)


def _attn_sample_kernel(pt_ref, lam_ref, qbd_ref, kn_ref, vn_ref, sbl_ref, sbn_ref, g_ref, *rest,
                        pps, t_dec, n_heads, lam_init):
    k_refs = rest[:pps]
    v_refs = rest[pps:2 * pps]
    o_ref = rest[2 * pps]
    m_scr, l_scr, acc_scr = rest[2 * pps + 1:]
    g = pl.program_id(1)
    last = pl.num_programs(1) - 1

    @pl.when(g == 0)
    def _():
        m_scr[...] = jnp.full(m_scr.shape, NEG_INF, F32)
        l_scr[...] = jnp.zeros(l_scr.shape, F32)
        acc_scr[...] = jnp.zeros(acc_scr.shape, F32)

    qbd = qbd_ref[...]
    kcat = jnp.concatenate([r[...].astype(BF16) for r in k_refs], axis=0)
    vcat = jnp.concatenate([r[...].astype(BF16) for r in v_refs], axis=0)
    s = _dot_nt(qbd, kcat)
    is_last = (g == last).astype(F32)
    w = s.shape[1]
    s = jnp.concatenate([s[:, 0:w - LANES], s[:, w - LANES:w] + is_last * sbl_ref[...]], axis=1)
    _softmax_step(s, vcat, m_scr, l_scr, acc_scr)

    @pl.when(g == last)
    def _():
        pad = jnp.zeros((LANES - kn_ref.shape[0], kn_ref.shape[1]), BF16)
        kn = jnp.concatenate([kn_ref[...], pad], axis=0)
        vn = jnp.concatenate([vn_ref[...], pad], axis=0)
        _softmax_step(_dot_nt(qbd, kn) + sbn_ref[...], vn, m_scr, l_scr, acc_scr)
        on = acc_scr[...] / l_scr[...]
        half = t_dec * n_heads
        d = on[0:half] - lam_ref[0:1, 0:1] * on[half:2 * half]
        row_h = lax.broadcasted_iota(jnp.int32, (n_heads, d.shape[1]), 0)
        col_h = lax.broadcasted_iota(jnp.int32, (n_heads, d.shape[1]), 1) // DV_A
        toks = []
        for t in range(t_dec):
            blk = d[t * n_heads:(t + 1) * n_heads]
            toks.append(jnp.sum(jnp.where(row_h == col_h, blk, 0.0), axis=0, keepdims=True))
        o = jnp.concatenate(toks, axis=0)
        heads = [_head_norm(o[:, h * DV_A:(h + 1) * DV_A], g_ref[...], lam_init) for h in range(n_heads)]
        o_ref[...] = jnp.concatenate(heads, axis=1).astype(BF16)


def _attn_sample(qbd, kn, vn, cache_k, cache_v, page_table, layer_off, sb_last, sb_new, lam, g_head,
                 *, t_dec, n_heads, pps, lam_init):
    nb, rows, width = qbd.shape
    n_pages = page_table.shape[1]
    steps = n_pages // pps
    seq_spec = lambda a: pl.BlockSpec((None,) + a.shape[1:], lambda b, g, pt: (b, 0, 0))
    full = lambda a: pl.BlockSpec(a.shape, lambda b, g, pt: (0, 0))

    def page_spec(jj):
        return pl.BlockSpec((None, PAGE_SIZE, width),
                            lambda b, g, pt: (pt[b, g * pps + jj] + layer_off, 0, 0))

    grid_spec = pltpu.PrefetchScalarGridSpec(
        num_scalar_prefetch=1,
        grid=(nb, steps),
        in_specs=[full(lam), seq_spec(qbd), seq_spec(kn), seq_spec(vn), full(sb_last), full(sb_new),
                  full(g_head)] + [page_spec(jj) for jj in range(pps)] * 2,
        out_specs=pl.BlockSpec((None, t_dec, width), lambda b, g, pt: (b, 0, 0)),
        scratch_shapes=[pltpu.VMEM((rows, 1), F32), pltpu.VMEM((rows, 1), F32),
                        pltpu.VMEM((rows, width), F32)],
    )
    return pl.pallas_call(
        functools.partial(_attn_sample_kernel, pps=pps, t_dec=t_dec, n_heads=n_heads, lam_init=lam_init),
        grid_spec=grid_spec,
        out_shape=jax.ShapeDtypeStruct((nb, t_dec, width), BF16),
        compiler_params=_cparams("parallel", "arbitrary"),
        name="attn_sample",
    )(page_table, lam, qbd, kn, vn, sb_last, sb_new, g_head, *([cache_k] * pps), *([cache_v] * pps))


def _gla_chunk(q, k, log_a, v, st, *, sub):
    c_len = q.shape[0]
    ri = lax.broadcasted_iota(jnp.int32, (c_len, c_len), 0)
    ci = lax.broadcasted_iota(jnp.int32, (c_len, c_len), 1)
    tril = (ri >= ci).astype(BF16)
    a_hi, a_lo = _split_bf16(log_a)
    b = _dot(tril, a_hi) + _dot(tril, a_lo)
    b_last = b[c_len - 1:c_len]
    o = _dot_nt((q * jnp.exp(b)).astype(BF16), st.astype(BF16))
    k_out = (k * jnp.exp(b_last - b)).astype(BF16)
    st_new = st * jnp.exp(b_last) + _dot_tn(v, k_out)
    row = lax.broadcasted_iota(jnp.int32, (c_len, 1), 0)
    atts = []
    for i in range(c_len // sub):
        lo, hi = i * sub, (i + 1) * sub
        b_ref = b[lo:lo + 1]
        qh = (q[lo:hi] * jnp.exp(b[lo:hi] - b_ref)).astype(BF16)
        kh = (k * jnp.exp(jnp.where(row < hi, b_ref - b, NEG_INF))).astype(BF16)
        atts.append(_dot_nt(qh, kh))
    att = jnp.concatenate(atts, axis=0) if len(atts) > 1 else atts[0]
    att = jnp.where(ri >= ci, att, 0.0)
    o = o + _dot(att.astype(BF16), v)
    return o, st_new


def _gla_out(o, g, r):
    return (o * lax.rsqrt(jnp.mean(o * o, axis=-1, keepdims=True) + EPS) * g * r.astype(F32)).astype(BF16)


def _gla_prompt_kernel(qk_ref, la_ref, v_ref, r_ref, g_ref, o_ref, s_ref, st_scr, *, batch, sub):
    ci = pl.program_id(0)

    @pl.when(ci == 0)
    def _():
        st_scr[...] = jnp.zeros(st_scr.shape, F32)

    nk = H_B * DK_B
    for b in range(batch):
        for h in range(H_B):
            ks = slice(h * DK_B, (h + 1) * DK_B)
            vs = slice(h * DV_B, (h + 1) * DV_B)
            o, st_new = _gla_chunk(qk_ref[b, :, ks], qk_ref[b, :, nk + h * DK_B:nk + (h + 1) * DK_B],
                                   la_ref[b, :, ks], v_ref[b, :, vs], st_scr[b, h], sub=sub)
            st_scr[b, h] = st_new
            o_ref[b, :, vs] = _gla_out(o, g_ref[...], r_ref[b, :, vs])

    @pl.when(ci == pl.num_programs(0) - 1)
    def _():
        for b in range(batch):
            for h in range(H_B):
                s_ref[b, h] = st_scr[b, h].T


def _gla_prompt(zf, log_a, zb, g_head, *, batch, seq, chunk, sub):
    nqk = zf.shape[1]
    nv = H_B * DV_B
    zf3 = zf.reshape(batch, seq, nqk)
    la3 = log_a.reshape(batch, seq, log_a.shape[1])
    zb3 = zb.reshape(batch, seq, zb.shape[1])
    blk = lambda n, col: pl.BlockSpec((batch, chunk, n), lambda c: (0, c, col))
    return pl.pallas_call(
        functools.partial(_gla_prompt_kernel, batch=batch, sub=sub),
        grid=(seq // chunk,),
        in_specs=[blk(nqk, 0), blk(la3.shape[2], 0), blk(nv, 0), blk(nv, 1),
                  pl.BlockSpec(g_head.shape, lambda c: (0, 0))],
        out_specs=(blk(nv, 0),
                   pl.BlockSpec((batch, H_B, DK_B, DV_B), lambda c: (0, 0, 0, 0))),
        out_shape=(jax.ShapeDtypeStruct((batch, seq, nv), BF16),
                   jax.ShapeDtypeStruct((batch, H_B, DK_B, DV_B), F32)),
        scratch_shapes=[pltpu.VMEM((batch, H_B, DV_B, DK_B), F32)],
        compiler_params=_cparams("arbitrary"),
        name="gla_prompt",
    )(zf3, la3, zb3, zb3, g_head)


def _gla_sample_kernel(qk_ref, la_ref, v_ref, r_ref, g_ref, s0_ref, o_ref, s_ref, *, sub):
    nk = H_B * DK_B
    for h in range(H_B):
        ks = slice(h * DK_B, (h + 1) * DK_B)
        vs = slice(h * DV_B, (h + 1) * DV_B)
        o, st_new = _gla_chunk(qk_ref[:, ks], qk_ref[:, nk + h * DK_B:nk + (h + 1) * DK_B],
                               la_ref[:, ks], v_ref[:, vs], s0_ref[h].T, sub=sub)
        s_ref[h] = st_new.T
        o_ref[:, vs] = _gla_out(o, g_ref[...], r_ref[:, vs])


def _gla_sample(zf3, la3, zb3, g_head, s0):
    nb, tp, nqk = zf3.shape
    nv = H_B * DV_B
    blk = lambda n, col: pl.BlockSpec((None, tp, n), lambda b: (b, 0, col))
    st = pl.BlockSpec((None, H_B, DK_B, DV_B), lambda b: (b, 0, 0, 0))
    return pl.pallas_call(
        functools.partial(_gla_sample_kernel, sub=tp),
        grid=(nb,),
        in_specs=[blk(nqk, 0), blk(la3.shape[2], 0), blk(nv, 0), blk(nv, 1),
                  pl.BlockSpec(g_head.shape, lambda b: (0, 0)), st],
        out_specs=(blk(nv, 0), st),
        out_shape=(jax.ShapeDtypeStruct((nb, tp, nv), BF16),
                   jax.ShapeDtypeStruct(s0.shape, F32)),
        compiler_params=_cparams("parallel"),
        name="gla_sample",
    )(zf3, la3, zb3, zb3, g_head, s0)


def _merge_kernel(oa_ref, ob_ref, sga_ref, sgb_ref, x_ref, wpd_ref, wpg_ref, wo_ref, g1_ref, g2_ref,
                  x1_ref, h2_ref):
    m = (sga_ref[...].astype(F32) * _dot(oa_ref[...], wpd_ref[...])
         + sgb_ref[...].astype(F32) * _dot(ob_ref[...], wpg_ref[...]))
    mix = _dot(m.astype(BF16), wo_ref[...])
    x1 = x_ref[...] + _rms(mix, g1_ref[...])
    x1_ref[...] = x1
    h2_ref[...] = _rms(x1, g2_ref[...]).astype(BF16)


def _merge(oa, ob, zb, x, wpd, wpg, wo, g1, g2, *, tm):
    m, d = x.shape
    na, nb_ = oa.shape[1], ob.shape[1]
    gate_a = (na + nb_) // d
    gate_b = gate_a + 1
    row = lambda n, col=0: pl.BlockSpec((tm, n), lambda i: (i, col))
    return pl.pallas_call(
        _merge_kernel,
        grid=(m // tm,),
        in_specs=[row(na), row(nb_), row(d, gate_a), row(d, gate_b), row(d),
                  _const_spec(wpd.shape), _const_spec(wpg.shape), _const_spec(wo.shape),
                  _const_spec(g1.shape), _const_spec(g2.shape)],
        out_specs=(row(d), row(d)),
        out_shape=(jax.ShapeDtypeStruct((m, d), F32), jax.ShapeDtypeStruct((m, d), BF16)),
        compiler_params=_cparams("parallel"),
        name="merge",
    )(oa, ob, zb, zb, x, wpd, wpg, wo, g1, g2)


def _geglu(cg, cv):
    return (jax.nn.gelu(cg, approximate=True) * cv).astype(BF16)


def _ffn_finish(acc, x1_ref, g_ref, y_ref):
    y_ref[...] = x1_ref[...] + _rms(acc, g_ref[...])


def _ffn_prompt_kernel(hp_ref, h_ref, x1_ref, wug_ref, wuv_ref, wcg_ref, wcv_ref, bcg_ref, bcv_ref,
                       wd_ref, g_ref, y_ref, csg_ref, csv_ref, acc_scr, *, tm, halo, seq_tiles):
    i = pl.program_id(0)
    j = pl.program_id(1)
    hp = jnp.where(i % seq_tiles == 0, jnp.zeros_like(hp_ref[...]), hp_ref[...])
    hx = jnp.concatenate([hp, h_ref[...]], axis=0)

    def conv(w_up_ref, wc_ref, bc_ref, cs_ref):
        u = _dot(hx, w_up_ref[...])
        cs_ref[...] = u[halo + tm - 8:halo + tm]
        return (bc_ref[...] + wc_ref[0:1] * u[halo - 2:halo - 2 + tm]
                + wc_ref[1:2] * u[halo - 1:halo - 1 + tm] + wc_ref[2:3] * u[halo:halo + tm])

    a = _geglu(conv(wug_ref, wcg_ref, bcg_ref, csg_ref), conv(wuv_ref, wcv_ref, bcv_ref, csv_ref))
    part = _dot(a, wd_ref[...])

    @pl.when(j == 0)
    def _():
        acc_scr[...] = part

    @pl.when(j > 0)
    def _():
        acc_scr[...] += part

    @pl.when(j == pl.num_programs(1) - 1)
    def _():
        _ffn_finish(acc_scr[...], x1_ref, g_ref, y_ref)


def _ffn_prompt(h2, x1, w_up, w_conv, b_conv, w_down, g, *, seq, tm, tf):
    m, d = x1.shape
    d_ff = w_down.shape[0]
    nf = d_ff // tf
    halo = SUBLANES_BF16
    hb = tm // halo
    col = lambda rows, off: pl.BlockSpec((rows, tf), lambda i, j: (0, j + off))
    tok = lambda n: pl.BlockSpec((tm, n), lambda i, j: (i, 0))
    cs = pl.BlockSpec((8, tf), lambda i, j: (i, j))
    cs_shape = jax.ShapeDtypeStruct((m // tm * 8, d_ff), F32)
    return pl.pallas_call(
        functools.partial(_ffn_prompt_kernel, tm=tm, halo=halo, seq_tiles=seq // tm),
        grid=(m // tm, nf),
        in_specs=[pl.BlockSpec((halo, d), lambda i, j: (jnp.maximum(i * hb - 1, 0), 0)),
                  tok(d), tok(d), col(d, 0), col(d, nf), col(CONV_W, 0), col(CONV_W, nf),
                  col(1, 0), col(1, nf), pl.BlockSpec((tf, d), lambda i, j: (j, 0)),
                  pl.BlockSpec(g.shape, lambda i, j: (0, 0))],
        out_specs=(tok(d), cs, cs),
        out_shape=(jax.ShapeDtypeStruct((m, d), F32), cs_shape, cs_shape),
        scratch_shapes=[pltpu.VMEM((tm, d), F32)],
        compiler_params=_cparams("parallel", "arbitrary"),
        name="ffn_prompt",
    )(h2, h2, x1, w_up, w_up, w_conv, w_conv, b_conv, b_conv, w_down, g)


def _ffn_sample_kernel(h_ref, x1_ref, ctxg_ref, ctxv_ref, wug_ref, wuv_ref, wcg_ref, wcv_ref,
                       bcg_ref, bcv_ref, wd_ref, g_ref, y_ref, csg_ref, csv_ref, acc_scr, *, nb):
    j = pl.program_id(0)
    m = h_ref.shape[0]
    h = h_ref[...]

    def conv(w_up_ref, ctx_ref, wc_ref, bc_ref, cs_ref):
        u = _dot(h, w_up_ref[...])
        up = jnp.concatenate([ctx_ref[0], ctx_ref[1], u], axis=0)
        cs_ref[0] = u[m - 2 * nb:m - nb]
        cs_ref[1] = u[m - nb:m]
        return (bc_ref[...] + wc_ref[0:1] * up[0:m] + wc_ref[1:2] * up[nb:nb + m]
                + wc_ref[2:3] * up[2 * nb:2 * nb + m])

    a = _geglu(conv(wug_ref, ctxg_ref, wcg_ref, bcg_ref, csg_ref),
               conv(wuv_ref, ctxv_ref, wcv_ref, bcv_ref, csv_ref))
    part = _dot(a, wd_ref[...])

    @pl.when(j == 0)
    def _():
        acc_scr[...] = part

    @pl.when(j > 0)
    def _():
        acc_scr[...] += part

    @pl.when(j == pl.num_programs(0) - 1)
    def _():
        _ffn_finish(acc_scr[...], x1_ref, g_ref, y_ref)


def _ffn_sample(h2, x1, ctx, w_up, w_conv, b_conv, w_down, g, *, tf):
    m, d = x1.shape
    nb = ctx.shape[1]
    d_ff = w_down.shape[0]
    nf = d_ff // tf
    col = lambda rows, off: pl.BlockSpec((rows, tf), lambda j: (0, j + off))
    full = lambda a: pl.BlockSpec(a.shape, lambda j: (0, 0))
    ctx_spec = lambda off: pl.BlockSpec((CONV_W - 1, nb, tf), lambda j: (0, 0, j + off))
    cs = pl.BlockSpec((CONV_W - 1, nb, tf), lambda j: (0, 0, j))
    cs_shape = jax.ShapeDtypeStruct((CONV_W - 1, nb, d_ff), F32)
    return pl.pallas_call(
        functools.partial(_ffn_sample_kernel, nb=nb),
        grid=(nf,),
        in_specs=[full(h2), full(x1), ctx_spec(0), ctx_spec(nf), col(d, 0), col(d, nf),
                  col(CONV_W, 0), col(CONV_W, nf), col(1, 0), col(1, nf),
                  pl.BlockSpec((tf, d), lambda j: (j, 0)), full(g)],
        out_specs=(full(x1), cs, cs),
        out_shape=(jax.ShapeDtypeStruct((m, d), F32), cs_shape, cs_shape),
        scratch_shapes=[pltpu.VMEM((m, d), F32)],
        compiler_params=_cparams("arbitrary"),
        name="ffn_sample",
    )(h2, x1, ctx, ctx, w_up, w_up, w_conv, w_conv, b_conv, b_conv, w_down, g)


def _pick(n, candidates):
    for c in candidates:
        if n % c == 0:
            return c
    raise ValueError(f"no tile in {candidates} divides {n}")


def _layer_weights(l, p):
    d, n_in = p['w_in'].shape[1:]
    n_attn = p['cache_heads'] * 2 * DH_A
    w = p['w_in'][l]
    o_gla = 3 * n_attn
    o_lr = o_gla + 2 * H_B * DK_B + 2 * H_B * DV_B
    o_gate = o_lr + GATE_RANK
    assert n_in == o_gate + 2 * d
    w_attn = jnp.concatenate([w[:, :o_gla], w[:, o_lr:o_gate],
                              jnp.zeros((d, LANES - GATE_RANK), F32)], axis=1).astype(BF16)
    w_mix = jnp.concatenate([w[:, o_gla:o_lr], w[:, o_gate:]], axis=1).astype(BF16)
    wg = jnp.concatenate([p['w_gla_gate'][l].astype(F32),
                          jnp.zeros((LANES - GATE_RANK, H_B * DK_B), F32)], axis=0)
    wg_hi = wg.astype(BF16)
    wg_lo = (wg - wg_hi.astype(F32)).astype(BF16)
    row = lambda a: a[l].reshape(1, -1).astype(F32)
    return dict(
        w_attn=w_attn, w_mix=w_mix, wg_hi=wg_hi, wg_lo=wg_lo, bg=row(p['b_gla_gate']),
        g_pre_mix=row(p['g_pre_mix']), g_head_diff=row(p['g_head_diff']), g_head_gla=row(p['g_head_gla']),
        wpd=p['w_proj_diff'][l].astype(BF16), wpg=p['w_proj_gla'][l].astype(BF16),
        wo=p['w_out'][l].astype(BF16), g_post_mix=row(p['g_post_mix']), g_pre_ffn=row(p['g_pre_ffn']),
        w_up=p['w_up'][l].astype(BF16), w_conv=p['w_conv'][l].astype(F32), b_conv=row(p['b_conv']),
        w_down=p['w_down'][l].astype(BF16), g_post_ffn=row(p['g_post_ffn']))


def _token_stages(x, lw, *, tm_proj, tm_mix):
    h, qkv, k32, v32, log_a = _attn_proj(x, lw['g_pre_mix'], lw['w_attn'], lw['wg_hi'], lw['wg_lo'],
                                         lw['bg'], tm=tm_proj)
    zf, zb = _mix_proj(h, lw['w_mix'], tm=tm_mix, tn=2 * H_B * DK_B)
    return qkv, k32, v32, log_a, zf, zb


def kernel(x_prompt, x_sample, cache_k, cache_v, page_table, state_gla, state_conv, rel_bias,
           g_pre_mix, w_in, lambda_q1, lambda_k1, lambda_q2, lambda_k2, g_head_diff,
           w_gla_gate, b_gla_gate, g_head_gla, w_proj_diff, w_proj_gla, w_out, g_post_mix,
           g_pre_ffn, w_up, w_conv, b_conv, w_down, g_post_ffn):
    batch, seq, d = x_prompt.shape
    nb, t_dec, _ = x_sample.shape
    depth, n_pool, page, n_heads, kw = cache_k.shape
    assert page == PAGE_SIZE and kw == 2 * DH_A and cache_v.shape[-1] == DV_A
    d_ff = w_down.shape[1]
    width = n_heads * DV_A
    params = dict(w_in=w_in, w_gla_gate=w_gla_gate, b_gla_gate=b_gla_gate, g_pre_mix=g_pre_mix,
                  g_head_diff=g_head_diff, g_head_gla=g_head_gla, w_proj_diff=w_proj_diff,
                  w_proj_gla=w_proj_gla, w_out=w_out, g_post_mix=g_post_mix, g_pre_ffn=g_pre_ffn,
                  w_up=w_up, w_conv=w_conv, b_conv=b_conv, w_down=w_down, g_post_ffn=g_post_ffn,
                  cache_heads=n_heads)

    mp = batch * seq
    ms = nb * t_dec
    tq = _pick(seq, (512, 256, 128))
    tm_proj = _pick(mp, (256, 128))
    tm_mix = _pick(mp, (1024, 512, 256, 128))
    tm_ffn = _pick(seq, (512, 256, 128))
    tf = _pick(d_ff, (512, 256, 128))
    chunk = _pick(seq, (GLA_CHUNK,))
    pps = _pick(page_table.shape[1], (8, 4, 2, 1))
    t_pad = SUBLANES_BF16

    cache_k3 = cache_k.reshape(depth * n_pool, page, width)
    cache_v3 = cache_v.reshape(depth * n_pool, page, width)
    page_table = page_table.astype(jnp.int32)
    eye_h = jnp.eye(n_heads, dtype=BF16)
    eye_2 = jnp.eye(2, dtype=BF16)

    yp = x_prompt.reshape(mp, d)
    ys = x_sample.reshape(ms, d)
    outs = [[] for _ in range(8)]
    for l in range(depth):
        lam0 = _lambda_init(l)
        lw = _layer_weights(l, params)
        toe, sb_last, sb_new, lam = _bias_tables(rel_bias, lambda_q1[l], lambda_k1[l], lambda_q2[l],
                                                 lambda_k2[l], n_heads=n_heads, t_dec=t_dec, lam_init=lam0)

        qkv, k32, v32, log_a, zf, zb = _token_stages(yp, lw, tm_proj=tm_proj, tm_mix=tm_mix)
        oa = _attn_prompt(qkv, toe, lam, lw['g_head_diff'], batch=batch, seq=seq, n_heads=n_heads,
                          tq=tq, lam_init=lam0)
        ob, s_p = _gla_prompt(zf, log_a, zb, lw['g_head_gla'], batch=batch, seq=seq, chunk=chunk,
                              sub=GLA_SUB)
        x1, h2 = _merge(oa, ob.reshape(mp, -1), zb, yp, lw['wpd'], lw['wpg'], lw['wo'],
                        lw['g_post_mix'], lw['g_pre_ffn'], tm=tm_proj)
        yp, csg, csv = _ffn_prompt(h2, x1, lw['w_up'], lw['w_conv'], lw['b_conv'], lw['w_down'],
                                   lw['g_post_ffn'], seq=seq, tm=tm_ffn, tf=tf)
        cs = jnp.concatenate([csg, csv], axis=1).reshape(batch, seq // tm_ffn, 8, 2 * d_ff)
        outs[0].append(k32.reshape(batch, seq, n_heads, 2 * DH_A))
        outs[1].append(v32.reshape(batch, seq, n_heads, DV_A))
        outs[2].append(s_p)
        outs[3].append(cs[:, -1, 8 - (CONV_W - 1):])

        qkv, k32, v32, log_a, zf, zb = _token_stages(ys, lw, tm_proj=ms, tm_mix=ms)
        q5 = qkv[:, :width].reshape(nb, t_dec, n_heads, 2, DH_A)
        qbd = jnp.einsum('bthmd,hg,mn->bmthgnd', q5, eye_h, eye_2).reshape(nb, 2 * t_dec * n_heads, width)
        pad_t = lambda a: jnp.pad(a.reshape(nb, t_dec, -1), ((0, 0), (0, t_pad - t_dec), (0, 0)))
        oa = _attn_sample(qbd, pad_t(qkv[:, width:2 * width]), pad_t(qkv[:, 2 * width:]),
                          cache_k3, cache_v3, page_table, l * n_pool, sb_last, sb_new, lam,
                          lw['g_head_diff'], t_dec=t_dec, n_heads=n_heads, pps=pps, lam_init=lam0)
        ob, s_s = _gla_sample(pad_t(zf), pad_t(log_a), pad_t(zb), lw['g_head_gla'], state_gla[l])
        x1, h2 = _merge(oa.reshape(ms, -1), ob[:, :t_dec].reshape(ms, -1), zb, ys, lw['wpd'], lw['wpg'],
                        lw['wo'], lw['g_post_mix'], lw['g_pre_ffn'], tm=ms)
        tmajor = lambda a: a.reshape(nb, t_dec, -1).transpose(1, 0, 2).reshape(ms, -1)
        y_t, csg, csv = _ffn_sample(tmajor(h2), tmajor(x1), state_conv[l].transpose(1, 0, 2),
                                    lw['w_up'], lw['w_conv'], lw['b_conv'], lw['w_down'],
                                    lw['g_post_ffn'], tf=tf)
        ys = y_t.reshape(t_dec, nb, d).transpose(1, 0, 2).reshape(ms, d)
        outs[4].append(k32.reshape(nb, t_dec, n_heads, 2 * DH_A))
        outs[5].append(v32.reshape(nb, t_dec, n_heads, DV_A))
        outs[6].append(s_s)
        outs[7].append(jnp.concatenate([csg, csv], axis=2).transpose(1, 0, 2))

    return (yp.reshape(batch, seq, d), ys.reshape(nb, t_dec, d)) + tuple(jnp.stack(o) for o in outs)
```

```python
import functools
import math

import numpy as np
import jax
import jax.numpy as jnp
from jax import lax
from jax.experimental import pallas as pl
from jax.experimental.pallas import tpu as pltpu

F32 = jnp.float32
BF16 = jnp.bfloat16

DH_A = 64
DV_A = 128
DK_B = 128
DV_B = 256
H_B = 4
GATE_RANK = 16
GATE_TAU = 16.0
NUM_BUCKETS = 32
MAX_DISTANCE = 128
CONV_W = 3
PAGE_SIZE = 128
EPS = 1e-6
NEG_INF = -1e30

LANES = 128
SUBLANES_BF16 = 16
VMEM_LIMIT_BYTES = 56 * 1024 * 1024

GLA_CHUNK = 64
GLA_SUB = 16


def _lambda_init(layer):
    return 0.8 - 0.6 * math.exp(-0.3 * layer)


def _cparams(*sem):
    return pltpu.CompilerParams(dimension_semantics=sem, vmem_limit_bytes=VMEM_LIMIT_BYTES)


def _const_spec(shape):
    nd = len(shape)
    return pl.BlockSpec(shape, lambda *_: (0,) * nd, pipeline_mode=pl.Buffered(1))


def _rms(xf, g):
    return xf * lax.rsqrt(jnp.mean(xf * xf, axis=-1, keepdims=True) + EPS) * g


def _dot(a, b):
    return jnp.dot(a, b, preferred_element_type=F32)


def _dot_nt(a, b):
    return lax.dot_general(a, b, (((1,), (1,)), ((), ())), preferred_element_type=F32)


def _dot_tn(a, b):
    return lax.dot_general(a, b, (((0,), (0,)), ((), ())), preferred_element_type=F32)


def _split_bf16(x):
    hi = x.astype(BF16)
    lo = (x - hi.astype(F32)).astype(BF16)
    return hi, lo


def _t5_bucket_np(rel):
    n = np.maximum(rel, 0)
    max_exact = NUM_BUCKETS // 2
    nf = np.maximum(n, 1).astype(np.float32)
    large = max_exact + (np.log(nf / max_exact) / math.log(MAX_DISTANCE / max_exact)
                         * (NUM_BUCKETS - max_exact)).astype(np.int32)
    large = np.minimum(large, NUM_BUCKETS - 1)
    return np.where(n < max_exact, n, large).astype(np.int32)


def _bias_tables_kernel(relb_ref, bk_toe_ref, bk_far_ref, bk_last_ref, bk_new_ref,
                        lq1_ref, lk1_ref, lq2_ref, lk2_ref,
                        toe_ref, sb_far_ref, sb_last_ref, sb_new_ref, lam_ref, *, n_heads, lam_init):
    far = NUM_BUCKETS - 1

    def lookup(bk, h):
        out = jnp.full(bk.shape, NEG_INF, F32)
        for b in range(NUM_BUCKETS):
            out = jnp.where(bk == b, relb_ref[b, h] - relb_ref[far, h], out)
        return out

    for h in range(n_heads):
        toe_ref[0, h] = lookup(bk_toe_ref[0], h)
        toe_ref[1, h] = lookup(bk_toe_ref[1], h)

    def per_row_head(bk_ref, out_ref):
        row_h = lax.broadcasted_iota(jnp.int32, bk_ref.shape, 0) % n_heads
        out = jnp.zeros(bk_ref.shape, F32)
        for h in range(n_heads):
            out = jnp.where(row_h == h, lookup(bk_ref[...], h), out)
        out_ref[...] = out

    per_row_head(bk_far_ref, sb_far_ref)
    per_row_head(bk_last_ref, sb_last_ref)
    per_row_head(bk_new_ref, sb_new_ref)
    d1 = jnp.sum(lq1_ref[...] * lk1_ref[...], axis=-1, keepdims=True)
    d2 = jnp.sum(lq2_ref[...] * lk2_ref[...], axis=-1, keepdims=True)
    lam = jnp.exp(d1) - jnp.exp(d2) + lam_init
    lam_ref[...] = jnp.broadcast_to(lam, lam_ref.shape)


def _bias_tables(rel_bias, lq1, lk1, lq2, lk2, *, n_heads, t_dec, lam_init):
    i = np.arange(LANES)[:, None]
    j = np.arange(LANES)[None, :]
    bk_toe = np.stack([np.where(j >= i, _t5_bucket_np(j - i), -1),
                       _t5_bucket_np(LANES + j - i)]).astype(np.int32)
    rows = 2 * t_dec * n_heads
    r = np.arange(rows)[:, None]
    t, h = (r // n_heads) % t_dec, r % n_heads
    c = np.arange(PAGE_SIZE * n_heads)[None, :]
    key, hk = c // n_heads, c % n_heads
    bk_far = np.where(hk == h, NUM_BUCKETS - 1, -1).astype(np.int32)
    bk_last = np.where(hk == h, _t5_bucket_np(PAGE_SIZE + t - key), -1).astype(np.int32)
    cn = c[:, :LANES]
    key, hk = cn // n_heads, cn % n_heads
    bk_new = np.where((hk == h) & (key <= t) & (key < t_dec), _t5_bucket_np(t - key), -1).astype(np.int32)
    vec = lambda a: a.reshape(1, -1).astype(F32)
    smem = pl.BlockSpec(memory_space=pltpu.SMEM)
    tab = lambda a: jax.ShapeDtypeStruct(a.shape, F32)
    return pl.pallas_call(
        functools.partial(_bias_tables_kernel, n_heads=n_heads, lam_init=lam_init),
        out_shape=(jax.ShapeDtypeStruct((2, n_heads, LANES, LANES), F32),
                   tab(bk_far), tab(bk_last), tab(bk_new),
                   jax.ShapeDtypeStruct((8, LANES), F32)),
        in_specs=[smem] + [pl.BlockSpec(memory_space=pltpu.VMEM)] * 8,
        name="bias_tables",
    )(rel_bias.astype(F32), jnp.asarray(bk_toe), jnp.asarray(bk_far), jnp.asarray(bk_last),
      jnp.asarray(bk_new), vec(lq1), vec(lk1), vec(lq2), vec(lk2))


def _attn_proj_kernel(x_ref, g_ref, w_ref, wgh_ref, wgl_ref, bg_ref,
                      h_ref, qkv_ref, k32_ref, v32_ref, loga_ref, *maybe_vt_ref, n_attn):
    h = _rms(x_ref[...], g_ref[...]).astype(BF16)
    h_ref[...] = h
    q = _dot(h, w_ref[:, 0:n_attn]) * DH_A ** -0.5
    qkv_ref[:, 0:n_attn] = q.astype(BF16)
    k = _dot(h, w_ref[:, n_attn:2 * n_attn])
    k32_ref[...] = k
    qkv_ref[:, n_attn:2 * n_attn] = k.astype(BF16)
    v = _dot(h, w_ref[:, 2 * n_attn:3 * n_attn])
    v32_ref[...] = v
    qkv_ref[:, 2 * n_attn:3 * n_attn] = v.astype(BF16)
    for vt_ref in maybe_vt_ref:
        vt_ref[...] = v.T.astype(BF16)
    lr = _dot(h, w_ref[:, 3 * n_attn:3 * n_attn + LANES])
    lr_hi, lr_lo = _split_bf16(lr)
    pre = (_dot(lr_hi, wgh_ref[...]) + _dot(lr_hi, wgl_ref[...]) + _dot(lr_lo, wgh_ref[...])
           + bg_ref[...])
    log_sig = jnp.minimum(pre, 0.0) - jnp.log1p(jnp.exp(-jnp.abs(pre)))
    loga_ref[...] = log_sig / GATE_TAU


def _attn_proj(x, g, w, wgh, wgl, bg, *, tm, transposed_v):
    m, d = x.shape
    n_attn = (w.shape[1] - LANES) // 3
    n_gate = wgh.shape[1]
    row = lambda n: pl.BlockSpec((tm, n), lambda i: (i, 0))
    out_specs = [row(d), row(3 * n_attn), row(n_attn), row(n_attn), row(n_gate)]
    out_shape = [jax.ShapeDtypeStruct((m, d), BF16),
                 jax.ShapeDtypeStruct((m, 3 * n_attn), BF16),
                 jax.ShapeDtypeStruct((m, n_attn), F32),
                 jax.ShapeDtypeStruct((m, n_attn), F32),
                 jax.ShapeDtypeStruct((m, n_gate), F32)]
    if transposed_v:
        out_specs.append(pl.BlockSpec((n_attn, tm), lambda i: (0, i)))
        out_shape.append(jax.ShapeDtypeStruct((n_attn, m), BF16))
    return pl.pallas_call(
        functools.partial(_attn_proj_kernel, n_attn=n_attn),
        grid=(m // tm,),
        in_specs=[row(d), _const_spec(g.shape), _const_spec(w.shape), _const_spec(wgh.shape),
                  _const_spec(wgl.shape), _const_spec(bg.shape)],
        out_specs=tuple(out_specs),
        out_shape=tuple(out_shape),
        compiler_params=_cparams("parallel"),
        name="attn_proj",
    )(x, g, w, wgh, wgl, bg)


def _mix_proj_kernel(h_ref, w_ref, zf_ref, zb_ref, *, n_qk):
    j = pl.program_id(1)
    acc = _dot(h_ref[...], w_ref[...])

    @pl.when(j == 0)
    def _():
        lane = lax.broadcasted_iota(jnp.int32, (1, acc.shape[1]), 1)
        zf_ref[...] = acc * jnp.where(lane < n_qk, DK_B ** -0.5, 1.0)

    @pl.when(j == 1)
    def _():
        zb_ref[...] = acc.astype(BF16)

    @pl.when(j == 2)
    def _():
        zb_ref[...] = (acc * jax.nn.sigmoid(acc)).astype(BF16)

    @pl.when(j >= 3)
    def _():
        zb_ref[...] = jax.nn.sigmoid(acc).astype(BF16)


def _mix_proj(h, w, *, tm, tn):
    m, d = h.shape
    n = w.shape[1]
    nj = n // tn
    return pl.pallas_call(
        functools.partial(_mix_proj_kernel, n_qk=H_B * DK_B),
        grid=(m // tm, nj),
        in_specs=[pl.BlockSpec((tm, d), lambda i, j: (i, 0)),
                  pl.BlockSpec((d, tn), lambda i, j: (0, j))],
        out_specs=(pl.BlockSpec((tm, tn), lambda i, j: (i, 0)),
                   pl.BlockSpec((tm, tn), lambda i, j: (i, jnp.maximum(j - 1, 0)))),
        out_shape=(jax.ShapeDtypeStruct((m, tn), F32),
                   jax.ShapeDtypeStruct((m, n - tn), BF16)),
        compiler_params=_cparams("parallel", "arbitrary"),
        name="mix_proj",
    )(h, w)


def _softmax_step(s, v, m_scr, l_scr, acc_scr):
    m_prev = m_scr[...]
    m_new = jnp.maximum(m_prev, jnp.max(s, axis=-1, keepdims=True))
    alpha = jnp.exp(m_prev - m_new)
    p = jnp.exp(s - m_new)
    l_scr[...] = alpha * l_scr[...] + jnp.sum(p, axis=-1, keepdims=True)
    acc_scr[...] = alpha * acc_scr[...] + _dot(p.astype(BF16), v)
    m_scr[...] = m_new


def _head_norm(o, g, lam_init):
    return o * lax.rsqrt(jnp.mean(o * o, axis=-1, keepdims=True) + EPS) * g * (1.0 - lam_init)


def _attn_prompt_kernel(lam_ref, q_ref, k_ref, vt_ref, toe_ref, g_ref, o_ref,
                        bias_scr, *scr, tq, cb, ahead, lam_init):
    qi = pl.program_id(2)
    nb = tq // LANES

    @pl.when(qi == 0)
    def _():
        for r in range(nb):
            for c in range(nb):
                blk = (slice(r * LANES, (r + 1) * LANES), slice(c * LANES, (c + 1) * LANES))
                for t, d in ((0, tq + (c - r) * LANES), (1, (c - r) * LANES)):
                    if d < 0:
                        bias_scr[(t,) + blk] = jnp.full((LANES, LANES), -NEG_INF, F32)
                    elif d == 0:
                        bias_scr[(t,) + blk] = -toe_ref[0]
                    elif d == LANES:
                        bias_scr[(t,) + blk] = -toe_ref[1]
                    else:
                        bias_scr[(t,) + blk] = jnp.zeros((LANES, LANES), F32)

    n_cb = 2 * tq // cb
    chains = [scr[4 * c:4 * c + 4] for c in range(n_cb)]
    q = q_ref[...]
    lane = lax.broadcasted_iota(jnp.int32, (cb, q.shape[1]), 1)
    for c, (q2_scr, m_scr, l_scr, acc_scr) in enumerate(chains):
        q_blk = q[(c * cb) % tq:(c * cb) % tq + cb]
        keep = (lane < DH_A) if c * cb < tq else (lane >= DH_A)
        q2_scr[...] = jnp.where(keep, q_blk, jnp.zeros_like(q_blk))
        m_scr[...] = jnp.full(m_scr.shape, NEG_INF, F32)
        l_scr[...] = jnp.zeros(l_scr.shape, F32)
        acc_scr[...] = jnp.zeros(acc_scr.shape, F32)

    def kv_tile(j, bias_idx):
        off = pl.multiple_of(j * tq, tq)
        k = k_ref[pl.ds(off, tq), :]
        vt = vt_ref[:, pl.ds(off, tq)]
        bias = None if bias_idx is None else [bias_scr[bias_idx, :, b0:b0 + cb] for b0 in range(0, tq, cb)]
        pending = [_dot_nt(k, chains[c][0][...]) for c in range(min(ahead, n_cb))]
        for c, (_, m_scr, l_scr, acc_scr) in enumerate(chains):
            s = pending.pop(0)
            if c + ahead < n_cb:
                pending.append(_dot_nt(k, chains[c + ahead][0][...]))
            if bias is not None:
                s = s - bias[c % len(bias)]
            m_prev = m_scr[...]
            m_new = jnp.maximum(m_prev, jnp.max(s, axis=0, keepdims=True))
            alpha = jnp.exp(m_prev - m_new)
            p = jnp.exp(s - m_new)
            l_scr[...] = alpha * l_scr[...] + jnp.sum(p, axis=0, keepdims=True)
            acc_scr[...] = alpha * acc_scr[...] + _dot(vt, p.astype(BF16))
            m_scr[...] = m_new

    def far_tile(j, carry):
        kv_tile(j, None)
        return carry

    def near_tile(j, carry):
        kv_tile(j, j - qi + 1)
        return carry

    n_far = jnp.maximum(qi - 1, 0)
    lax.fori_loop(0, n_far, far_tile, 0)
    lax.fori_loop(n_far, qi + 1, near_tile, 0)

    on = jnp.concatenate([acc_scr[...] / l_scr[...] for _, _, l_scr, acc_scr in chains], axis=1)
    lam = lam_ref[0:1, 0:1]
    o_t = on[:, 0:tq] - lam * on[:, tq:2 * tq]
    o_t = o_t * lax.rsqrt(jnp.mean(o_t * o_t, axis=0, keepdims=True) + EPS)
    o_ref[...] = (o_t.T * (g_ref[...] * (1.0 - lam_init))).astype(BF16)


def _attn_prompt(qkv, vt, toe, lam, g_head, *, batch, seq, n_heads, tq, lam_init):
    nq = seq // tq
    koff = n_heads
    cb = min(tq, 2 * LANES)
    return pl.pallas_call(
        functools.partial(_attn_prompt_kernel, tq=tq, cb=cb, ahead=4, lam_init=lam_init),
        grid=(batch, n_heads, nq),
        in_specs=[pl.BlockSpec(lam.shape, lambda b, h, i: (0, 0)),
                  pl.BlockSpec((tq, LANES), lambda b, h, i: (b * nq + i, h)),
                  pl.BlockSpec((seq, LANES), lambda b, h, i: (b, koff + h)),
                  pl.BlockSpec((DV_A, seq), lambda b, h, i: (h, b)),
                  pl.BlockSpec((2, None, LANES, LANES), lambda b, h, i: (0, h, 0, 0)),
                  pl.BlockSpec(g_head.shape, lambda b, h, i: (0, 0))],
        out_specs=pl.BlockSpec((tq, LANES), lambda b, h, i: (b * nq + i, h)),
        out_shape=jax.ShapeDtypeStruct((batch * seq, n_heads * LANES), BF16),
        scratch_shapes=[pltpu.VMEM((2, tq, tq), F32)]
        + [pltpu.VMEM((cb, LANES), BF16), pltpu.VMEM((1, cb), F32), pltpu.VMEM((1, cb), F32),
           pltpu.VMEM((DV_A, cb), F32)] * (2 * tq // cb),
        compiler_params=_cparams("parallel", "parallel", "arbitrary"),
        name="attn_prompt",
    )(lam, qkv, qkv, vt, toe, g_head)


def _attn_sample_kernel(pt_ref, lam_ref, q_ref, kn_ref, vn_ref, sbf_ref, sbl_ref, sbn_ref, g_ref, *rest,
                        pps, lam_init):
    k_refs = rest[:pps]
    v_refs = rest[pps:2 * pps]
    o_ref = rest[2 * pps]
    m_scr, l_scr, acc_scr = rest[2 * pps + 1:]
    g = pl.program_id(1)
    last = pl.num_programs(1) - 1

    @pl.when(g == 0)
    def _():
        m_scr[...] = jnp.full(m_scr.shape, NEG_INF, F32)
        l_scr[...] = jnp.zeros(l_scr.shape, F32)
        acc_scr[...] = jnp.zeros(acc_scr.shape, F32)

    q = q_ref[...]
    cols = sbf_ref.shape[1]
    flat = lambda r: r[...].reshape(cols, r.shape[-1]).astype(BF16)
    kcat = jnp.concatenate([flat(r) for r in k_refs], axis=0)
    vcat = jnp.concatenate([flat(r) for r in v_refs], axis=0)
    s = _dot_nt(q, kcat)
    far = sbf_ref[...]
    tail = jnp.where(g == last, sbl_ref[...], far)
    s = jnp.concatenate([s[:, jj * cols:(jj + 1) * cols] + (far if jj < pps - 1 else tail)
                         for jj in range(pps)], axis=1)
    _softmax_step(s, vcat, m_scr, l_scr, acc_scr)

    @pl.when(g == last)
    def _():
        pad = jnp.zeros((sbn_ref.shape[1] - kn_ref.shape[0], kn_ref.shape[1]), BF16)
        kn = jnp.concatenate([kn_ref[...], pad], axis=0)
        vn = jnp.concatenate([vn_ref[...], pad], axis=0)
        _softmax_step(_dot_nt(q, kn) + sbn_ref[...], vn, m_scr, l_scr, acc_scr)
        on = acc_scr[...] / l_scr[...]
        half = on.shape[0] // 2
        o = on[0:half] - lam_ref[0:1, 0:1] * on[half:2 * half]
        o_ref[...] = _head_norm(o, g_ref[...], lam_init).astype(BF16)


def _attn_sample(q, kn, vn, cache_k, cache_v, page_table, layer, sb_far, sb_last, sb_new, lam, g_head,
                 *, pps, lam_init):
    nb, rows, feat = q.shape
    n_pages = page_table.shape[1]
    steps = n_pages // pps
    seq_spec = lambda a: pl.BlockSpec((None,) + a.shape[1:], lambda b, g, pt: (b, 0, 0))
    full = lambda a: pl.BlockSpec(a.shape, lambda b, g, pt: (0, 0))

    def page_spec(jj):
        return pl.BlockSpec((None, None) + cache_k.shape[2:],
                            lambda b, g, pt: (layer, pt[b, g * pps + jj], 0, 0, 0))

    grid_spec = pltpu.PrefetchScalarGridSpec(
        num_scalar_prefetch=1,
        grid=(nb, steps),
        in_specs=[full(lam), seq_spec(q), seq_spec(kn), seq_spec(vn), full(sb_far), full(sb_last),
                  full(sb_new), full(g_head)] + [page_spec(jj) for jj in range(pps)] * 2,
        out_specs=pl.BlockSpec((None, rows // 2, feat), lambda b, g, pt: (b, 0, 0)),
        scratch_shapes=[pltpu.VMEM((rows, 1), F32), pltpu.VMEM((rows, 1), F32),
                        pltpu.VMEM((rows, feat), F32)],
    )
    return pl.pallas_call(
        functools.partial(_attn_sample_kernel, pps=pps, lam_init=lam_init),
        grid_spec=grid_spec,
        out_shape=jax.ShapeDtypeStruct((nb, rows // 2, feat), BF16),
        compiler_params=_cparams("parallel", "arbitrary"),
        name="attn_sample",
    )(page_table, lam, q, kn, vn, sb_far, sb_last, sb_new, g_head, *([cache_k] * pps), *([cache_v] * pps))


def _gla_chunk(q, k, log_a, v, st, *, sub):
    c_len = q.shape[0]
    ri = lax.broadcasted_iota(jnp.int32, (c_len, c_len), 0)
    ci = lax.broadcasted_iota(jnp.int32, (c_len, c_len), 1)
    tril = (ri >= ci).astype(BF16)
    a_hi, a_lo = _split_bf16(log_a)
    b = _dot(tril, a_hi) + _dot(tril, a_lo)
    b_last = b[c_len - 1:c_len]
    o = _dot_nt((q * jnp.exp(b)).astype(BF16), st.astype(BF16))
    k_out = (k * jnp.exp(b_last - b)).astype(BF16)
    st_new = st * jnp.exp(b_last) + _dot_tn(v, k_out)
    row = lax.broadcasted_iota(jnp.int32, (c_len, 1), 0)
    atts = []
    for i in range(c_len // sub):
        lo, hi = i * sub, (i + 1) * sub
        b_ref = b[lo:lo + 1]
        qh = (q[lo:hi] * jnp.exp(b[lo:hi] - b_ref)).astype(BF16)
        kh = (k * jnp.exp(jnp.where(row < hi, b_ref - b, NEG_INF))).astype(BF16)
        atts.append(_dot_nt(qh, kh))
    att = jnp.concatenate(atts, axis=0) if len(atts) > 1 else atts[0]
    att = jnp.where(ri >= ci, att, 0.0)
    o = o + _dot(att.astype(BF16), v)
    return o, st_new


def _gla_out(o, g, r):
    return (o * lax.rsqrt(jnp.mean(o * o, axis=-1, keepdims=True) + EPS) * g * r.astype(F32)).astype(BF16)


def _gla_prompt_kernel(qk_ref, la_ref, v_ref, r_ref, g_ref, o_ref, s_ref, st_scr, *, batch, sub):
    ci = pl.program_id(0)

    @pl.when(ci == 0)
    def _():
        st_scr[...] = jnp.zeros(st_scr.shape, F32)

    nk = H_B * DK_B
    for b in range(batch):
        for h in range(H_B):
            ks = slice(h * DK_B, (h + 1) * DK_B)
            vs = slice(h * DV_B, (h + 1) * DV_B)
            o, st_new = _gla_chunk(qk_ref[b, :, ks], qk_ref[b, :, nk + h * DK_B:nk + (h + 1) * DK_B],
                                   la_ref[b, :, ks], v_ref[b, :, vs], st_scr[b, h], sub=sub)
            st_scr[b, h] = st_new
            o_ref[b, :, vs] = _gla_out(o, g_ref[...], r_ref[b, :, vs])

    @pl.when(ci == pl.num_programs(0) - 1)
    def _():
        for b in range(batch):
            for h in range(H_B):
                s_ref[b, h] = st_scr[b, h].T


def _gla_prompt(zf, log_a, zb, g_head, *, batch, seq, chunk, sub):
    nqk = zf.shape[1]
    nv = H_B * DV_B
    zf3 = zf.reshape(batch, seq, nqk)
    la3 = log_a.reshape(batch, seq, log_a.shape[1])
    zb3 = zb.reshape(batch, seq, zb.shape[1])
    blk = lambda n, col: pl.BlockSpec((batch, chunk, n), lambda c: (0, c, col))
    return pl.pallas_call(
        functools.partial(_gla_prompt_kernel, batch=batch, sub=sub),
        grid=(seq // chunk,),
        in_specs=[blk(nqk, 0), blk(la3.shape[2], 0), blk(nv, 0), blk(nv, 1),
                  pl.BlockSpec(g_head.shape, lambda c: (0, 0))],
        out_specs=(blk(nv, 0),
                   pl.BlockSpec((batch, H_B, DK_B, DV_B), lambda c: (0, 0, 0, 0))),
        out_shape=(jax.ShapeDtypeStruct((batch, seq, nv), BF16),
                   jax.ShapeDtypeStruct((batch, H_B, DK_B, DV_B), F32)),
        scratch_shapes=[pltpu.VMEM((batch, H_B, DV_B, DK_B), F32)],
        compiler_params=_cparams("arbitrary"),
        name="gla_prompt",
    )(zf3, la3, zb3, zb3, g_head)


def _gla_sample_kernel(qk_ref, la_ref, v_ref, r_ref, g_ref, s0_ref, o_ref, s_ref, *, sub):
    nk = H_B * DK_B
    for h in range(H_B):
        ks = slice(h * DK_B, (h + 1) * DK_B)
        vs = slice(h * DV_B, (h + 1) * DV_B)
        o, st_new = _gla_chunk(qk_ref[:, ks], qk_ref[:, nk + h * DK_B:nk + (h + 1) * DK_B],
                               la_ref[:, ks], v_ref[:, vs], s0_ref[h].T, sub=sub)
        s_ref[h] = st_new.T
        o_ref[:, vs] = _gla_out(o, g_ref[...], r_ref[:, vs])


def _gla_sample(zf3, la3, zb3, g_head, s0):
    nb, tp, nqk = zf3.shape
    nv = H_B * DV_B
    blk = lambda n, col: pl.BlockSpec((None, tp, n), lambda b: (b, 0, col))
    st = pl.BlockSpec((None, H_B, DK_B, DV_B), lambda b: (b, 0, 0, 0))
    return pl.pallas_call(
        functools.partial(_gla_sample_kernel, sub=tp),
        grid=(nb,),
        in_specs=[blk(nqk, 0), blk(la3.shape[2], 0), blk(nv, 0), blk(nv, 1),
                  pl.BlockSpec(g_head.shape, lambda b: (0, 0)), st],
        out_specs=(blk(nv, 0), st),
        out_shape=(jax.ShapeDtypeStruct((nb, tp, nv), BF16),
                   jax.ShapeDtypeStruct(s0.shape, F32)),
        compiler_params=_cparams("parallel"),
        name="gla_sample",
    )(zf3, la3, zb3, zb3, g_head, s0)


def _merge_kernel(oa_ref, ob_ref, sga_ref, sgb_ref, x_ref, wpd_ref, wpg_ref, wo_ref, g1_ref, g2_ref,
                  x1_ref, h2_ref):
    m = (sga_ref[...].astype(F32) * _dot(oa_ref[...], wpd_ref[...])
         + sgb_ref[...].astype(F32) * _dot(ob_ref[...], wpg_ref[...]))
    mix = _dot(m.astype(BF16), wo_ref[...])
    x1 = x_ref[...] + _rms(mix, g1_ref[...])
    x1_ref[...] = x1
    h2_ref[...] = _rms(x1, g2_ref[...]).astype(BF16)


def _merge(oa, ob, zb, x, wpd, wpg, wo, g1, g2, *, tm):
    m, d = x.shape
    na, nb_ = oa.shape[1], ob.shape[1]
    gate_a = (na + nb_) // d
    gate_b = gate_a + 1
    row = lambda n, col=0: pl.BlockSpec((tm, n), lambda i: (i, col))
    return pl.pallas_call(
        _merge_kernel,
        grid=(m // tm,),
        in_specs=[row(na), row(nb_), row(d, gate_a), row(d, gate_b), row(d),
                  _const_spec(wpd.shape), _const_spec(wpg.shape), _const_spec(wo.shape),
                  _const_spec(g1.shape), _const_spec(g2.shape)],
        out_specs=(row(d), row(d)),
        out_shape=(jax.ShapeDtypeStruct((m, d), F32), jax.ShapeDtypeStruct((m, d), BF16)),
        compiler_params=_cparams("parallel"),
        name="merge",
    )(oa, ob, zb, zb, x, wpd, wpg, wo, g1, g2)


def _geglu(cg, cv):
    return (jax.nn.gelu(cg, approximate=True) * cv).astype(BF16)


def _ffn_finish(acc, x1_ref, g_ref, y_ref):
    y_ref[...] = x1_ref[...] + _rms(acc, g_ref[...])


def _ffn_prompt_kernel(hp_ref, h_ref, x1_ref, wug_ref, wuv_ref, wcg_ref, wcv_ref, bcg_ref, bcv_ref,
                       wd_ref, g_ref, y_ref, csg_ref, csv_ref, acc_scr, *, tm, halo, seq_tiles):
    i = pl.program_id(0)
    j = pl.program_id(1)
    hp = jnp.where(i % seq_tiles == 0, jnp.zeros_like(hp_ref[...]), hp_ref[...])
    hx = jnp.concatenate([hp, h_ref[...]], axis=0)

    def conv(w_up_ref, wc_ref, bc_ref, cs_ref):
        u = _dot(hx, w_up_ref[...])
        cs_ref[...] = u[halo + tm - 8:halo + tm]
        return (bc_ref[...] + wc_ref[0:1] * u[halo - 2:halo - 2 + tm]
                + wc_ref[1:2] * u[halo - 1:halo - 1 + tm] + wc_ref[2:3] * u[halo:halo + tm])

    a = _geglu(conv(wug_ref, wcg_ref, bcg_ref, csg_ref), conv(wuv_ref, wcv_ref, bcv_ref, csv_ref))
    part = _dot(a, wd_ref[...])

    @pl.when(j == 0)
    def _():
        acc_scr[...] = part

    @pl.when(j > 0)
    def _():
        acc_scr[...] += part

    @pl.when(j == pl.num_programs(1) - 1)
    def _():
        _ffn_finish(acc_scr[...], x1_ref, g_ref, y_ref)


def _ffn_prompt(h2, x1, w_up, w_conv, b_conv, w_down, g, *, seq, tm, tf):
    m, d = x1.shape
    d_ff = w_down.shape[0]
    nf = d_ff // tf
    halo = SUBLANES_BF16
    hb = tm // halo
    col = lambda rows, off: pl.BlockSpec((rows, tf), lambda i, j: (0, j + off))
    tok = lambda n: pl.BlockSpec((tm, n), lambda i, j: (i, 0))
    cs = pl.BlockSpec((8, tf), lambda i, j: (i, j))
    cs_shape = jax.ShapeDtypeStruct((m // tm * 8, d_ff), F32)
    return pl.pallas_call(
        functools.partial(_ffn_prompt_kernel, tm=tm, halo=halo, seq_tiles=seq // tm),
        grid=(m // tm, nf),
        in_specs=[pl.BlockSpec((halo, d), lambda i, j: (jnp.maximum(i * hb - 1, 0), 0)),
                  tok(d), tok(d), col(d, 0), col(d, nf), col(CONV_W, 0), col(CONV_W, nf),
                  col(1, 0), col(1, nf), pl.BlockSpec((tf, d), lambda i, j: (j, 0)),
                  pl.BlockSpec(g.shape, lambda i, j: (0, 0))],
        out_specs=(tok(d), cs, cs),
        out_shape=(jax.ShapeDtypeStruct((m, d), F32), cs_shape, cs_shape),
        scratch_shapes=[pltpu.VMEM((tm, d), F32)],
        compiler_params=_cparams("parallel", "arbitrary"),
        name="ffn_prompt",
    )(h2, h2, x1, w_up, w_up, w_conv, w_conv, b_conv, b_conv, w_down, g)


def _ffn_sample_kernel(h_ref, x1_ref, ctxg_ref, ctxv_ref, wug_ref, wuv_ref, wcg_ref, wcv_ref,
                       bcg_ref, bcv_ref, wd_ref, g_ref, y_ref, csg_ref, csv_ref, acc_scr, *, nb):
    j = pl.program_id(0)
    m = h_ref.shape[0]
    h = h_ref[...]

    def conv(w_up_ref, ctx_ref, wc_ref, bc_ref, cs_ref):
        u = _dot(h, w_up_ref[...])
        up = jnp.concatenate([ctx_ref[0], ctx_ref[1], u], axis=0)
        cs_ref[0] = u[m - 2 * nb:m - nb]
        cs_ref[1] = u[m - nb:m]
        return (bc_ref[...] + wc_ref[0:1] * up[0:m] + wc_ref[1:2] * up[nb:nb + m]
                + wc_ref[2:3] * up[2 * nb:2 * nb + m])

    a = _geglu(conv(wug_ref, ctxg_ref, wcg_ref, bcg_ref, csg_ref),
               conv(wuv_ref, ctxv_ref, wcv_ref, bcv_ref, csv_ref))
    part = _dot(a, wd_ref[...])

    @pl.when(j == 0)
    def _():
        acc_scr[...] = part

    @pl.when(j > 0)
    def _():
        acc_scr[...] += part

    @pl.when(j == pl.num_programs(0) - 1)
    def _():
        _ffn_finish(acc_scr[...], x1_ref, g_ref, y_ref)


def _ffn_sample(h2, x1, ctx, w_up, w_conv, b_conv, w_down, g, *, tf):
    m, d = x1.shape
    nb = ctx.shape[1]
    d_ff = w_down.shape[0]
    nf = d_ff // tf
    col = lambda rows, off: pl.BlockSpec((rows, tf), lambda j: (0, j + off))
    full = lambda a: pl.BlockSpec(a.shape, lambda j: (0, 0))
    ctx_spec = lambda off: pl.BlockSpec((CONV_W - 1, nb, tf), lambda j: (0, 0, j + off))
    cs = pl.BlockSpec((CONV_W - 1, nb, tf), lambda j: (0, 0, j))
    cs_shape = jax.ShapeDtypeStruct((CONV_W - 1, nb, d_ff), F32)
    return pl.pallas_call(
        functools.partial(_ffn_sample_kernel, nb=nb),
        grid=(nf,),
        in_specs=[full(h2), full(x1), ctx_spec(0), ctx_spec(nf), col(d, 0), col(d, nf),
                  col(CONV_W, 0), col(CONV_W, nf), col(1, 0), col(1, nf),
                  pl.BlockSpec((tf, d), lambda j: (j, 0)), full(g)],
        out_specs=(full(x1), cs, cs),
        out_shape=(jax.ShapeDtypeStruct((m, d), F32), cs_shape, cs_shape),
        scratch_shapes=[pltpu.VMEM((m, d), F32)],
        compiler_params=_cparams("arbitrary"),
        name="ffn_sample",
    )(h2, x1, ctx, ctx, w_up, w_up, w_conv, w_conv, b_conv, b_conv, w_down, g)


def _pick(n, candidates):
    for c in candidates:
        if n % c == 0:
            return c
    raise ValueError(f"no tile in {candidates} divides {n}")


def _layer_weights(l, p):
    d, n_in = p['w_in'].shape[1:]
    n_attn = p['cache_heads'] * 2 * DH_A
    w = p['w_in'][l]
    o_gla = 3 * n_attn
    o_lr = o_gla + 2 * H_B * DK_B + 2 * H_B * DV_B
    o_gate = o_lr + GATE_RANK
    assert n_in == o_gate + 2 * d
    w_attn = jnp.concatenate([w[:, :o_gla], w[:, o_lr:o_gate],
                              jnp.zeros((d, LANES - GATE_RANK), F32)], axis=1).astype(BF16)
    w_mix = jnp.concatenate([w[:, o_gla:o_lr], w[:, o_gate:]], axis=1).astype(BF16)
    wg = jnp.concatenate([p['w_gla_gate'][l].astype(F32),
                          jnp.zeros((LANES - GATE_RANK, H_B * DK_B), F32)], axis=0)
    wg_hi = wg.astype(BF16)
    wg_lo = (wg - wg_hi.astype(F32)).astype(BF16)
    row = lambda a: a[l].reshape(1, -1).astype(F32)
    return dict(
        w_attn=w_attn, w_mix=w_mix, wg_hi=wg_hi, wg_lo=wg_lo, bg=row(p['b_gla_gate']),
        g_pre_mix=row(p['g_pre_mix']), g_head_diff=row(p['g_head_diff']), g_head_gla=row(p['g_head_gla']),
        wpd=p['w_proj_diff'][l].astype(BF16), wpg=p['w_proj_gla'][l].astype(BF16),
        wo=p['w_out'][l].astype(BF16), g_post_mix=row(p['g_post_mix']), g_pre_ffn=row(p['g_pre_ffn']),
        w_up=p['w_up'][l].astype(BF16), w_conv=p['w_conv'][l].astype(F32), b_conv=row(p['b_conv']),
        w_down=p['w_down'][l].astype(BF16), g_post_ffn=row(p['g_post_ffn']))


def _token_stages(x, lw, *, tm_proj, tm_mix, transposed_v):
    h, *proj = _attn_proj(x, lw['g_pre_mix'], lw['w_attn'], lw['wg_hi'], lw['wg_lo'], lw['bg'],
                          tm=tm_proj, transposed_v=transposed_v)
    zf, zb = _mix_proj(h, lw['w_mix'], tm=tm_mix, tn=2 * H_B * DK_B)
    return proj, zf, zb


def kernel(x_prompt, x_sample, cache_k, cache_v, page_table, state_gla, state_conv, rel_bias,
           g_pre_mix, w_in, lambda_q1, lambda_k1, lambda_q2, lambda_k2, g_head_diff,
           w_gla_gate, b_gla_gate, g_head_gla, w_proj_diff, w_proj_gla, w_out, g_post_mix,
           g_pre_ffn, w_up, w_conv, b_conv, w_down, g_post_ffn):
    batch, seq, d = x_prompt.shape
    nb, t_dec, _ = x_sample.shape
    depth, n_pool, page, n_heads, kw = cache_k.shape
    assert page == PAGE_SIZE and kw == 2 * DH_A and cache_v.shape[-1] == DV_A
    d_ff = w_down.shape[1]
    width = n_heads * DV_A
    params = dict(w_in=w_in, w_gla_gate=w_gla_gate, b_gla_gate=b_gla_gate, g_pre_mix=g_pre_mix,
                  g_head_diff=g_head_diff, g_head_gla=g_head_gla, w_proj_diff=w_proj_diff,
                  w_proj_gla=w_proj_gla, w_out=w_out, g_post_mix=g_post_mix, g_pre_ffn=g_pre_ffn,
                  w_up=w_up, w_conv=w_conv, b_conv=b_conv, w_down=w_down, g_post_ffn=g_post_ffn,
                  cache_heads=n_heads)

    mp = batch * seq
    ms = nb * t_dec
    tq = _pick(seq, (512, 256, 128))
    tm_proj = _pick(mp, (256, 128))
    tm_mix = _pick(mp, (1024, 512, 256, 128))
    tm_ffn = _pick(seq, (512, 256, 128))
    tf = _pick(d_ff, (512, 256, 128))
    chunk = _pick(seq, (GLA_CHUNK,))
    pps = _pick(page_table.shape[1], (8, 4, 2, 1))
    t_pad = SUBLANES_BF16

    page_table = page_table.astype(jnp.int32)
    eye_2 = jnp.eye(2, dtype=BF16)

    yp = x_prompt.reshape(mp, d)
    ys = x_sample.reshape(ms, d)
    outs = [[] for _ in range(8)]
    for l in range(depth):
        lam0 = _lambda_init(l)
        lw = _layer_weights(l, params)
        toe, sb_far, sb_last, sb_new, lam = _bias_tables(rel_bias, lambda_q1[l], lambda_k1[l], lambda_q2[l],
                                                 lambda_k2[l], n_heads=n_heads, t_dec=t_dec, lam_init=lam0)

        (qkv, k32, v32, log_a, vt), zf, zb = _token_stages(yp, lw, tm_proj=tm_proj, tm_mix=tm_mix,
                                                           transposed_v=True)
        oa = _attn_prompt(qkv, vt, toe, lam, lw['g_head_diff'], batch=batch, seq=seq, n_heads=n_heads,
                          tq=tq, lam_init=lam0)
        ob, s_p = _gla_prompt(zf, log_a, zb, lw['g_head_gla'], batch=batch, seq=seq, chunk=chunk,
                              sub=GLA_SUB)
        x1, h2 = _merge(oa, ob.reshape(mp, -1), zb, yp, lw['wpd'], lw['wpg'], lw['wo'],
                        lw['g_post_mix'], lw['g_pre_ffn'], tm=tm_proj)
        yp, csg, csv = _ffn_prompt(h2, x1, lw['w_up'], lw['w_conv'], lw['b_conv'], lw['w_down'],
                                   lw['g_post_ffn'], seq=seq, tm=tm_ffn, tf=tf)
        cs = jnp.concatenate([csg, csv], axis=1).reshape(batch, seq // tm_ffn, 8, 2 * d_ff)
        outs[0].append(k32.reshape(batch, seq, n_heads, 2 * DH_A))
        outs[1].append(v32.reshape(batch, seq, n_heads, DV_A))
        outs[2].append(s_p)
        outs[3].append(cs[:, -1, 8 - (CONV_W - 1):])

        (qkv, k32, v32, log_a), zf, zb = _token_stages(ys, lw, tm_proj=ms, tm_mix=ms, transposed_v=False)
        q5 = qkv[:, :width].reshape(nb, t_dec, n_heads, 2, DH_A)
        q_rows = jnp.einsum('bthnd,mn->bmthnd', q5, eye_2).reshape(nb, 2 * t_dec * n_heads, 2 * DH_A)
        per_head = lambda a: a.reshape(nb, t_dec * n_heads, -1)
        pad_t = lambda a: jnp.pad(a.reshape(nb, t_dec, -1), ((0, 0), (0, t_pad - t_dec), (0, 0)))
        oa = _attn_sample(q_rows, per_head(qkv[:, width:2 * width]), per_head(qkv[:, 2 * width:]),
                          cache_k, cache_v, page_table, l, sb_far, sb_last, sb_new, lam,
                          lw['g_head_diff'], pps=pps, lam_init=lam0)
        ob, s_s = _gla_sample(pad_t(zf), pad_t(log_a), pad_t(zb), lw['g_head_gla'], state_gla[l])
        x1, h2 = _merge(oa.reshape(ms, -1), ob[:, :t_dec].reshape(ms, -1), zb, ys, lw['wpd'], lw['wpg'],
                        lw['wo'], lw['g_post_mix'], lw['g_pre_ffn'], tm=ms)
        tmajor = lambda a: a.reshape(nb, t_dec, -1).transpose(1, 0, 2).reshape(ms, -1)
        y_t, csg, csv = _ffn_sample(tmajor(h2), tmajor(x1), state_conv[l].transpose(1, 0, 2),
                                    lw['w_up'], lw['w_conv'], lw['b_conv'], lw['w_down'],
                                    lw['g_post_ffn'], tf=tf)
        ys = y_t.reshape(t_dec, nb, d).transpose(1, 0, 2).reshape(ms, d)
        outs[4].append(k32.reshape(nb, t_dec, n_heads, 2 * DH_A))
        outs[5].append(v32.reshape(nb, t_dec, n_heads, DV_A))
        outs[6].append(s_s)
        outs[7].append(jnp.concatenate([csg, csv], axis=2).transpose(1, 0, 2))

    return (yp.reshape(batch, seq, d), ys.reshape(nb, t_dec, d)) + tuple(jnp.stack(o) for o in outs)
```

```python
import functools
import math

import numpy as np
import jax
import jax.numpy as jnp
from jax import lax
from jax.experimental import pallas as pl
from jax.experimental.pallas import tpu as pltpu

F32 = jnp.float32
BF16 = jnp.bfloat16

DH_A = 64
DV_A = 128
DK_B = 128
DV_B = 256
H_B = 4
GATE_RANK = 16
GATE_TAU = 16.0
NUM_BUCKETS = 32
MAX_DISTANCE = 128
CONV_W = 3
PAGE_SIZE = 128
EPS = 1e-6
NEG_INF = -1e30

LANES = 128
SUBLANES_BF16 = 16
VMEM_LIMIT_BYTES = 56 * 1024 * 1024

GLA_CHUNK = 64
GLA_SUB = 16


def _lambda_init(layer):
    return 0.8 - 0.6 * math.exp(-0.3 * layer)


def _cparams(*sem):
    return pltpu.CompilerParams(dimension_semantics=sem, vmem_limit_bytes=VMEM_LIMIT_BYTES)


def _const_spec(shape):
    nd = len(shape)
    return pl.BlockSpec(shape, lambda *_: (0,) * nd, pipeline_mode=pl.Buffered(1))


def _rms(xf, g):
    return xf * lax.rsqrt(jnp.mean(xf * xf, axis=-1, keepdims=True) + EPS) * g


def _dot(a, b):
    return jnp.dot(a, b, preferred_element_type=F32)


def _dot_nt(a, b):
    return lax.dot_general(a, b, (((1,), (1,)), ((), ())), preferred_element_type=F32)


def _dot_tn(a, b):
    return lax.dot_general(a, b, (((0,), (0,)), ((), ())), preferred_element_type=F32)


def _split_bf16(x):
    hi = x.astype(BF16)
    lo = (x - hi.astype(F32)).astype(BF16)
    return hi, lo


def _t5_bucket_np(rel):
    n = np.maximum(rel, 0)
    max_exact = NUM_BUCKETS // 2
    nf = np.maximum(n, 1).astype(np.float32)
    large = max_exact + (np.log(nf / max_exact) / math.log(MAX_DISTANCE / max_exact)
                         * (NUM_BUCKETS - max_exact)).astype(np.int32)
    large = np.minimum(large, NUM_BUCKETS - 1)
    return np.where(n < max_exact, n, large).astype(np.int32)


def _bias_tables_kernel(relb_ref, bk_toe_ref, bk_far_ref, bk_last_ref, bk_new_ref,
                        lq1_ref, lk1_ref, lq2_ref, lk2_ref,
                        toe_ref, sb_far_ref, sb_last_ref, sb_new_ref, lam_ref, *, n_heads, lam_init):
    far = NUM_BUCKETS - 1

    def lookup(bk, h):
        out = jnp.full(bk.shape, NEG_INF, F32)
        for b in range(NUM_BUCKETS):
            out = jnp.where(bk == b, relb_ref[b, h] - relb_ref[far, h], out)
        return out

    for h in range(n_heads):
        toe_ref[0, h] = lookup(bk_toe_ref[0], h)
        toe_ref[1, h] = lookup(bk_toe_ref[1], h)

    def per_row_head(bk_ref, out_ref):
        row_h = lax.broadcasted_iota(jnp.int32, bk_ref.shape, 0) % n_heads
        out = jnp.zeros(bk_ref.shape, F32)
        for h in range(n_heads):
            out = jnp.where(row_h == h, lookup(bk_ref[...], h), out)
        out_ref[...] = -out

    per_row_head(bk_far_ref, sb_far_ref)
    per_row_head(bk_last_ref, sb_last_ref)
    per_row_head(bk_new_ref, sb_new_ref)
    d1 = jnp.sum(lq1_ref[...] * lk1_ref[...], axis=-1, keepdims=True)
    d2 = jnp.sum(lq2_ref[...] * lk2_ref[...], axis=-1, keepdims=True)
    lam = jnp.exp(d1) - jnp.exp(d2) + lam_init
    lam_ref[...] = jnp.broadcast_to(lam, lam_ref.shape)


def _bias_tables(rel_bias, lq1, lk1, lq2, lk2, *, n_heads, t_dec, lam_init):
    i = np.arange(LANES)[:, None]
    j = np.arange(LANES)[None, :]
    bk_toe = np.stack([np.where(j >= i, _t5_bucket_np(j - i), -1),
                       _t5_bucket_np(LANES + j - i)]).astype(np.int32)
    rows = 2 * t_dec * n_heads
    r = np.arange(rows)[:, None]
    t, h = (r // n_heads) % t_dec, r % n_heads
    c = np.arange(PAGE_SIZE * n_heads)[None, :]
    key, hk = c // n_heads, c % n_heads
    bk_far = np.where(hk == h, NUM_BUCKETS - 1, -1).astype(np.int32)
    bk_last = np.where(hk == h, _t5_bucket_np(PAGE_SIZE + t - key), -1).astype(np.int32)
    cn = c[:, :LANES]
    key, hk = cn // n_heads, cn % n_heads
    bk_new = np.where((hk == h) & (key <= t) & (key < t_dec), _t5_bucket_np(t - key), -1).astype(np.int32)
    vec = lambda a: a.reshape(1, -1).astype(F32)
    smem = pl.BlockSpec(memory_space=pltpu.SMEM)
    tab = lambda a: jax.ShapeDtypeStruct(a.shape, F32)
    return pl.pallas_call(
        functools.partial(_bias_tables_kernel, n_heads=n_heads, lam_init=lam_init),
        out_shape=(jax.ShapeDtypeStruct((2, n_heads, LANES, LANES), F32),
                   tab(bk_far), tab(bk_last), tab(bk_new),
                   jax.ShapeDtypeStruct((8, LANES), F32)),
        in_specs=[smem] + [pl.BlockSpec(memory_space=pltpu.VMEM)] * 8,
        name="bias_tables",
    )(rel_bias.astype(F32), jnp.asarray(bk_toe), jnp.asarray(bk_far), jnp.asarray(bk_last),
      jnp.asarray(bk_new), vec(lq1), vec(lk1), vec(lq2), vec(lk2))


def _attn_proj_kernel(x_ref, g_ref, w_ref, wlr_ref, wgh_ref, wgl_ref, bg_ref,
                      h_ref, qkv_ref, k32_ref, v32_ref, zf_ref, loga_ref, *maybe_vt_ref, n_attn, n_qk):
    h = _rms(x_ref[...], g_ref[...]).astype(BF16)
    h_ref[...] = h
    q = _dot(h, w_ref[:, 0:n_attn]) * DH_A ** -0.5
    qkv_ref[:, 0:n_attn] = q.astype(BF16)
    k = _dot(h, w_ref[:, n_attn:2 * n_attn])
    k32_ref[...] = k
    qkv_ref[:, n_attn:2 * n_attn] = k.astype(BF16)
    v = _dot(h, w_ref[:, 2 * n_attn:3 * n_attn])
    v32_ref[...] = v
    qkv_ref[:, 2 * n_attn:3 * n_attn] = v.astype(BF16)
    for vt_ref in maybe_vt_ref:
        vt_ref[...] = v.T.astype(BF16)
    zf_ref[:, 0:n_qk] = _dot(h, w_ref[:, 3 * n_attn:3 * n_attn + n_qk]) * DK_B ** -0.5
    zf_ref[:, n_qk:2 * n_qk] = _dot(h, w_ref[:, 3 * n_attn + n_qk:3 * n_attn + 2 * n_qk])
    lr = _dot(h, wlr_ref[...])
    lr_hi, lr_lo = _split_bf16(lr)
    pre = (_dot(lr_hi, wgh_ref[...]) + _dot(lr_hi, wgl_ref[...]) + _dot(lr_lo, wgh_ref[...])
           + bg_ref[...])
    log_sig = jnp.minimum(pre, 0.0) - jnp.log1p(jnp.exp(-jnp.abs(pre)))
    loga_ref[...] = log_sig / GATE_TAU


def _attn_proj(x, g, w_in, o_lr, wgh, wgl, bg, *, n_attn, tm, transposed_v):
    m, d = x.shape
    n_qk = H_B * DK_B
    n_lead = 3 * n_attn + 2 * n_qk
    assert o_lr % LANES == 0
    row = lambda n: pl.BlockSpec((tm, n), lambda i: (i, 0))
    out_specs = [row(d), row(3 * n_attn), row(n_attn), row(n_attn), row(2 * n_qk), row(n_qk)]
    out_shape = [jax.ShapeDtypeStruct((m, d), BF16),
                 jax.ShapeDtypeStruct((m, 3 * n_attn), BF16),
                 jax.ShapeDtypeStruct((m, n_attn), F32),
                 jax.ShapeDtypeStruct((m, n_attn), F32),
                 jax.ShapeDtypeStruct((m, 2 * n_qk), F32),
                 jax.ShapeDtypeStruct((m, n_qk), F32)]
    if transposed_v:
        out_specs.append(pl.BlockSpec((n_attn, tm), lambda i: (0, i)))
        out_shape.append(jax.ShapeDtypeStruct((n_attn, m), BF16))
    once = pl.Buffered(1)
    return pl.pallas_call(
        functools.partial(_attn_proj_kernel, n_attn=n_attn, n_qk=n_qk),
        grid=(m // tm,),
        in_specs=[row(d), _const_spec(g.shape),
                  pl.BlockSpec((d, n_lead), lambda i: (0, 0), pipeline_mode=once),
                  pl.BlockSpec((d, LANES), lambda i: (0, o_lr // LANES), pipeline_mode=once),
                  _const_spec(wgh.shape), _const_spec(wgl.shape), _const_spec(bg.shape)],
        out_specs=tuple(out_specs),
        out_shape=tuple(out_shape),
        compiler_params=_cparams("parallel"),
        name="attn_proj",
    )(x, g, w_in, w_in, wgh, wgl, bg)


def _sigmoid(x):
    return 0.5 * jnp.tanh(0.5 * x) + 0.5


_ACTS = {
    'id': lambda x: x,
    'silu': lambda x: x * _sigmoid(x),
    'sigmoid': _sigmoid,
}


def _act_proj_kernel(h_ref, w_ref, o_ref, *, acts, cw):
    h = h_ref[...]
    accs = [_dot(h, w_ref[:, c * cw:(c + 1) * cw]) for c in range(len(acts))]
    for c, (act, acc) in enumerate(zip(acts, accs)):
        o_ref[:, c * cw:(c + 1) * cw] = _ACTS[act](acc).astype(BF16)


def _act_proj(h, w, col0, n, acts, *, tm, tn, name):
    m, d = h.shape
    cw = tn // len(acts)
    assert col0 % tn == 0 and n % tn == 0 and cw % LANES == 0
    j0 = col0 // tn
    nj = n // tn
    w_mode = dict(pipeline_mode=pl.Buffered(1)) if nj == 1 else {}
    return pl.pallas_call(
        functools.partial(_act_proj_kernel, acts=acts, cw=cw),
        grid=(m // tm, nj),
        in_specs=[pl.BlockSpec((tm, d), lambda i, j: (i, 0)),
                  pl.BlockSpec((d, tn), lambda i, j: (0, j0 + j), **w_mode)],
        out_specs=pl.BlockSpec((tm, tn), lambda i, j: (i, j)),
        out_shape=jax.ShapeDtypeStruct((m, n), BF16),
        compiler_params=_cparams("parallel", "arbitrary"),
        name=name,
    )(h, w)


def _softmax_step(s, v, m_scr, l_scr, acc_scr):
    m_prev = m_scr[...]
    m_new = jnp.maximum(m_prev, jnp.max(s, axis=-1, keepdims=True))
    alpha = jnp.exp(m_prev - m_new)
    p = jnp.exp(s - m_new)
    l_scr[...] = alpha * l_scr[...] + jnp.sum(p, axis=-1, keepdims=True)
    acc_scr[...] = alpha * acc_scr[...] + _dot(p.astype(BF16), v)
    m_scr[...] = m_new


def _head_norm(o, g, lam_init):
    return o * lax.rsqrt(jnp.mean(o * o, axis=-1, keepdims=True) + EPS) * g * (1.0 - lam_init)


def _attn_prompt_kernel(lam_ref, q_ref, k_ref, vt_ref, toe_ref, g_ref, o_ref,
                        bias_scr, *scr, tq, cb, ahead, lam_init):
    qi = pl.program_id(2)
    nb = tq // LANES

    @pl.when(qi == 0)
    def _():
        for r in range(nb):
            for c in range(nb):
                blk = (slice(r * LANES, (r + 1) * LANES), slice(c * LANES, (c + 1) * LANES))
                for t, d in ((0, tq + (c - r) * LANES), (1, (c - r) * LANES)):
                    if d < 0:
                        bias_scr[(t,) + blk] = jnp.full((LANES, LANES), -NEG_INF, F32)
                    elif d == 0:
                        bias_scr[(t,) + blk] = -toe_ref[0]
                    elif d == LANES:
                        bias_scr[(t,) + blk] = -toe_ref[1]
                    else:
                        bias_scr[(t,) + blk] = jnp.zeros((LANES, LANES), F32)

    n_cb = 2 * tq // cb
    chains = [scr[4 * c:4 * c + 4] for c in range(n_cb)]
    q = q_ref[...]
    lane = lax.broadcasted_iota(jnp.int32, (cb, q.shape[1]), 1)
    for c, (q2_scr, m_scr, l_scr, acc_scr) in enumerate(chains):
        q_blk = q[(c * cb) % tq:(c * cb) % tq + cb]
        keep = (lane < DH_A) if c * cb < tq else (lane >= DH_A)
        q2_scr[...] = jnp.where(keep, q_blk, jnp.zeros_like(q_blk))
        m_scr[...] = jnp.full(m_scr.shape, NEG_INF, F32)
        l_scr[...] = jnp.zeros(l_scr.shape, F32)
        acc_scr[...] = jnp.zeros(acc_scr.shape, F32)

    def kv_tiles(js, bias_idx):
        offs = [pl.multiple_of(j * tq, tq) for j in js]
        ks = [k_ref[pl.ds(off, tq), :] for off in offs]
        vts = [vt_ref[:, pl.ds(off, tq)] for off in offs]
        bias = None if bias_idx is None else [
            [bias_scr[bi, :, b0:b0 + cb] for b0 in range(0, tq, cb)] for bi in bias_idx]
        blocks = [(t, c) for t in range(len(js)) for c in range(n_cb)]
        score = lambda t, c: _dot_nt(ks[t], chains[c][0][...])
        pending = [score(t, c) for t, c in blocks[:ahead]]
        for n, (t, c) in enumerate(blocks):
            _, m_scr, l_scr, acc_scr = chains[c]
            s = pending.pop(0)
            if n + ahead < len(blocks):
                pending.append(score(*blocks[n + ahead]))
            if bias is not None:
                s = s - bias[t][c % len(bias[t])]
            m_prev = m_scr[...]
            m_new = jnp.maximum(m_prev, jnp.max(s, axis=0, keepdims=True))
            alpha = jnp.exp(m_prev - m_new)
            p = jnp.exp(s - m_new)
            l_scr[...] = alpha * l_scr[...] + jnp.sum(p, axis=0, keepdims=True)
            acc_scr[...] = alpha * acc_scr[...] + _dot(vts[t], p.astype(BF16))
            m_scr[...] = m_new

    def far_pair(i, carry):
        kv_tiles([2 * i, 2 * i + 1], None)
        return carry

    def far_tile(j, carry):
        kv_tiles([j], None)
        return carry

    n_far = jnp.maximum(qi - 1, 0)
    lax.fori_loop(0, n_far // 2, far_pair, 0)
    lax.fori_loop(n_far // 2 * 2, n_far, far_tile, 0)

    @pl.when(qi == 0)
    def _():
        kv_tiles([0], [1])

    @pl.when(qi > 0)
    def _():
        kv_tiles([qi - 1, qi], [0, 1])

    on = jnp.concatenate([acc_scr[...] / l_scr[...] for _, _, l_scr, acc_scr in chains], axis=1)
    lam = lam_ref[0:1, 0:1]
    o_t = on[:, 0:tq] - lam * on[:, tq:2 * tq]
    o_t = o_t * lax.rsqrt(jnp.mean(o_t * o_t, axis=0, keepdims=True) + EPS)
    o_ref[...] = (o_t.T * (g_ref[...] * (1.0 - lam_init))).astype(BF16)


def _attn_prompt(qkv, vt, toe, lam, g_head, *, batch, seq, n_heads, tq, lam_init):
    nq = seq // tq
    koff = n_heads
    cb = min(tq, 2 * LANES)
    return pl.pallas_call(
        functools.partial(_attn_prompt_kernel, tq=tq, cb=cb, ahead=4, lam_init=lam_init),
        grid=(batch, n_heads, nq),
        in_specs=[pl.BlockSpec(lam.shape, lambda b, h, i: (0, 0)),
                  pl.BlockSpec((tq, LANES), lambda b, h, i: (b * nq + i, h)),
                  pl.BlockSpec((seq, LANES), lambda b, h, i: (b, koff + h)),
                  pl.BlockSpec((DV_A, seq), lambda b, h, i: (h, b)),
                  pl.BlockSpec((2, None, LANES, LANES), lambda b, h, i: (0, h, 0, 0)),
                  pl.BlockSpec(g_head.shape, lambda b, h, i: (0, 0))],
        out_specs=pl.BlockSpec((tq, LANES), lambda b, h, i: (b * nq + i, h)),
        out_shape=jax.ShapeDtypeStruct((batch * seq, n_heads * LANES), BF16),
        scratch_shapes=[pltpu.VMEM((2, tq, tq), F32)]
        + [pltpu.VMEM((cb, LANES), BF16), pltpu.VMEM((1, cb), F32), pltpu.VMEM((1, cb), F32),
           pltpu.VMEM((DV_A, cb), F32)] * (2 * tq // cb),
        compiler_params=_cparams("parallel", "parallel", "arbitrary"),
        name="attn_prompt",
    )(lam, qkv, qkv, vt, toe, g_head)


def _attn_sample_kernel(pt_ref, lam_ref, q_ref, kn_ref, vn_ref, sbf_ref, sbl_ref, sbn_ref, g_ref, *rest,
                        pps, bp, lam_init):
    k_refs = rest[:pps]
    v_refs = rest[pps:2 * pps]
    o_ref = rest[2 * pps]
    m_scr, l_scr, acc_scr = rest[2 * pps + 1:]
    g = pl.program_id(1)
    last = pl.num_programs(1) - 1

    @pl.when(g == 0)
    def _():
        m_scr[...] = jnp.full(m_scr.shape, NEG_INF, F32)
        l_scr[...] = jnp.zeros(l_scr.shape, F32)
        acc_scr[...] = jnp.zeros(acc_scr.shape, F32)

    q = q_ref[...]
    cols = sbf_ref.shape[1]
    flat = lambda r: r[...].reshape(cols, r.shape[-1]).astype(BF16)
    pages = lambda refs, b: jnp.concatenate([flat(r) for r in refs[b * bp:(b + 1) * bp]], axis=0)
    scores = [_dot_nt(q, pages(k_refs, b)) for b in range(pps // bp)]
    far = sbf_ref[...]
    tail = jnp.where(g == last, sbl_ref[...], far)
    for b, s in enumerate(scores):
        s = jnp.concatenate([s[:, i * cols:(i + 1) * cols] - (far if b * bp + i < pps - 1 else tail)
                             for i in range(bp)], axis=1)
        _softmax_step(s, pages(v_refs, b), m_scr, l_scr, acc_scr)

    @pl.when(g == last)
    def _():
        pad = jnp.zeros((sbn_ref.shape[1] - kn_ref.shape[0], kn_ref.shape[1]), BF16)
        kn = jnp.concatenate([kn_ref[...], pad], axis=0)
        vn = jnp.concatenate([vn_ref[...], pad], axis=0)
        _softmax_step(_dot_nt(q, kn) - sbn_ref[...], vn, m_scr, l_scr, acc_scr)
        on = acc_scr[...] / l_scr[...]
        half = on.shape[0] // 2
        o = on[0:half] - lam_ref[0:1, 0:1] * on[half:2 * half]
        o_ref[...] = _head_norm(o, g_ref[...], lam_init).astype(BF16)


def _attn_sample(q, kn, vn, cache_k, cache_v, page_table, layer, sb_far, sb_last, sb_new, lam, g_head,
                 *, pps, lam_init):
    nb, rows, feat = q.shape
    n_pages = page_table.shape[1]
    steps = n_pages // pps
    seq_spec = lambda a: pl.BlockSpec((None,) + a.shape[1:], lambda b, g, pt: (b, 0, 0))
    full = lambda a: pl.BlockSpec(a.shape, lambda b, g, pt: (0, 0))

    def page_spec(jj):
        return pl.BlockSpec((None, None) + cache_k.shape[2:],
                            lambda b, g, pt: (layer, pt[b, g * pps + jj], 0, 0, 0))

    grid_spec = pltpu.PrefetchScalarGridSpec(
        num_scalar_prefetch=1,
        grid=(nb, steps),
        in_specs=[full(lam), seq_spec(q), seq_spec(kn), seq_spec(vn), full(sb_far), full(sb_last),
                  full(sb_new), full(g_head)] + [page_spec(jj) for jj in range(pps)] * 2,
        out_specs=pl.BlockSpec((None, rows // 2, feat), lambda b, g, pt: (b, 0, 0)),
        scratch_shapes=[pltpu.VMEM((rows, 1), F32), pltpu.VMEM((rows, 1), F32),
                        pltpu.VMEM((rows, feat), F32)],
    )
    return pl.pallas_call(
        functools.partial(_attn_sample_kernel, pps=pps, bp=min(pps, 2), lam_init=lam_init),
        grid_spec=grid_spec,
        out_shape=jax.ShapeDtypeStruct((nb, rows // 2, feat), BF16),
        compiler_params=_cparams("parallel", "arbitrary"),
        name="attn_sample",
    )(page_table, lam, q, kn, vn, sb_far, sb_last, sb_new, g_head, *([cache_k] * pps), *([cache_v] * pps))


def _gla_chunk(heads, *, sub):
    c_len = heads[0][0].shape[0]
    ri = lax.broadcasted_iota(jnp.int32, (c_len, c_len), 0)
    ci = lax.broadcasted_iota(jnp.int32, (c_len, c_len), 1)
    row = lax.broadcasted_iota(jnp.int32, (c_len, 1), 0)
    tril = (ri >= ci).astype(BF16)
    bs = []
    for _, _, log_a, _, _ in heads:
        a_hi, a_lo = _split_bf16(log_a)
        bs.append(_dot(tril, a_hi) + _dot(tril, a_lo))
    outs, states = [], []
    for (q, k, _, v, st), b in zip(heads, bs):
        b_last = b[c_len - 1:c_len]
        outs.append(_dot_nt((q * jnp.exp(b)).astype(BF16), st.astype(BF16)))
        k_out = (k * jnp.exp(b_last - b)).astype(BF16)
        states.append(st * jnp.exp(b_last) + _dot_tn(v, k_out))
    atts = []
    for (q, k, _, _, _), b in zip(heads, bs):
        rows = []
        for i in range(c_len // sub):
            lo, hi = i * sub, (i + 1) * sub
            b_ref = b[lo:lo + 1]
            qh = (q[lo:hi] * jnp.exp(b[lo:hi] - b_ref)).astype(BF16)
            kh = (k * jnp.exp(jnp.where(row < hi, b_ref - b, NEG_INF))).astype(BF16)
            rows.append(_dot_nt(qh, kh))
        att = jnp.concatenate(rows, axis=0) if len(rows) > 1 else rows[0]
        atts.append(jnp.where(ri >= ci, att, 0.0).astype(BF16))
    outs = [o + _dot(att, v) for o, att, (_, _, _, v, _) in zip(outs, atts, heads)]
    return list(zip(outs, states))


def _gla_out(o, g, r):
    return (o * lax.rsqrt(jnp.mean(o * o, axis=-1, keepdims=True) + EPS) * g * r.astype(F32)).astype(BF16)


def _gla_prompt_kernel(qk_ref, la_ref, v_ref, r_ref, g_ref, o_ref, s_ref, st_scr, *, batch, sub):
    ci = pl.program_id(0)

    @pl.when(ci == 0)
    def _():
        st_scr[...] = jnp.zeros(st_scr.shape, F32)

    nk = H_B * DK_B
    ids = [(b, h) for b in range(batch) for h in range(H_B)]
    ks = lambda h: slice(h * DK_B, (h + 1) * DK_B)
    vs = lambda h: slice(h * DV_B, (h + 1) * DV_B)
    heads = [(qk_ref[b, :, ks(h)], qk_ref[b, :, nk + h * DK_B:nk + (h + 1) * DK_B], la_ref[b, :, ks(h)],
              v_ref[b, :, vs(h)], st_scr[b, h]) for b, h in ids]
    for (b, h), (o, st_new) in zip(ids, _gla_chunk(heads, sub=sub)):
        st_scr[b, h] = st_new
        o_ref[b, :, vs(h)] = _gla_out(o, g_ref[...], r_ref[b, :, vs(h)])

    @pl.when(ci == pl.num_programs(0) - 1)
    def _():
        for b in range(batch):
            for h in range(H_B):
                s_ref[b, h] = st_scr[b, h].T


def _gla_prompt(zf, log_a, zb, g_head, *, batch, seq, chunk, sub):
    nqk = zf.shape[1]
    nv = H_B * DV_B
    zf3 = zf.reshape(batch, seq, nqk)
    la3 = log_a.reshape(batch, seq, log_a.shape[1])
    zb3 = zb.reshape(batch, seq, zb.shape[1])
    blk = lambda n, col: pl.BlockSpec((batch, chunk, n), lambda c: (0, c, col))
    return pl.pallas_call(
        functools.partial(_gla_prompt_kernel, batch=batch, sub=sub),
        grid=(seq // chunk,),
        in_specs=[blk(nqk, 0), blk(la3.shape[2], 0), blk(nv, 0), blk(nv, 1),
                  pl.BlockSpec(g_head.shape, lambda c: (0, 0))],
        out_specs=(blk(nv, 0),
                   pl.BlockSpec((batch, H_B, DK_B, DV_B), lambda c: (0, 0, 0, 0))),
        out_shape=(jax.ShapeDtypeStruct((batch, seq, nv), BF16),
                   jax.ShapeDtypeStruct((batch, H_B, DK_B, DV_B), F32)),
        scratch_shapes=[pltpu.VMEM((batch, H_B, DV_B, DK_B), F32)],
        compiler_params=_cparams("arbitrary"),
        name="gla_prompt",
    )(zf3, la3, zb3, zb3, g_head)


def _gla_sample_kernel(qk_ref, la_ref, v_ref, r_ref, g_ref, s0_ref, o_ref, s_ref, *, sub):
    nk = H_B * DK_B
    ks = lambda h: slice(h * DK_B, (h + 1) * DK_B)
    vs = lambda h: slice(h * DV_B, (h + 1) * DV_B)
    heads = [(qk_ref[:, ks(h)], qk_ref[:, nk + h * DK_B:nk + (h + 1) * DK_B], la_ref[:, ks(h)],
              v_ref[:, vs(h)], s0_ref[h].T) for h in range(H_B)]
    for h, (o, st_new) in enumerate(_gla_chunk(heads, sub=sub)):
        s_ref[h] = st_new.T
        o_ref[:, vs(h)] = _gla_out(o, g_ref[...], r_ref[:, vs(h)])


def _gla_sample(zf3, la3, zb3, g_head, s0):
    nb, tp, nqk = zf3.shape
    nv = H_B * DV_B
    blk = lambda n, col: pl.BlockSpec((None, tp, n), lambda b: (b, 0, col))
    st = pl.BlockSpec((None, H_B, DK_B, DV_B), lambda b: (b, 0, 0, 0))
    return pl.pallas_call(
        functools.partial(_gla_sample_kernel, sub=tp),
        grid=(nb,),
        in_specs=[blk(nqk, 0), blk(la3.shape[2], 0), blk(nv, 0), blk(nv, 1),
                  pl.BlockSpec(g_head.shape, lambda b: (0, 0)), st],
        out_specs=(blk(nv, 0), st),
        out_shape=(jax.ShapeDtypeStruct((nb, tp, nv), BF16),
                   jax.ShapeDtypeStruct(s0.shape, F32)),
        compiler_params=_cparams("parallel"),
        name="gla_sample",
    )(zf3, la3, zb3, zb3, g_head, s0)


def _merge_kernel(oa_ref, ob_ref, sga_ref, sgb_ref, x_ref, wpd_ref, wpg_ref, wo_ref, g1_ref, g2_ref,
                  x1_ref, h2_ref):
    m = (sga_ref[...].astype(F32) * _dot(oa_ref[...], wpd_ref[...])
         + sgb_ref[...].astype(F32) * _dot(ob_ref[...], wpg_ref[...]))
    mix = _dot(m.astype(BF16), wo_ref[...])
    x1 = x_ref[...] + _rms(mix, g1_ref[...])
    x1_ref[...] = x1
    h2_ref[...] = _rms(x1, g2_ref[...]).astype(BF16)


def _merge(oa, ob, zg, x, wpd, wpg, wo, g1, g2, *, tm):
    m, d = x.shape
    na, nb_ = oa.shape[1], ob.shape[1]
    row = lambda n, col=0: pl.BlockSpec((tm, n), lambda i: (i, col))
    return pl.pallas_call(
        _merge_kernel,
        grid=(m // tm,),
        in_specs=[row(na), row(nb_), row(d, 0), row(d, 1), row(d),
                  _const_spec(wpd.shape), _const_spec(wpg.shape), _const_spec(wo.shape),
                  _const_spec(g1.shape), _const_spec(g2.shape)],
        out_specs=(row(d), row(d)),
        out_shape=(jax.ShapeDtypeStruct((m, d), F32), jax.ShapeDtypeStruct((m, d), BF16)),
        compiler_params=_cparams("parallel"),
        name="merge",
    )(oa, ob, zg, zg, x, wpd, wpg, wo, g1, g2)


def _geglu(cg, cv):
    return (jax.nn.gelu(cg, approximate=True) * cv).astype(BF16)


def _ffn_finish(acc, x1_ref, g_ref, y_ref):
    y_ref[...] = x1_ref[...] + _rms(acc, g_ref[...])


def _ffn_prompt_kernel(hp_ref, h_ref, x1_ref, wug_ref, wuv_ref, wcg_ref, wcv_ref, bcg_ref, bcv_ref,
                       wd_ref, g_ref, y_ref, csg_ref, csv_ref, acc_scr, *, tm, halo, cw, seq_tiles):
    i = pl.program_id(0)
    j = pl.program_id(1)

    @pl.when(j == 0)
    def _():
        acc_scr[...] = jnp.zeros(acc_scr.shape, F32)

    hp = jnp.where(i % seq_tiles == 0, jnp.zeros_like(hp_ref[...]), hp_ref[...])
    rb = tm // 2
    lhs = (jnp.concatenate([hp, h_ref[0:rb]], axis=0), h_ref[rb:tm])
    chunks = [slice(c0, c0 + cw) for c0 in range(0, wug_ref.shape[1], cw)]

    ups = [[(_dot(x, wug_ref[:, cols]), _dot(x, wuv_ref[:, cols])) for cols in chunks] for x in lhs]

    def conv(u, cols, wc_ref, bc_ref):
        return (bc_ref[:, cols] + wc_ref[0:1, cols] * u[halo - 2:halo - 2 + rb]
                + wc_ref[1:2, cols] * u[halo - 1:halo - 1 + rb] + wc_ref[2:3, cols] * u[halo:halo + rb])

    for r in range(2):
        acts = []
        for c, cols in enumerate(chunks):
            ug, uv = ups[r][c]
            if r == 1:
                ug = jnp.concatenate([ups[0][c][0][rb:rb + halo], ug], axis=0)
                uv = jnp.concatenate([ups[0][c][1][rb:rb + halo], uv], axis=0)
                csg_ref[:, cols] = ug[halo + rb - 8:halo + rb]
                csv_ref[:, cols] = uv[halo + rb - 8:halo + rb]
            acts.append(_geglu(conv(ug, cols, wcg_ref, bcg_ref), conv(uv, cols, wcv_ref, bcv_ref)))
        a = jnp.concatenate(acts, axis=1) if len(acts) > 1 else acts[0]
        acc_scr[r * rb:(r + 1) * rb] += _dot(a, wd_ref[...])

    @pl.when(j == pl.num_programs(1) - 1)
    def _():
        _ffn_finish(acc_scr[...], x1_ref, g_ref, y_ref)


def _ffn_prompt(h2, x1, w_up, w_conv, b_conv, w_down, g, *, seq, tm, tf):
    m, d = x1.shape
    d_ff = w_down.shape[0]
    nf = d_ff // tf
    halo = SUBLANES_BF16
    hb = tm // halo
    col = lambda rows, off: pl.BlockSpec((rows, tf), lambda i, j: (0, j + off))
    tok = lambda n: pl.BlockSpec((tm, n), lambda i, j: (i, 0))
    cs = pl.BlockSpec((8, tf), lambda i, j: (i, j))
    cs_shape = jax.ShapeDtypeStruct((m // tm * 8, d_ff), F32)
    cw = min(tf, 2 * LANES)
    return pl.pallas_call(
        functools.partial(_ffn_prompt_kernel, tm=tm, halo=halo, cw=cw, seq_tiles=seq // tm),
        grid=(m // tm, nf),
        in_specs=[pl.BlockSpec((halo, d), lambda i, j: (jnp.maximum(i * hb - 1, 0), 0)),
                  tok(d), tok(d), col(d, 0), col(d, nf), col(CONV_W, 0), col(CONV_W, nf),
                  col(1, 0), col(1, nf), pl.BlockSpec((tf, d), lambda i, j: (j, 0)),
                  pl.BlockSpec(g.shape, lambda i, j: (0, 0))],
        out_specs=(tok(d), cs, cs),
        out_shape=(jax.ShapeDtypeStruct((m, d), F32), cs_shape, cs_shape),
        scratch_shapes=[pltpu.VMEM((tm, d), F32)],
        compiler_params=_cparams("parallel", "arbitrary"),
        name="ffn_prompt",
    )(h2, h2, x1, w_up, w_up, w_conv, w_conv, b_conv, b_conv, w_down, g)


def _ffn_sample_kernel(h_ref, x1_ref, ctxg_ref, ctxv_ref, wug_ref, wuv_ref, wcg_ref, wcv_ref,
                       bcg_ref, bcv_ref, wd_ref, g_ref, y_ref, csg_ref, csv_ref, acc_scr, *, nb):
    j = pl.program_id(0)
    m = h_ref.shape[0]
    h = h_ref[...]

    def conv(w_up_ref, ctx_ref, wc_ref, bc_ref, cs_ref):
        u = _dot(h, w_up_ref[...])
        up = jnp.concatenate([ctx_ref[0], ctx_ref[1], u], axis=0)
        cs_ref[0] = u[m - 2 * nb:m - nb]
        cs_ref[1] = u[m - nb:m]
        return (bc_ref[...] + wc_ref[0:1] * up[0:m] + wc_ref[1:2] * up[nb:nb + m]
                + wc_ref[2:3] * up[2 * nb:2 * nb + m])

    a = _geglu(conv(wug_ref, ctxg_ref, wcg_ref, bcg_ref, csg_ref),
               conv(wuv_ref, ctxv_ref, wcv_ref, bcv_ref, csv_ref))
    part = _dot(a, wd_ref[...])

    @pl.when(j == 0)
    def _():
        acc_scr[...] = part

    @pl.when(j > 0)
    def _():
        acc_scr[...] += part

    @pl.when(j == pl.num_programs(0) - 1)
    def _():
        _ffn_finish(acc_scr[...], x1_ref, g_ref, y_ref)


def _ffn_sample(h2, x1, ctx, w_up, w_conv, b_conv, w_down, g, *, tf):
    m, d = x1.shape
    nb = ctx.shape[1]
    d_ff = w_down.shape[0]
    nf = d_ff // tf
    col = lambda rows, off: pl.BlockSpec((rows, tf), lambda j: (0, j + off))
    full = lambda a: pl.BlockSpec(a.shape, lambda j: (0, 0))
    ctx_spec = lambda off: pl.BlockSpec((CONV_W - 1, nb, tf), lambda j: (0, 0, j + off))
    cs = pl.BlockSpec((CONV_W - 1, nb, tf), lambda j: (0, 0, j))
    cs_shape = jax.ShapeDtypeStruct((CONV_W - 1, nb, d_ff), F32)
    return pl.pallas_call(
        functools.partial(_ffn_sample_kernel, nb=nb),
        grid=(nf,),
        in_specs=[full(h2), full(x1), ctx_spec(0), ctx_spec(nf), col(d, 0), col(d, nf),
                  col(CONV_W, 0), col(CONV_W, nf), col(1, 0), col(1, nf),
                  pl.BlockSpec((tf, d), lambda j: (j, 0)), full(g)],
        out_specs=(full(x1), cs, cs),
        out_shape=(jax.ShapeDtypeStruct((m, d), F32), cs_shape, cs_shape),
        scratch_shapes=[pltpu.VMEM((m, d), F32)],
        compiler_params=_cparams("arbitrary"),
        name="ffn_sample",
    )(h2, x1, ctx, ctx, w_up, w_up, w_conv, w_conv, b_conv, b_conv, w_down, g)


def _pick(n, candidates):
    for c in candidates:
        if n % c == 0:
            return c
    raise ValueError(f"no tile in {candidates} divides {n}")


def _layer_weights(l, p):
    d, n_in = p['w_in'].shape[1:]
    n_attn = p['cache_heads'] * 2 * DH_A
    w = p['w_in'][l]
    o_gla = 3 * n_attn
    o_lr = o_gla + 2 * H_B * DK_B + 2 * H_B * DV_B
    o_gate = o_lr + GATE_RANK
    assert n_in == o_gate + 2 * d
    w_in_b = w.astype(BF16)
    w_gate = w[:, o_gate:].astype(BF16)
    wg = jnp.concatenate([p['w_gla_gate'][l].astype(F32),
                          jnp.zeros((LANES - GATE_RANK, H_B * DK_B), F32)], axis=0)
    wg_hi = wg.astype(BF16)
    wg_lo = (wg - wg_hi.astype(F32)).astype(BF16)
    row = lambda a: a[l].reshape(1, -1).astype(F32)
    return dict(
        w_in=w_in_b, w_gate=w_gate, n_attn=n_attn, o_vr=o_gla + 2 * H_B * DK_B, o_lr=o_lr,
        wg_hi=wg_hi, wg_lo=wg_lo, bg=row(p['b_gla_gate']),
        g_pre_mix=row(p['g_pre_mix']), g_head_diff=row(p['g_head_diff']), g_head_gla=row(p['g_head_gla']),
        wpd=p['w_proj_diff'][l].astype(BF16), wpg=p['w_proj_gla'][l].astype(BF16),
        wo=p['w_out'][l].astype(BF16), g_post_mix=row(p['g_post_mix']), g_pre_ffn=row(p['g_pre_ffn']),
        w_up=p['w_up'][l].astype(BF16), w_conv=p['w_conv'][l].astype(F32), b_conv=row(p['b_conv']),
        w_down=p['w_down'][l].astype(BF16), g_post_ffn=row(p['g_post_ffn']))


def _token_stages(x, lw, *, tm_proj, tm_mix, transposed_v):
    h, *proj = _attn_proj(x, lw['g_pre_mix'], lw['w_in'], lw['o_lr'], lw['wg_hi'], lw['wg_lo'], lw['bg'],
                          n_attn=lw['n_attn'], tm=tm_proj, transposed_v=transposed_v)
    nv = H_B * DV_B
    zb = _act_proj(h, lw['w_in'], lw['o_vr'], 2 * nv, ('id',) * 4 + ('silu',) * 4,
                   tm=tm_mix, tn=2 * nv, name="gla_proj")
    zg = _act_proj(h, lw['w_gate'], 0, lw['w_gate'].shape[1], ('sigmoid',) * 4,
                   tm=tm_mix, tn=nv, name="gate_proj")
    return proj, zb, zg


def kernel(x_prompt, x_sample, cache_k, cache_v, page_table, state_gla, state_conv, rel_bias,
           g_pre_mix, w_in, lambda_q1, lambda_k1, lambda_q2, lambda_k2, g_head_diff,
           w_gla_gate, b_gla_gate, g_head_gla, w_proj_diff, w_proj_gla, w_out, g_post_mix,
           g_pre_ffn, w_up, w_conv, b_conv, w_down, g_post_ffn):
    batch, seq, d = x_prompt.shape
    nb, t_dec, _ = x_sample.shape
    depth, n_pool, page, n_heads, kw = cache_k.shape
    assert page == PAGE_SIZE and kw == 2 * DH_A and cache_v.shape[-1] == DV_A
    d_ff = w_down.shape[1]
    width = n_heads * DV_A
    params = dict(w_in=w_in, w_gla_gate=w_gla_gate, b_gla_gate=b_gla_gate, g_pre_mix=g_pre_mix,
                  g_head_diff=g_head_diff, g_head_gla=g_head_gla, w_proj_diff=w_proj_diff,
                  w_proj_gla=w_proj_gla, w_out=w_out, g_post_mix=g_post_mix, g_pre_ffn=g_pre_ffn,
                  w_up=w_up, w_conv=w_conv, b_conv=b_conv, w_down=w_down, g_post_ffn=g_post_ffn,
                  cache_heads=n_heads)

    mp = batch * seq
    ms = nb * t_dec
    tq = _pick(seq, (512, 256, 128))
    tm_proj = _pick(mp, (256, 128))
    tm_mix = _pick(mp, (1024, 512, 256, 128))
    tm_ffn = _pick(seq, (512, 256, 128))
    tf = _pick(d_ff, (512, 256, 128))
    chunk = _pick(seq, (GLA_CHUNK,))
    pps = _pick(page_table.shape[1], (16, 8, 4, 2, 1))
    t_pad = SUBLANES_BF16

    page_table = page_table.astype(jnp.int32)
    eye_2 = jnp.eye(2, dtype=BF16)

    yp = x_prompt.reshape(mp, d)
    ys = x_sample.reshape(ms, d)
    outs = [[] for _ in range(8)]
    for l in range(depth):
        lam0 = _lambda_init(l)
        lw = _layer_weights(l, params)
        toe, sb_far, sb_last, sb_new, lam = _bias_tables(rel_bias, lambda_q1[l], lambda_k1[l], lambda_q2[l],
                                                 lambda_k2[l], n_heads=n_heads, t_dec=t_dec, lam_init=lam0)

        (qkv, k32, v32, zf, log_a, vt), zb, zg = _token_stages(yp, lw, tm_proj=tm_proj, tm_mix=tm_mix,
                                                               transposed_v=True)
        oa = _attn_prompt(qkv, vt, toe, lam, lw['g_head_diff'], batch=batch, seq=seq, n_heads=n_heads,
                          tq=tq, lam_init=lam0)
        ob, s_p = _gla_prompt(zf, log_a, zb, lw['g_head_gla'], batch=batch, seq=seq, chunk=chunk,
                              sub=GLA_SUB)
        x1, h2 = _merge(oa, ob.reshape(mp, -1), zg, yp, lw['wpd'], lw['wpg'], lw['wo'],
                        lw['g_post_mix'], lw['g_pre_ffn'], tm=tm_proj)
        yp, csg, csv = _ffn_prompt(h2, x1, lw['w_up'], lw['w_conv'], lw['b_conv'], lw['w_down'],
                                   lw['g_post_ffn'], seq=seq, tm=tm_ffn, tf=tf)
        cs = jnp.concatenate([csg, csv], axis=1).reshape(batch, seq // tm_ffn, 8, 2 * d_ff)
        outs[0].append(k32.reshape(batch, seq, n_heads, 2 * DH_A))
        outs[1].append(v32.reshape(batch, seq, n_heads, DV_A))
        outs[2].append(s_p)
        outs[3].append(cs[:, -1, 8 - (CONV_W - 1):])

        (qkv, k32, v32, zf, log_a), zb, zg = _token_stages(ys, lw, tm_proj=ms, tm_mix=ms,
                                                           transposed_v=False)
        q5 = qkv[:, :width].reshape(nb, t_dec, n_heads, 2, DH_A)
        q_rows = jnp.einsum('bthnd,mn->bmthnd', q5, eye_2).reshape(nb, 2 * t_dec * n_heads, 2 * DH_A)
        per_head = lambda a: a.reshape(nb, t_dec * n_heads, -1)
        pad_t = lambda a: jnp.pad(a.reshape(nb, t_dec, -1), ((0, 0), (0, t_pad - t_dec), (0, 0)))
        oa = _attn_sample(q_rows, per_head(qkv[:, width:2 * width]), per_head(qkv[:, 2 * width:]),
                          cache_k, cache_v, page_table, l, sb_far, sb_last, sb_new, lam,
                          lw['g_head_diff'], pps=pps, lam_init=lam0)
        ob, s_s = _gla_sample(pad_t(zf), pad_t(log_a), pad_t(zb), lw['g_head_gla'], state_gla[l])
        x1, h2 = _merge(oa.reshape(ms, -1), ob[:, :t_dec].reshape(ms, -1), zg, ys, lw['wpd'], lw['wpg'],
                        lw['wo'], lw['g_post_mix'], lw['g_pre_ffn'], tm=ms)
        tmajor = lambda a: a.reshape(nb, t_dec, -1).transpose(1, 0, 2).reshape(ms, -1)
        y_t, csg, csv = _ffn_sample(tmajor(h2), tmajor(x1), state_conv[l].transpose(1, 0, 2),
                                    lw['w_up'], lw['w_conv'], lw['b_conv'], lw['w_down'],
                                    lw['g_post_ffn'], tf=tf)
        ys = y_t.reshape(t_dec, nb, d).transpose(1, 0, 2).reshape(ms, d)
        outs[4].append(k32.reshape(nb, t_dec, n_heads, 2 * DH_A))
        outs[5].append(v32.reshape(nb, t_dec, n_heads, DV_A))
        outs[6].append(s_s)
        outs[7].append(jnp.concatenate([csg, csv], axis=2).transpose(1, 0, 2))

    return (yp.reshape(batch, seq, d), ys.reshape(nb, t_dec, d)) + tuple(jnp.stack(o) for o in outs)
```

```python
import functools
import math

import numpy as np
import jax
import jax.numpy as jnp
from jax import lax
from jax.experimental import pallas as pl
from jax.experimental.pallas import tpu as pltpu

F32 = jnp.float32
BF16 = jnp.bfloat16

DH_A = 64
DV_A = 128
DK_B = 128
DV_B = 256
H_B = 4
GATE_RANK = 16
GATE_TAU = 16.0
NUM_BUCKETS = 32
MAX_DISTANCE = 128
CONV_W = 3
PAGE_SIZE = 128
EPS = 1e-6
NEG_INF = -1e30
LOG2_E = math.log2(math.e)

LANES = 128
SUBLANES_BF16 = 16
VT_ROWS = DV_A + SUBLANES_BF16
VMEM_LIMIT_BYTES = 56 * 1024 * 1024

GLA_CHUNK = 64
GLA_SUB = 16


def _lambda_init(layer):
    return 0.8 - 0.6 * math.exp(-0.3 * layer)


def _cparams(*sem):
    return pltpu.CompilerParams(dimension_semantics=sem, vmem_limit_bytes=VMEM_LIMIT_BYTES)


def _const_spec(shape):
    nd = len(shape)
    return pl.BlockSpec(shape, lambda *_: (0,) * nd, pipeline_mode=pl.Buffered(1))


def _rms(xf, g):
    return xf * lax.rsqrt(jnp.mean(xf * xf, axis=-1, keepdims=True) + EPS) * g


def _dot(a, b):
    return jnp.dot(a, b, preferred_element_type=F32)


def _dot_nt(a, b):
    return lax.dot_general(a, b, (((1,), (1,)), ((), ())), preferred_element_type=F32)


def _dot_tn(a, b):
    return lax.dot_general(a, b, (((0,), (0,)), ((), ())), preferred_element_type=F32)


def _split_bf16(x):
    hi = x.astype(BF16)
    lo = (x - hi.astype(F32)).astype(BF16)
    return hi, lo


def _t5_bucket_np(rel):
    n = np.maximum(rel, 0)
    max_exact = NUM_BUCKETS // 2
    nf = np.maximum(n, 1).astype(np.float32)
    large = max_exact + (np.log(nf / max_exact) / math.log(MAX_DISTANCE / max_exact)
                         * (NUM_BUCKETS - max_exact)).astype(np.int32)
    large = np.minimum(large, NUM_BUCKETS - 1)
    return np.where(n < max_exact, n, large).astype(np.int32)


def _bias_tables_kernel(relb_ref, bk_toe_ref, bk_far_ref, bk_last_ref, bk_new_ref,
                        lq1_ref, lk1_ref, lq2_ref, lk2_ref,
                        toe_ref, sb_far_ref, sb_last_ref, sb_new_ref, lam_ref, *, n_heads, lam_init):
    far = NUM_BUCKETS - 1

    def lookup(bk, h):
        out = jnp.full(bk.shape, NEG_INF, F32)
        for b in range(NUM_BUCKETS):
            out = jnp.where(bk == b, relb_ref[b, h] - relb_ref[far, h], out)
        return out

    for h in range(n_heads):
        toe_ref[0, h] = lookup(bk_toe_ref[0], h)
        toe_ref[1, h] = lookup(bk_toe_ref[1], h)

    def per_row_head(bk_ref, out_ref):
        row_h = lax.broadcasted_iota(jnp.int32, bk_ref.shape, 0) % n_heads
        out = jnp.zeros(bk_ref.shape, F32)
        for h in range(n_heads):
            out = jnp.where(row_h == h, lookup(bk_ref[...], h), out)
        out_ref[...] = -out

    per_row_head(bk_far_ref, sb_far_ref)
    per_row_head(bk_last_ref, sb_last_ref)
    per_row_head(bk_new_ref, sb_new_ref)
    d1 = jnp.sum(lq1_ref[...] * lk1_ref[...], axis=-1, keepdims=True)
    d2 = jnp.sum(lq2_ref[...] * lk2_ref[...], axis=-1, keepdims=True)
    lam = jnp.exp(d1) - jnp.exp(d2) + lam_init
    lam_ref[...] = jnp.broadcast_to(lam, lam_ref.shape)


def _bias_tables(rel_bias, lq1, lk1, lq2, lk2, *, n_heads, t_dec, lam_init):
    i = np.arange(LANES)[:, None]
    j = np.arange(LANES)[None, :]
    bk_toe = np.stack([np.where(j >= i, _t5_bucket_np(j - i), -1),
                       _t5_bucket_np(LANES + j - i)]).astype(np.int32)
    rows = 2 * t_dec * n_heads
    r = np.arange(rows)[:, None]
    t, h = (r // n_heads) % t_dec, r % n_heads
    c = np.arange(PAGE_SIZE * n_heads)[None, :]
    key, hk = c // n_heads, c % n_heads
    bk_far = np.where(hk == h, NUM_BUCKETS - 1, -1).astype(np.int32)
    bk_last = np.where(hk == h, _t5_bucket_np(PAGE_SIZE + t - key), -1).astype(np.int32)
    cn = c[:, :LANES]
    key, hk = cn // n_heads, cn % n_heads
    bk_new = np.where((hk == h) & (key <= t) & (key < t_dec), _t5_bucket_np(t - key), -1).astype(np.int32)
    vec = lambda a: a.reshape(1, -1).astype(F32)
    smem = pl.BlockSpec(memory_space=pltpu.SMEM)
    tab = lambda a: jax.ShapeDtypeStruct(a.shape, F32)
    return pl.pallas_call(
        functools.partial(_bias_tables_kernel, n_heads=n_heads, lam_init=lam_init),
        out_shape=(jax.ShapeDtypeStruct((2, n_heads, LANES, LANES), F32),
                   tab(bk_far), tab(bk_last), tab(bk_new),
                   jax.ShapeDtypeStruct((8, LANES), F32)),
        in_specs=[smem] + [pl.BlockSpec(memory_space=pltpu.VMEM)] * 8,
        name="bias_tables",
    )(rel_bias.astype(F32), jnp.asarray(bk_toe), jnp.asarray(bk_far), jnp.asarray(bk_last),
      jnp.asarray(bk_new), vec(lq1), vec(lk1), vec(lq2), vec(lk2))


def _attn_proj_kernel(x_ref, g_ref, w_ref, wlr_ref, wgh_ref, wgl_ref, bg_ref,
                      h_ref, qkv_ref, k32_ref, v32_ref, zf_ref, loga_ref, *maybe_vt_ref, n_attn, n_qk):
    h = _rms(x_ref[...], g_ref[...]).astype(BF16)
    h_ref[...] = h
    q = _dot(h, w_ref[:, 0:n_attn]) * DH_A ** -0.5
    qkv_ref[:, 0:n_attn] = q.astype(BF16)
    k = _dot(h, w_ref[:, n_attn:2 * n_attn])
    k32_ref[...] = k
    qkv_ref[:, n_attn:2 * n_attn] = k.astype(BF16)
    v = _dot(h, w_ref[:, 2 * n_attn:3 * n_attn])
    v32_ref[...] = v
    qkv_ref[:, 2 * n_attn:3 * n_attn] = v.astype(BF16)
    for vt_ref in maybe_vt_ref:
        vt = v.T.astype(BF16)
        ones = jnp.ones((VT_ROWS - DV_A, vt.shape[1]), BF16)
        vt_ref[...] = jnp.concatenate(
            [part for h in range(n_attn // DV_A) for part in (vt[h * DV_A:(h + 1) * DV_A], ones)], axis=0)
    zf_ref[:, 0:n_qk] = _dot(h, w_ref[:, 3 * n_attn:3 * n_attn + n_qk]) * DK_B ** -0.5
    zf_ref[:, n_qk:2 * n_qk] = _dot(h, w_ref[:, 3 * n_attn + n_qk:3 * n_attn + 2 * n_qk])
    lr = _dot(h, wlr_ref[...])
    lr_hi, lr_lo = _split_bf16(lr)
    pre = (_dot(lr_hi, wgh_ref[...]) + _dot(lr_hi, wgl_ref[...]) + _dot(lr_lo, wgh_ref[...])
           + bg_ref[...])
    log_sig = jnp.minimum(pre, 0.0) - jnp.log1p(jnp.exp(-jnp.abs(pre)))
    loga_ref[...] = log_sig / GATE_TAU


def _attn_proj(x, g, w_in, o_lr, wgh, wgl, bg, *, n_attn, tm, transposed_v):
    m, d = x.shape
    n_qk = H_B * DK_B
    n_lead = 3 * n_attn + 2 * n_qk
    assert o_lr % LANES == 0
    row = lambda n: pl.BlockSpec((tm, n), lambda i: (i, 0))
    out_specs = [row(d), row(3 * n_attn), row(n_attn), row(n_attn), row(2 * n_qk), row(n_qk)]
    out_shape = [jax.ShapeDtypeStruct((m, d), BF16),
                 jax.ShapeDtypeStruct((m, 3 * n_attn), BF16),
                 jax.ShapeDtypeStruct((m, n_attn), F32),
                 jax.ShapeDtypeStruct((m, n_attn), F32),
                 jax.ShapeDtypeStruct((m, 2 * n_qk), F32),
                 jax.ShapeDtypeStruct((m, n_qk), F32)]
    if transposed_v:
        vt_rows = n_attn // DV_A * VT_ROWS
        out_specs.append(pl.BlockSpec((vt_rows, tm), lambda i: (0, i)))
        out_shape.append(jax.ShapeDtypeStruct((vt_rows, m), BF16))
    once = pl.Buffered(1)
    return pl.pallas_call(
        functools.partial(_attn_proj_kernel, n_attn=n_attn, n_qk=n_qk),
        grid=(m // tm,),
        in_specs=[row(d), _const_spec(g.shape),
                  pl.BlockSpec((d, n_lead), lambda i: (0, 0), pipeline_mode=once),
                  pl.BlockSpec((d, LANES), lambda i: (0, o_lr // LANES), pipeline_mode=once),
                  _const_spec(wgh.shape), _const_spec(wgl.shape), _const_spec(bg.shape)],
        out_specs=tuple(out_specs),
        out_shape=tuple(out_shape),
        compiler_params=_cparams("parallel"),
        name="attn_proj",
    )(x, g, w_in, w_in, wgh, wgl, bg)


def _sigmoid(x):
    return 0.5 * jnp.tanh(0.5 * x) + 0.5


_ACTS = {
    'id': lambda x: x,
    'silu': lambda x: x * _sigmoid(x),
    'sigmoid': _sigmoid,
}


def _act_proj_kernel(h_ref, w_ref, o_ref, *, acts, cw):
    h = h_ref[...]
    accs = [_dot(h, w_ref[:, c * cw:(c + 1) * cw]) for c in range(len(acts))]
    for c, (act, acc) in enumerate(zip(acts, accs)):
        o_ref[:, c * cw:(c + 1) * cw] = _ACTS[act](acc).astype(BF16)


def _act_proj(h, w, col0, n, acts, *, tm, tn, name):
    m, d = h.shape
    cw = tn // len(acts)
    assert col0 % tn == 0 and n % tn == 0 and cw % LANES == 0
    j0 = col0 // tn
    nj = n // tn
    w_mode = dict(pipeline_mode=pl.Buffered(1)) if nj == 1 else {}
    return pl.pallas_call(
        functools.partial(_act_proj_kernel, acts=acts, cw=cw),
        grid=(m // tm, nj),
        in_specs=[pl.BlockSpec((tm, d), lambda i, j: (i, 0)),
                  pl.BlockSpec((d, tn), lambda i, j: (0, j0 + j), **w_mode)],
        out_specs=pl.BlockSpec((tm, tn), lambda i, j: (i, j)),
        out_shape=jax.ShapeDtypeStruct((m, n), BF16),
        compiler_params=_cparams("parallel", "arbitrary"),
        name=name,
    )(h, w)


def _softmax_step(s, v, m_scr, l_scr, acc_scr):
    m_prev = m_scr[...]
    m_new = jnp.maximum(m_prev, jnp.max(s, axis=-1, keepdims=True))
    alpha = jnp.exp(m_prev - m_new)
    p = jnp.exp(s - m_new)
    l_scr[...] = alpha * l_scr[...] + jnp.sum(p, axis=-1, keepdims=True)
    acc_scr[...] = alpha * acc_scr[...] + _dot(p.astype(BF16), v)
    m_scr[...] = m_new


def _head_norm(o, g, lam_init):
    return o * lax.rsqrt(jnp.mean(o * o, axis=-1, keepdims=True) + EPS) * g * (1.0 - lam_init)


def _attn_prompt_kernel(lam_ref, q_ref, k_ref, vt_ref, toe_ref, g_ref, o_ref,
                        bias_scr, *scr, tq, cb, ahead, lam_init):
    qi = pl.program_id(2)
    nb = tq // LANES

    @pl.when(qi == 0)
    def _():
        for r in range(nb):
            for c in range(nb):
                blk = (slice(r * LANES, (r + 1) * LANES), slice(c * LANES, (c + 1) * LANES))
                for t, d in ((0, tq + (c - r) * LANES), (1, (c - r) * LANES)):
                    if d < 0:
                        bias_scr[(t,) + blk] = jnp.full((LANES, LANES), -NEG_INF, F32)
                    elif d == 0:
                        bias_scr[(t,) + blk] = toe_ref[0] * -LOG2_E
                    elif d == LANES:
                        bias_scr[(t,) + blk] = toe_ref[1] * -LOG2_E
                    else:
                        bias_scr[(t,) + blk] = jnp.zeros((LANES, LANES), F32)

    n_cb = 2 * tq // cb
    chains = [scr[3 * c:3 * c + 3] for c in range(n_cb)]
    q = (q_ref[...].astype(F32) * LOG2_E).astype(BF16)
    lane = lax.broadcasted_iota(jnp.int32, (cb, q.shape[1]), 1)
    for c, (q2_scr, m_scr, acc_scr) in enumerate(chains):
        q_blk = q[(c * cb) % tq:(c * cb) % tq + cb]
        keep = (lane < DH_A) if c * cb < tq else (lane >= DH_A)
        q2_scr[...] = jnp.where(keep, q_blk, jnp.zeros_like(q_blk))
        m_scr[...] = jnp.full(m_scr.shape, NEG_INF, F32)
        acc_scr[...] = jnp.zeros(acc_scr.shape, F32)

    def kv_tiles(js, bias_idx):
        offs = [pl.multiple_of(j * tq, tq) for j in js]
        ks = [k_ref[pl.ds(off, tq), :] for off in offs]
        vts = [vt_ref[:, pl.ds(off, tq)] for off in offs]
        bias = None if bias_idx is None else [
            [bias_scr[bi, :, b0:b0 + cb] for b0 in range(0, tq, cb)] for bi in bias_idx]
        blocks = [(t, c) for t in range(len(js)) for c in range(n_cb)]
        score = lambda t, c: _dot_nt(ks[t], chains[c][0][...])
        pending = [score(t, c) for t, c in blocks[:ahead]]
        for n, (t, c) in enumerate(blocks):
            _, m_scr, acc_scr = chains[c]
            s = pending.pop(0)
            if n + ahead < len(blocks):
                pending.append(score(*blocks[n + ahead]))
            if bias is not None:
                s = s - bias[t][c % len(bias[t])]
            m_prev = m_scr[...]
            m_new = jnp.maximum(m_prev, jnp.max(s, axis=0, keepdims=True))
            alpha = jnp.exp2(m_prev - m_new)
            p = jnp.exp2((s - m_new).astype(BF16))
            acc_scr[...] = alpha * acc_scr[...] + _dot(vts[t], p)
            m_scr[...] = m_new

    def far_pair(i, carry):
        kv_tiles([2 * i, 2 * i + 1], None)
        return carry

    def far_tile(j, carry):
        kv_tiles([j], None)
        return carry

    n_far = jnp.maximum(qi - 1, 0)
    lax.fori_loop(0, n_far // 2, far_pair, 0)
    lax.fori_loop(n_far // 2 * 2, n_far, far_tile, 0)

    @pl.when(qi == 0)
    def _():
        kv_tiles([0], [1])

    @pl.when(qi > 0)
    def _():
        kv_tiles([qi - 1, qi], [0, 1])

    on = jnp.concatenate([acc_scr[0:DV_A] / acc_scr[DV_A:DV_A + 1] for _, _, acc_scr in chains], axis=1)
    lam = lam_ref[0:1, 0:1]
    o_t = on[:, 0:tq] - lam * on[:, tq:2 * tq]
    o_t = o_t * lax.rsqrt(jnp.mean(o_t * o_t, axis=0, keepdims=True) + EPS)
    o_ref[...] = (o_t.T * (g_ref[...] * (1.0 - lam_init))).astype(BF16)


def _attn_prompt(qkv, vt, toe, lam, g_head, *, batch, seq, n_heads, tq, lam_init):
    nq = seq // tq
    koff = n_heads
    cb = min(tq, 2 * LANES)
    return pl.pallas_call(
        functools.partial(_attn_prompt_kernel, tq=tq, cb=cb, ahead=4, lam_init=lam_init),
        grid=(batch, n_heads, nq),
        in_specs=[pl.BlockSpec(lam.shape, lambda b, h, i: (0, 0)),
                  pl.BlockSpec((tq, LANES), lambda b, h, i: (b * nq + i, h)),
                  pl.BlockSpec((seq, LANES), lambda b, h, i: (b, koff + h)),
                  pl.BlockSpec((VT_ROWS, seq), lambda b, h, i: (h, b)),
                  pl.BlockSpec((2, None, LANES, LANES), lambda b, h, i: (0, h, 0, 0)),
                  pl.BlockSpec(g_head.shape, lambda b, h, i: (0, 0))],
        out_specs=pl.BlockSpec((tq, LANES), lambda b, h, i: (b * nq + i, h)),
        out_shape=jax.ShapeDtypeStruct((batch * seq, n_heads * LANES), BF16),
        scratch_shapes=[pltpu.VMEM((2, tq, tq), F32)]
        + [pltpu.VMEM((cb, LANES), BF16), pltpu.VMEM((1, cb), F32),
           pltpu.VMEM((VT_ROWS, cb), F32)] * (2 * tq // cb),
        compiler_params=_cparams("parallel", "parallel", "arbitrary"),
        name="attn_prompt",
    )(lam, qkv, qkv, vt, toe, g_head)


def _attn_sample_kernel(pt_ref, lam_ref, q_ref, kn_ref, vn_ref, sbf_ref, sbl_ref, sbn_ref, g_ref, *rest,
                        pps, bp, lam_init):
    k_refs = rest[:pps]
    v_refs = rest[pps:2 * pps]
    o_ref = rest[2 * pps]
    m_scr, l_scr, acc_scr = rest[2 * pps + 1:]
    g = pl.program_id(1)
    last = pl.num_programs(1) - 1

    @pl.when(g == 0)
    def _():
        m_scr[...] = jnp.full(m_scr.shape, NEG_INF, F32)
        l_scr[...] = jnp.zeros(l_scr.shape, F32)
        acc_scr[...] = jnp.zeros(acc_scr.shape, F32)

    q = q_ref[...]
    cols = sbf_ref.shape[1]
    flat = lambda r: r[...].reshape(cols, r.shape[-1]).astype(BF16)
    pages = lambda refs, b: jnp.concatenate([flat(r) for r in refs[b * bp:(b + 1) * bp]], axis=0)
    scores = [_dot_nt(q, pages(k_refs, b)) for b in range(pps // bp)]
    far = sbf_ref[...]
    tail = jnp.where(g == last, sbl_ref[...], far)
    for b, s in enumerate(scores):
        s = jnp.concatenate([s[:, i * cols:(i + 1) * cols] - (far if b * bp + i < pps - 1 else tail)
                             for i in range(bp)], axis=1)
        _softmax_step(s, pages(v_refs, b), m_scr, l_scr, acc_scr)

    @pl.when(g == last)
    def _():
        pad = jnp.zeros((sbn_ref.shape[1] - kn_ref.shape[0], kn_ref.shape[1]), BF16)
        kn = jnp.concatenate([kn_ref[...], pad], axis=0)
        vn = jnp.concatenate([vn_ref[...], pad], axis=0)
        _softmax_step(_dot_nt(q, kn) - sbn_ref[...], vn, m_scr, l_scr, acc_scr)
        on = acc_scr[...] / l_scr[...]
        half = on.shape[0] // 2
        o = on[0:half] - lam_ref[0:1, 0:1] * on[half:2 * half]
        o_ref[...] = _head_norm(o, g_ref[...], lam_init).astype(BF16)


def _attn_sample(q, kn, vn, cache_k, cache_v, page_table, layer, sb_far, sb_last, sb_new, lam, g_head,
                 *, pps, lam_init):
    nb, rows, feat = q.shape
    n_pages = page_table.shape[1]
    steps = n_pages // pps
    seq_spec = lambda a: pl.BlockSpec((None,) + a.shape[1:], lambda b, g, pt: (b, 0, 0))
    full = lambda a: pl.BlockSpec(a.shape, lambda b, g, pt: (0, 0))

    def page_spec(jj):
        return pl.BlockSpec((None, None) + cache_k.shape[2:],
                            lambda b, g, pt: (layer, pt[b, g * pps + jj], 0, 0, 0))

    grid_spec = pltpu.PrefetchScalarGridSpec(
        num_scalar_prefetch=1,
        grid=(nb, steps),
        in_specs=[full(lam), seq_spec(q), seq_spec(kn), seq_spec(vn), full(sb_far), full(sb_last),
                  full(sb_new), full(g_head)] + [page_spec(jj) for jj in range(pps)] * 2,
        out_specs=pl.BlockSpec((None, rows // 2, feat), lambda b, g, pt: (b, 0, 0)),
        scratch_shapes=[pltpu.VMEM((rows, 1), F32), pltpu.VMEM((rows, 1), F32),
                        pltpu.VMEM((rows, feat), F32)],
    )
    return pl.pallas_call(
        functools.partial(_attn_sample_kernel, pps=pps, bp=min(pps, 2), lam_init=lam_init),
        grid_spec=grid_spec,
        out_shape=jax.ShapeDtypeStruct((nb, rows // 2, feat), BF16),
        compiler_params=_cparams("parallel", "arbitrary"),
        name="attn_sample",
    )(page_table, lam, q, kn, vn, sb_far, sb_last, sb_new, g_head, *([cache_k] * pps), *([cache_v] * pps))


def _gla_chunk(heads, *, sub):
    c_len = heads[0][0].shape[0]
    ri = lax.broadcasted_iota(jnp.int32, (c_len, c_len), 0)
    ci = lax.broadcasted_iota(jnp.int32, (c_len, c_len), 1)
    row = lax.broadcasted_iota(jnp.int32, (c_len, 1), 0)
    tril = (ri >= ci).astype(BF16)
    bs = []
    for _, _, log_a, _, _ in heads:
        a_hi, a_lo = _split_bf16(log_a)
        bs.append(_dot(tril, a_hi) + _dot(tril, a_lo))
    outs, states = [], []
    for (q, k, _, v, st), b in zip(heads, bs):
        b_last = b[c_len - 1:c_len]
        outs.append(_dot_nt((q * jnp.exp(b)).astype(BF16), st.astype(BF16)))
        k_out = (k * jnp.exp(b_last - b)).astype(BF16)
        states.append(st * jnp.exp(b_last) + _dot_tn(v, k_out))
    atts = []
    for (q, k, _, _, _), b in zip(heads, bs):
        rows = []
        for i in range(c_len // sub):
            lo, hi = i * sub, (i + 1) * sub
            b_ref = b[lo:lo + 1]
            qh = (q[lo:hi] * jnp.exp(b[lo:hi] - b_ref)).astype(BF16)
            kh = (k * jnp.exp(jnp.where(row < hi, b_ref - b, NEG_INF))).astype(BF16)
            rows.append(_dot_nt(qh, kh))
        att = jnp.concatenate(rows, axis=0) if len(rows) > 1 else rows[0]
        atts.append(jnp.where(ri >= ci, att, 0.0).astype(BF16))
    outs = [o + _dot(att, v) for o, att, (_, _, _, v, _) in zip(outs, atts, heads)]
    return list(zip(outs, states))


def _gla_out(o, g, r):
    return (o * lax.rsqrt(jnp.mean(o * o, axis=-1, keepdims=True) + EPS) * g * r.astype(F32)).astype(BF16)


def _gla_prompt_kernel(qk_ref, la_ref, v_ref, r_ref, g_ref, o_ref, s_ref, st_scr, *, batch, sub):
    ci = pl.program_id(0)

    @pl.when(ci == 0)
    def _():
        st_scr[...] = jnp.zeros(st_scr.shape, F32)

    nk = H_B * DK_B
    ids = [(b, h) for b in range(batch) for h in range(H_B)]
    ks = lambda h: slice(h * DK_B, (h + 1) * DK_B)
    vs = lambda h: slice(h * DV_B, (h + 1) * DV_B)
    heads = [(qk_ref[b, :, ks(h)], qk_ref[b, :, nk + h * DK_B:nk + (h + 1) * DK_B], la_ref[b, :, ks(h)],
              v_ref[b, :, vs(h)], st_scr[b, h]) for b, h in ids]
    for (b, h), (o, st_new) in zip(ids, _gla_chunk(heads, sub=sub)):
        st_scr[b, h] = st_new
        o_ref[b, :, vs(h)] = _gla_out(o, g_ref[...], r_ref[b, :, vs(h)])

    @pl.when(ci == pl.num_programs(0) - 1)
    def _():
        for b in range(batch):
            for h in range(H_B):
                s_ref[b, h] = st_scr[b, h].T


def _gla_prompt(zf, log_a, zb, g_head, *, batch, seq, chunk, sub):
    nqk = zf.shape[1]
    nv = H_B * DV_B
    zf3 = zf.reshape(batch, seq, nqk)
    la3 = log_a.reshape(batch, seq, log_a.shape[1])
    zb3 = zb.reshape(batch, seq, zb.shape[1])
    blk = lambda n, col: pl.BlockSpec((batch, chunk, n), lambda c: (0, c, col))
    return pl.pallas_call(
        functools.partial(_gla_prompt_kernel, batch=batch, sub=sub),
        grid=(seq // chunk,),
        in_specs=[blk(nqk, 0), blk(la3.shape[2], 0), blk(nv, 0), blk(nv, 1),
                  pl.BlockSpec(g_head.shape, lambda c: (0, 0))],
        out_specs=(blk(nv, 0),
                   pl.BlockSpec((batch, H_B, DK_B, DV_B), lambda c: (0, 0, 0, 0))),
        out_shape=(jax.ShapeDtypeStruct((batch, seq, nv), BF16),
                   jax.ShapeDtypeStruct((batch, H_B, DK_B, DV_B), F32)),
        scratch_shapes=[pltpu.VMEM((batch, H_B, DV_B, DK_B), F32)],
        compiler_params=_cparams("arbitrary"),
        name="gla_prompt",
    )(zf3, la3, zb3, zb3, g_head)


def _gla_sample_kernel(qk_ref, la_ref, v_ref, r_ref, g_ref, s0_ref, o_ref, s_ref, *, sub):
    nk = H_B * DK_B
    ks = lambda h: slice(h * DK_B, (h + 1) * DK_B)
    vs = lambda h: slice(h * DV_B, (h + 1) * DV_B)
    heads = [(qk_ref[:, ks(h)], qk_ref[:, nk + h * DK_B:nk + (h + 1) * DK_B], la_ref[:, ks(h)],
              v_ref[:, vs(h)], s0_ref[h].T) for h in range(H_B)]
    for h, (o, st_new) in enumerate(_gla_chunk(heads, sub=sub)):
        s_ref[h] = st_new.T
        o_ref[:, vs(h)] = _gla_out(o, g_ref[...], r_ref[:, vs(h)])


def _gla_sample(zf3, la3, zb3, g_head, s0):
    nb, tp, nqk = zf3.shape
    nv = H_B * DV_B
    blk = lambda n, col: pl.BlockSpec((None, tp, n), lambda b: (b, 0, col))
    st = pl.BlockSpec((None, H_B, DK_B, DV_B), lambda b: (b, 0, 0, 0))
    return pl.pallas_call(
        functools.partial(_gla_sample_kernel, sub=tp),
        grid=(nb,),
        in_specs=[blk(nqk, 0), blk(la3.shape[2], 0), blk(nv, 0), blk(nv, 1),
                  pl.BlockSpec(g_head.shape, lambda b: (0, 0)), st],
        out_specs=(blk(nv, 0), st),
        out_shape=(jax.ShapeDtypeStruct((nb, tp, nv), BF16),
                   jax.ShapeDtypeStruct(s0.shape, F32)),
        compiler_params=_cparams("parallel"),
        name="gla_sample",
    )(zf3, la3, zb3, zb3, g_head, s0)


def _merge_kernel(oa_ref, ob_ref, sga_ref, sgb_ref, x_ref, wpd_ref, wpg_ref, wo_ref, g1_ref, g2_ref,
                  x1_ref, h2_ref):
    m = (sga_ref[...].astype(F32) * _dot(oa_ref[...], wpd_ref[...])
         + sgb_ref[...].astype(F32) * _dot(ob_ref[...], wpg_ref[...]))
    mix = _dot(m.astype(BF16), wo_ref[...])
    x1 = x_ref[...] + _rms(mix, g1_ref[...])
    x1_ref[...] = x1
    h2_ref[...] = _rms(x1, g2_ref[...]).astype(BF16)


def _merge(oa, ob, zg, x, wpd, wpg, wo, g1, g2, *, tm):
    m, d = x.shape
    na, nb_ = oa.shape[1], ob.shape[1]
    row = lambda n, col=0: pl.BlockSpec((tm, n), lambda i: (i, col))
    return pl.pallas_call(
        _merge_kernel,
        grid=(m // tm,),
        in_specs=[row(na), row(nb_), row(d, 0), row(d, 1), row(d),
                  _const_spec(wpd.shape), _const_spec(wpg.shape), _const_spec(wo.shape),
                  _const_spec(g1.shape), _const_spec(g2.shape)],
        out_specs=(row(d), row(d)),
        out_shape=(jax.ShapeDtypeStruct((m, d), F32), jax.ShapeDtypeStruct((m, d), BF16)),
        compiler_params=_cparams("parallel"),
        name="merge",
    )(oa, ob, zg, zg, x, wpd, wpg, wo, g1, g2)


def _geglu(cg, cv):
    return (jax.nn.gelu(cg, approximate=True) * cv).astype(BF16)


def _ffn_finish(acc, x1_ref, g_ref, y_ref):
    y_ref[...] = x1_ref[...] + _rms(acc, g_ref[...])


def _ffn_prompt_kernel(hp_ref, h_ref, x1_ref, wug_ref, wuv_ref, wcg_ref, wcv_ref, bcg_ref, bcv_ref,
                       wd_ref, g_ref, y_ref, csg_ref, csv_ref, acc_scr, *, tm, halo, cw, seq_tiles):
    i = pl.program_id(0)
    j = pl.program_id(1)

    @pl.when(j == 0)
    def _():
        acc_scr[...] = jnp.zeros(acc_scr.shape, F32)

    hp = jnp.where(i % seq_tiles == 0, jnp.zeros_like(hp_ref[...]), hp_ref[...])
    rb = tm // 2
    lhs = (jnp.concatenate([hp, h_ref[0:rb]], axis=0), h_ref[rb:tm])
    chunks = [slice(c0, c0 + cw) for c0 in range(0, wug_ref.shape[1], cw)]

    ups = [[(_dot(x, wug_ref[:, cols]), _dot(x, wuv_ref[:, cols])) for cols in chunks] for x in lhs]

    def conv(u, cols, wc_ref, bc_ref):
        return (bc_ref[:, cols] + wc_ref[0:1, cols] * u[halo - 2:halo - 2 + rb]
                + wc_ref[1:2, cols] * u[halo - 1:halo - 1 + rb] + wc_ref[2:3, cols] * u[halo:halo + rb])

    for r in range(2):
        acts = []
        for c, cols in enumerate(chunks):
            ug, uv = ups[r][c]
            if r == 1:
                ug = jnp.concatenate([ups[0][c][0][rb:rb + halo], ug], axis=0)
                uv = jnp.concatenate([ups[0][c][1][rb:rb + halo], uv], axis=0)
                csg_ref[:, cols] = ug[halo + rb - 8:halo + rb]
                csv_ref[:, cols] = uv[halo + rb - 8:halo + rb]
            acts.append(_geglu(conv(ug, cols, wcg_ref, bcg_ref), conv(uv, cols, wcv_ref, bcv_ref)))
        a = jnp.concatenate(acts, axis=1) if len(acts) > 1 else acts[0]
        acc_scr[r * rb:(r + 1) * rb] += _dot(a, wd_ref[...])

    @pl.when(j == pl.num_programs(1) - 1)
    def _():
        _ffn_finish(acc_scr[...], x1_ref, g_ref, y_ref)


def _ffn_prompt(h2, x1, w_up, w_conv, b_conv, w_down, g, *, seq, tm, tf):
    m, d = x1.shape
    d_ff = w_down.shape[0]
    nf = d_ff // tf
    halo = SUBLANES_BF16
    hb = tm // halo
    col = lambda rows, off: pl.BlockSpec((rows, tf), lambda i, j: (0, j + off))
    tok = lambda n: pl.BlockSpec((tm, n), lambda i, j: (i, 0))
    cs = pl.BlockSpec((8, tf), lambda i, j: (i, j))
    cs_shape = jax.ShapeDtypeStruct((m // tm * 8, d_ff), F32)
    cw = min(tf, 2 * LANES)
    return pl.pallas_call(
        functools.partial(_ffn_prompt_kernel, tm=tm, halo=halo, cw=cw, seq_tiles=seq // tm),
        grid=(m // tm, nf),
        in_specs=[pl.BlockSpec((halo, d), lambda i, j: (jnp.maximum(i * hb - 1, 0), 0)),
                  tok(d), tok(d), col(d, 0), col(d, nf), col(CONV_W, 0), col(CONV_W, nf),
                  col(1, 0), col(1, nf), pl.BlockSpec((tf, d), lambda i, j: (j, 0)),
                  pl.BlockSpec(g.shape, lambda i, j: (0, 0))],
        out_specs=(tok(d), cs, cs),
        out_shape=(jax.ShapeDtypeStruct((m, d), F32), cs_shape, cs_shape),
        scratch_shapes=[pltpu.VMEM((tm, d), F32)],
        compiler_params=_cparams("parallel", "arbitrary"),
        name="ffn_prompt",
    )(h2, h2, x1, w_up, w_up, w_conv, w_conv, b_conv, b_conv, w_down, g)


def _ffn_sample_kernel(h_ref, x1_ref, ctxg_ref, ctxv_ref, wug_ref, wuv_ref, wcg_ref, wcv_ref,
                       bcg_ref, bcv_ref, wd_ref, g_ref, y_ref, csg_ref, csv_ref, acc_scr, *, nb):
    j = pl.program_id(0)
    m = h_ref.shape[0]
    h = h_ref[...]

    def conv(w_up_ref, ctx_ref, wc_ref, bc_ref, cs_ref):
        u = _dot(h, w_up_ref[...])
        up = jnp.concatenate([ctx_ref[0], ctx_ref[1], u], axis=0)
        cs_ref[0] = u[m - 2 * nb:m - nb]
        cs_ref[1] = u[m - nb:m]
        return (bc_ref[...] + wc_ref[0:1] * up[0:m] + wc_ref[1:2] * up[nb:nb + m]
                + wc_ref[2:3] * up[2 * nb:2 * nb + m])

    a = _geglu(conv(wug_ref, ctxg_ref, wcg_ref, bcg_ref, csg_ref),
               conv(wuv_ref, ctxv_ref, wcv_ref, bcv_ref, csv_ref))
    part = _dot(a, wd_ref[...])

    @pl.when(j == 0)
    def _():
        acc_scr[...] = part

    @pl.when(j > 0)
    def _():
        acc_scr[...] += part

    @pl.when(j == pl.num_programs(0) - 1)
    def _():
        _ffn_finish(acc_scr[...], x1_ref, g_ref, y_ref)


def _ffn_sample(h2, x1, ctx, w_up, w_conv, b_conv, w_down, g, *, tf):
    m, d = x1.shape
    nb = ctx.shape[1]
    d_ff = w_down.shape[0]
    nf = d_ff // tf
    col = lambda rows, off: pl.BlockSpec((rows, tf), lambda j: (0, j + off))
    full = lambda a: pl.BlockSpec(a.shape, lambda j: (0, 0))
    ctx_spec = lambda off: pl.BlockSpec((CONV_W - 1, nb, tf), lambda j: (0, 0, j + off))
    cs = pl.BlockSpec((CONV_W - 1, nb, tf), lambda j: (0, 0, j))
    cs_shape = jax.ShapeDtypeStruct((CONV_W - 1, nb, d_ff), F32)
    return pl.pallas_call(
        functools.partial(_ffn_sample_kernel, nb=nb),
        grid=(nf,),
        in_specs=[full(h2), full(x1), ctx_spec(0), ctx_spec(nf), col(d, 0), col(d, nf),
                  col(CONV_W, 0), col(CONV_W, nf), col(1, 0), col(1, nf),
                  pl.BlockSpec((tf, d), lambda j: (j, 0)), full(g)],
        out_specs=(full(x1), cs, cs),
        out_shape=(jax.ShapeDtypeStruct((m, d), F32), cs_shape, cs_shape),
        scratch_shapes=[pltpu.VMEM((m, d), F32)],
        compiler_params=_cparams("arbitrary"),
        name="ffn_sample",
    )(h2, x1, ctx, ctx, w_up, w_up, w_conv, w_conv, b_conv, b_conv, w_down, g)


def _pick(n, candidates):
    for c in candidates:
        if n % c == 0:
            return c
    raise ValueError(f"no tile in {candidates} divides {n}")


def _layer_weights(l, p):
    d, n_in = p['w_in'].shape[1:]
    n_attn = p['cache_heads'] * 2 * DH_A
    w = p['w_in'][l]
    o_gla = 3 * n_attn
    o_lr = o_gla + 2 * H_B * DK_B + 2 * H_B * DV_B
    o_gate = o_lr + GATE_RANK
    assert n_in == o_gate + 2 * d
    w_in_b = jnp.pad(w.astype(BF16), ((0, 0), (0, -n_in % LANES)))
    w_gate = w[:, o_gate:].astype(BF16)
    wg = jnp.concatenate([p['w_gla_gate'][l].astype(F32),
                          jnp.zeros((LANES - GATE_RANK, H_B * DK_B), F32)], axis=0)
    wg_hi = wg.astype(BF16)
    wg_lo = (wg - wg_hi.astype(F32)).astype(BF16)
    row = lambda a: a[l].reshape(1, -1).astype(F32)
    return dict(
        w_in=w_in_b, w_gate=w_gate, n_attn=n_attn, o_vr=o_gla + 2 * H_B * DK_B, o_lr=o_lr,
        wg_hi=wg_hi, wg_lo=wg_lo, bg=row(p['b_gla_gate']),
        g_pre_mix=row(p['g_pre_mix']), g_head_diff=row(p['g_head_diff']), g_head_gla=row(p['g_head_gla']),
        wpd=p['w_proj_diff'][l].astype(BF16), wpg=p['w_proj_gla'][l].astype(BF16),
        wo=p['w_out'][l].astype(BF16), g_post_mix=row(p['g_post_mix']), g_pre_ffn=row(p['g_pre_ffn']),
        w_up=p['w_up'][l].astype(BF16), w_conv=p['w_conv'][l].astype(F32), b_conv=row(p['b_conv']),
        w_down=p['w_down'][l].astype(BF16), g_post_ffn=row(p['g_post_ffn']))


def _token_stages(x, lw, *, tm_proj, tm_mix, transposed_v):
    h, *proj = _attn_proj(x, lw['g_pre_mix'], lw['w_in'], lw['o_lr'], lw['wg_hi'], lw['wg_lo'], lw['bg'],
                          n_attn=lw['n_attn'], tm=tm_proj, transposed_v=transposed_v)
    nv = H_B * DV_B
    zb = _act_proj(h, lw['w_in'], lw['o_vr'], 2 * nv, ('id',) * 4 + ('silu',) * 4,
                   tm=tm_mix, tn=2 * nv, name="gla_proj")
    zg = _act_proj(h, lw['w_gate'], 0, lw['w_gate'].shape[1], ('sigmoid',) * 4,
                   tm=tm_mix, tn=nv, name="gate_proj")
    return proj, zb, zg


def kernel(x_prompt, x_sample, cache_k, cache_v, page_table, state_gla, state_conv, rel_bias,
           g_pre_mix, w_in, lambda_q1, lambda_k1, lambda_q2, lambda_k2, g_head_diff,
           w_gla_gate, b_gla_gate, g_head_gla, w_proj_diff, w_proj_gla, w_out, g_post_mix,
           g_pre_ffn, w_up, w_conv, b_conv, w_down, g_post_ffn):
    batch, seq, d = x_prompt.shape
    nb, t_dec, _ = x_sample.shape
    depth, n_pool, page, n_heads, kw = cache_k.shape
    assert page == PAGE_SIZE and kw == 2 * DH_A and cache_v.shape[-1] == DV_A
    d_ff = w_down.shape[1]
    width = n_heads * DV_A
    params = dict(w_in=w_in, w_gla_gate=w_gla_gate, b_gla_gate=b_gla_gate, g_pre_mix=g_pre_mix,
                  g_head_diff=g_head_diff, g_head_gla=g_head_gla, w_proj_diff=w_proj_diff,
                  w_proj_gla=w_proj_gla, w_out=w_out, g_post_mix=g_post_mix, g_pre_ffn=g_pre_ffn,
                  w_up=w_up, w_conv=w_conv, b_conv=b_conv, w_down=w_down, g_post_ffn=g_post_ffn,
                  cache_heads=n_heads)

    mp = batch * seq
    ms = nb * t_dec
    tq = _pick(seq, (512, 256, 128))
    tm_proj = _pick(mp, (256, 128))
    tm_mix = _pick(mp, (1024, 512, 256, 128))
    tm_ffn = _pick(seq, (512, 256, 128))
    tf = _pick(d_ff, (512, 256, 128))
    chunk = _pick(seq, (GLA_CHUNK,))
    pps = _pick(page_table.shape[1], (16, 8, 4, 2, 1))
    t_pad = SUBLANES_BF16

    page_table = page_table.astype(jnp.int32)
    eye_2 = jnp.eye(2, dtype=BF16)

    yp = x_prompt.reshape(mp, d)
    ys = x_sample.reshape(ms, d)
    outs = [[] for _ in range(8)]
    for l in range(depth):
        lam0 = _lambda_init(l)
        lw = _layer_weights(l, params)
        toe, sb_far, sb_last, sb_new, lam = _bias_tables(rel_bias, lambda_q1[l], lambda_k1[l], lambda_q2[l],
                                                 lambda_k2[l], n_heads=n_heads, t_dec=t_dec, lam_init=lam0)

        (qkv, k32, v32, zf, log_a, vt), zb, zg = _token_stages(yp, lw, tm_proj=tm_proj, tm_mix=tm_mix,
                                                               transposed_v=True)
        oa = _attn_prompt(qkv, vt, toe, lam, lw['g_head_diff'], batch=batch, seq=seq, n_heads=n_heads,
                          tq=tq, lam_init=lam0)
        ob, s_p = _gla_prompt(zf, log_a, zb, lw['g_head_gla'], batch=batch, seq=seq, chunk=chunk,
                              sub=GLA_SUB)
        x1, h2 = _merge(oa, ob.reshape(mp, -1), zg, yp, lw['wpd'], lw['wpg'], lw['wo'],
                        lw['g_post_mix'], lw['g_pre_ffn'], tm=tm_proj)
        yp, csg, csv = _ffn_prompt(h2, x1, lw['w_up'], lw['w_conv'], lw['b_conv'], lw['w_down'],
                                   lw['g_post_ffn'], seq=seq, tm=tm_ffn, tf=tf)
        cs = jnp.concatenate([csg, csv], axis=1).reshape(batch, seq // tm_ffn, 8, 2 * d_ff)
        outs[0].append(k32.reshape(batch, seq, n_heads, 2 * DH_A))
        outs[1].append(v32.reshape(batch, seq, n_heads, DV_A))
        outs[2].append(s_p)
        outs[3].append(cs[:, -1, 8 - (CONV_W - 1):])

        (qkv, k32, v32, zf, log_a), zb, zg = _token_stages(ys, lw, tm_proj=ms, tm_mix=ms,
                                                           transposed_v=False)
        q5 = qkv[:, :width].reshape(nb, t_dec, n_heads, 2, DH_A)
        q_rows = jnp.einsum('bthnd,mn->bmthnd', q5, eye_2).reshape(nb, 2 * t_dec * n_heads, 2 * DH_A)
        per_head = lambda a: a.reshape(nb, t_dec * n_heads, -1)
        pad_t = lambda a: jnp.pad(a.reshape(nb, t_dec, -1), ((0, 0), (0, t_pad - t_dec), (0, 0)))
        oa = _attn_sample(q_rows, per_head(qkv[:, width:2 * width]), per_head(qkv[:, 2 * width:]),
                          cache_k, cache_v, page_table, l, sb_far, sb_last, sb_new, lam,
                          lw['g_head_diff'], pps=pps, lam_init=lam0)
        ob, s_s = _gla_sample(pad_t(zf), pad_t(log_a), pad_t(zb), lw['g_head_gla'], state_gla[l])
        x1, h2 = _merge(oa.reshape(ms, -1), ob[:, :t_dec].reshape(ms, -1), zg, ys, lw['wpd'], lw['wpg'],
                        lw['wo'], lw['g_post_mix'], lw['g_pre_ffn'], tm=ms)
        tmajor = lambda a: a.reshape(nb, t_dec, -1).transpose(1, 0, 2).reshape(ms, -1)
        y_t, csg, csv = _ffn_sample(tmajor(h2), tmajor(x1), state_conv[l].transpose(1, 0, 2),
                                    lw['w_up'], lw['w_conv'], lw['b_conv'], lw['w_down'],
                                    lw['g_post_ffn'], tf=tf)
        ys = y_t.reshape(t_dec, nb, d).transpose(1, 0, 2).reshape(ms, d)
        outs[4].append(k32.reshape(nb, t_dec, n_heads, 2 * DH_A))
        outs[5].append(v32.reshape(nb, t_dec, n_heads, DV_A))
        outs[6].append(s_s)
        outs[7].append(jnp.concatenate([csg, csv], axis=2).transpose(1, 0, 2))

    return (yp.reshape(batch, seq, d), ys.reshape(nb, t_dec, d)) + tuple(jnp.stack(o) for o in outs)
```

```python
import functools
import math

import numpy as np
import jax
import jax.numpy as jnp
from jax import lax
from jax.experimental import pallas as pl
from jax.experimental.pallas import tpu as pltpu

F32 = jnp.float32
BF16 = jnp.bfloat16

DH_A = 64
DV_A = 128
DK_B = 128
DV_B = 256
H_B = 4
GATE_RANK = 16
GATE_TAU = 16.0
NUM_BUCKETS = 32
MAX_DISTANCE = 128
CONV_W = 3
PAGE_SIZE = 128
EPS = 1e-6
NEG_INF = -1e30
LOG2_E = math.log2(math.e)

LANES = 128
SUBLANES_BF16 = 16
VT_ROWS = DV_A + SUBLANES_BF16
VMEM_LIMIT_BYTES = 56 * 1024 * 1024

GLA_CHUNK = 64
GLA_SUB = 16


def _lambda_init(layer):
    return 0.8 - 0.6 * math.exp(-0.3 * layer)


def _cparams(*sem):
    return pltpu.CompilerParams(dimension_semantics=sem, vmem_limit_bytes=VMEM_LIMIT_BYTES)


def _const_spec(shape):
    nd = len(shape)
    return pl.BlockSpec(shape, lambda *_: (0,) * nd, pipeline_mode=pl.Buffered(1))


def _rms(xf, g):
    return xf * lax.rsqrt(jnp.mean(xf * xf, axis=-1, keepdims=True) + EPS) * g


def _dot(a, b):
    return jnp.dot(a, b, preferred_element_type=F32)


def _dot_nt(a, b):
    return lax.dot_general(a, b, (((1,), (1,)), ((), ())), preferred_element_type=F32)


def _dot_tn(a, b):
    return lax.dot_general(a, b, (((0,), (0,)), ((), ())), preferred_element_type=F32)


def _split_bf16(x):
    hi = x.astype(BF16)
    lo = (x - hi.astype(F32)).astype(BF16)
    return hi, lo


def _t5_bucket_np(rel):
    n = np.maximum(rel, 0)
    max_exact = NUM_BUCKETS // 2
    nf = np.maximum(n, 1).astype(np.float32)
    large = max_exact + (np.log(nf / max_exact) / math.log(MAX_DISTANCE / max_exact)
                         * (NUM_BUCKETS - max_exact)).astype(np.int32)
    large = np.minimum(large, NUM_BUCKETS - 1)
    return np.where(n < max_exact, n, large).astype(np.int32)


def _bias_tables_kernel(relb_ref, bk_toe_ref, bk_far_ref, bk_last_ref, bk_new_ref,
                        lq1_ref, lk1_ref, lq2_ref, lk2_ref,
                        toe_ref, sb_far_ref, sb_last_ref, sb_new_ref, lam_ref, *, n_heads, lam_init):
    far = NUM_BUCKETS - 1

    def lookup(bk, h):
        out = jnp.full(bk.shape, NEG_INF, F32)
        for b in range(NUM_BUCKETS):
            out = jnp.where(bk == b, relb_ref[b, h] - relb_ref[far, h], out)
        return out

    for h in range(n_heads):
        toe_ref[0, h] = lookup(bk_toe_ref[0], h)
        toe_ref[1, h] = lookup(bk_toe_ref[1], h)

    def per_row_head(bk_ref, out_ref):
        row_h = lax.broadcasted_iota(jnp.int32, bk_ref.shape, 0) % n_heads
        out = jnp.zeros(bk_ref.shape, F32)
        for h in range(n_heads):
            out = jnp.where(row_h == h, lookup(bk_ref[...], h), out)
        out_ref[...] = -out

    per_row_head(bk_far_ref, sb_far_ref)
    per_row_head(bk_last_ref, sb_last_ref)
    per_row_head(bk_new_ref, sb_new_ref)
    d1 = jnp.sum(lq1_ref[...] * lk1_ref[...], axis=-1, keepdims=True)
    d2 = jnp.sum(lq2_ref[...] * lk2_ref[...], axis=-1, keepdims=True)
    lam = jnp.exp(d1) - jnp.exp(d2) + lam_init
    lam_ref[...] = jnp.broadcast_to(lam, lam_ref.shape)


def _bias_tables(rel_bias, lq1, lk1, lq2, lk2, *, n_heads, t_dec, lam_init):
    i = np.arange(LANES)[:, None]
    j = np.arange(LANES)[None, :]
    bk_toe = np.stack([np.where(j >= i, _t5_bucket_np(j - i), -1),
                       _t5_bucket_np(LANES + j - i)]).astype(np.int32)
    rows = 2 * t_dec * n_heads
    r = np.arange(rows)[:, None]
    t, h = (r // n_heads) % t_dec, r % n_heads
    c = np.arange(PAGE_SIZE * n_heads)[None, :]
    key, hk = c // n_heads, c % n_heads
    bk_far = np.where(hk == h, NUM_BUCKETS - 1, -1).astype(np.int32)
    bk_last = np.where(hk == h, _t5_bucket_np(PAGE_SIZE + t - key), -1).astype(np.int32)
    cn = c[:, :LANES]
    key, hk = cn // n_heads, cn % n_heads
    bk_new = np.where((hk == h) & (key <= t) & (key < t_dec), _t5_bucket_np(t - key), -1).astype(np.int32)
    vec = lambda a: a.reshape(1, -1).astype(F32)
    smem = pl.BlockSpec(memory_space=pltpu.SMEM)
    tab = lambda a: jax.ShapeDtypeStruct(a.shape, F32)
    return pl.pallas_call(
        functools.partial(_bias_tables_kernel, n_heads=n_heads, lam_init=lam_init),
        out_shape=(jax.ShapeDtypeStruct((2, n_heads, LANES, LANES), F32),
                   tab(bk_far), tab(bk_last), tab(bk_new),
                   jax.ShapeDtypeStruct((8, LANES), F32)),
        in_specs=[smem] + [pl.BlockSpec(memory_space=pltpu.VMEM)] * 8,
        name="bias_tables",
    )(rel_bias.astype(F32), jnp.asarray(bk_toe), jnp.asarray(bk_far), jnp.asarray(bk_last),
      jnp.asarray(bk_new), vec(lq1), vec(lk1), vec(lq2), vec(lk2))


def _cast_w_in_kernel(w_ref, wb_ref, wg_ref, *, o_gate):
    x = w_ref[...]
    n_in = x.shape[1]
    wb_ref[:, 0:n_in] = x.astype(BF16)
    wb_ref[:, n_in:] = jnp.zeros((x.shape[0], wb_ref.shape[1] - n_in), BF16)
    wg_ref[...] = x[:, o_gate:].astype(BF16)


def _cast_w_in(w, o_gate):
    d, n_in = w.shape
    n_pad = n_in + (-n_in % LANES)
    tr = _pick(d, (128, 64, 32, SUBLANES_BF16))
    return pl.pallas_call(
        functools.partial(_cast_w_in_kernel, o_gate=o_gate),
        grid=(d // tr,),
        in_specs=[pl.BlockSpec((tr, n_in), lambda i: (i, 0))],
        out_specs=(pl.BlockSpec((tr, n_pad), lambda i: (i, 0)),
                   pl.BlockSpec((tr, n_in - o_gate), lambda i: (i, 0))),
        out_shape=(jax.ShapeDtypeStruct((d, n_pad), BF16),
                   jax.ShapeDtypeStruct((d, n_in - o_gate), BF16)),
        compiler_params=_cparams("parallel"),
        name="cast_w_in",
    )(w)


def _attn_proj_kernel(x_ref, g_ref, w_ref, wlr_ref, wgh_ref, wgl_ref, bg_ref,
                      h_ref, qkv_ref, k32_ref, v32_ref, zf_ref, loga_ref, *maybe_vt_ref, n_attn, n_qk):
    h = _rms(x_ref[...], g_ref[...]).astype(BF16)
    h_ref[...] = h
    q = _dot(h, w_ref[:, 0:n_attn]) * DH_A ** -0.5
    qkv_ref[:, 0:n_attn] = q.astype(BF16)
    k = _dot(h, w_ref[:, n_attn:2 * n_attn])
    k32_ref[...] = k
    qkv_ref[:, n_attn:2 * n_attn] = k.astype(BF16)
    v = _dot(h, w_ref[:, 2 * n_attn:3 * n_attn])
    v32_ref[...] = v
    qkv_ref[:, 2 * n_attn:3 * n_attn] = v.astype(BF16)
    for vt_ref in maybe_vt_ref:
        vt = v.T.astype(BF16)
        ones = jnp.ones((VT_ROWS - DV_A, vt.shape[1]), BF16)
        vt_ref[...] = jnp.concatenate(
            [part for h in range(n_attn // DV_A) for part in (vt[h * DV_A:(h + 1) * DV_A], ones)], axis=0)
    zf_ref[:, 0:n_qk] = _dot(h, w_ref[:, 3 * n_attn:3 * n_attn + n_qk]) * DK_B ** -0.5
    zf_ref[:, n_qk:2 * n_qk] = _dot(h, w_ref[:, 3 * n_attn + n_qk:3 * n_attn + 2 * n_qk])
    lr = _dot(h, wlr_ref[...])
    lr_hi, lr_lo = _split_bf16(lr)
    pre = (_dot(lr_hi, wgh_ref[...]) + _dot(lr_hi, wgl_ref[...]) + _dot(lr_lo, wgh_ref[...])
           + bg_ref[...])
    log_sig = jnp.minimum(pre, 0.0) - jnp.log1p(jnp.exp(-jnp.abs(pre)))
    loga_ref[...] = log_sig / GATE_TAU


def _attn_proj(x, g, w_in, o_lr, wgh, wgl, bg, *, n_attn, tm, transposed_v):
    m, d = x.shape
    n_qk = H_B * DK_B
    n_lead = 3 * n_attn + 2 * n_qk
    assert o_lr % LANES == 0
    row = lambda n: pl.BlockSpec((tm, n), lambda i: (i, 0))
    out_specs = [row(d), row(3 * n_attn), row(n_attn), row(n_attn), row(2 * n_qk), row(n_qk)]
    out_shape = [jax.ShapeDtypeStruct((m, d), BF16),
                 jax.ShapeDtypeStruct((m, 3 * n_attn), BF16),
                 jax.ShapeDtypeStruct((m, n_attn), F32),
                 jax.ShapeDtypeStruct((m, n_attn), F32),
                 jax.ShapeDtypeStruct((m, 2 * n_qk), F32),
                 jax.ShapeDtypeStruct((m, n_qk), F32)]
    if transposed_v:
        vt_rows = n_attn // DV_A * VT_ROWS
        out_specs.append(pl.BlockSpec((vt_rows, tm), lambda i: (0, i)))
        out_shape.append(jax.ShapeDtypeStruct((vt_rows, m), BF16))
    once = pl.Buffered(1)
    return pl.pallas_call(
        functools.partial(_attn_proj_kernel, n_attn=n_attn, n_qk=n_qk),
        grid=(m // tm,),
        in_specs=[row(d), _const_spec(g.shape),
                  pl.BlockSpec((d, n_lead), lambda i: (0, 0), pipeline_mode=once),
                  pl.BlockSpec((d, LANES), lambda i: (0, o_lr // LANES), pipeline_mode=once),
                  _const_spec(wgh.shape), _const_spec(wgl.shape), _const_spec(bg.shape)],
        out_specs=tuple(out_specs),
        out_shape=tuple(out_shape),
        compiler_params=_cparams("parallel"),
        name="attn_proj",
    )(x, g, w_in, w_in, wgh, wgl, bg)


def _sigmoid(x):
    return 0.5 * jnp.tanh(0.5 * x) + 0.5


_ACTS = {
    'id': lambda x: x,
    'silu': lambda x: x * _sigmoid(x),
    'sigmoid': _sigmoid,
}


def _act_proj_kernel(h_ref, w_ref, o_ref, *, acts, cw):
    h = h_ref[...]
    accs = [_dot(h, w_ref[:, c * cw:(c + 1) * cw]) for c in range(len(acts))]
    for c, (act, acc) in enumerate(zip(acts, accs)):
        o_ref[:, c * cw:(c + 1) * cw] = _ACTS[act](acc).astype(BF16)


def _act_proj(h, w, col0, n, acts, *, tm, tn, name):
    m, d = h.shape
    cw = tn // len(acts)
    assert col0 % tn == 0 and n % tn == 0 and cw % LANES == 0
    j0 = col0 // tn
    nj = n // tn
    w_mode = dict(pipeline_mode=pl.Buffered(1)) if nj == 1 else {}
    return pl.pallas_call(
        functools.partial(_act_proj_kernel, acts=acts, cw=cw),
        grid=(m // tm, nj),
        in_specs=[pl.BlockSpec((tm, d), lambda i, j: (i, 0)),
                  pl.BlockSpec((d, tn), lambda i, j: (0, j0 + j), **w_mode)],
        out_specs=pl.BlockSpec((tm, tn), lambda i, j: (i, j)),
        out_shape=jax.ShapeDtypeStruct((m, n), BF16),
        compiler_params=_cparams("parallel", "arbitrary"),
        name=name,
    )(h, w)


def _softmax_step(s, v, m_scr, l_scr, acc_scr):
    m_prev = m_scr[...]
    m_new = jnp.maximum(m_prev, jnp.max(s, axis=-1, keepdims=True))
    alpha = jnp.exp(m_prev - m_new)
    p = jnp.exp(s - m_new)
    l_scr[...] = alpha * l_scr[...] + jnp.sum(p, axis=-1, keepdims=True)
    acc_scr[...] = alpha * acc_scr[...] + _dot(p.astype(BF16), v)
    m_scr[...] = m_new


def _head_norm(o, g, lam_init):
    return o * lax.rsqrt(jnp.mean(o * o, axis=-1, keepdims=True) + EPS) * g * (1.0 - lam_init)


def _attn_prompt_kernel(lam_ref, q_ref, k_ref, vt_ref, toe_ref, g_ref, o_ref,
                        bias_scr, *scr, tq, cb, ahead, lam_init):
    qi = pl.program_id(2)
    nb = tq // LANES

    @pl.when(qi == 0)
    def _():
        for r in range(nb):
            for c in range(nb):
                blk = (slice(r * LANES, (r + 1) * LANES), slice(c * LANES, (c + 1) * LANES))
                for t, d in ((0, tq + (c - r) * LANES), (1, (c - r) * LANES)):
                    if d < 0:
                        bias_scr[(t,) + blk] = jnp.full((LANES, LANES), -NEG_INF, F32)
                    elif d == 0:
                        bias_scr[(t,) + blk] = toe_ref[0] * -LOG2_E
                    elif d == LANES:
                        bias_scr[(t,) + blk] = toe_ref[1] * -LOG2_E
                    else:
                        bias_scr[(t,) + blk] = jnp.zeros((LANES, LANES), F32)

    n_cb = 2 * tq // cb
    chains = [scr[3 * c:3 * c + 3] for c in range(n_cb)]
    q = (q_ref[...].astype(F32) * LOG2_E).astype(BF16)
    lane = lax.broadcasted_iota(jnp.int32, (cb, q.shape[1]), 1)
    for c, (q2_scr, m_scr, acc_scr) in enumerate(chains):
        q_blk = q[(c * cb) % tq:(c * cb) % tq + cb]
        keep = (lane < DH_A) if c * cb < tq else (lane >= DH_A)
        q2_scr[...] = jnp.where(keep, q_blk, jnp.zeros_like(q_blk))
        m_scr[...] = jnp.full(m_scr.shape, NEG_INF, F32)
        acc_scr[...] = jnp.zeros(acc_scr.shape, F32)

    def kv_tiles(js, bias_idx):
        offs = [pl.multiple_of(j * tq, tq) for j in js]
        ks = [k_ref[pl.ds(off, tq), :] for off in offs]
        vts = [vt_ref[:, pl.ds(off, tq)] for off in offs]
        bias = None if bias_idx is None else [
            [bias_scr[bi, :, b0:b0 + cb] for b0 in range(0, tq, cb)] for bi in bias_idx]
        blocks = [(t, c) for t in range(len(js)) for c in range(n_cb)]
        score = lambda t, c: _dot_nt(ks[t], chains[c][0][...])
        pending = [score(t, c) for t, c in blocks[:ahead]]
        for n, (t, c) in enumerate(blocks):
            _, m_scr, acc_scr = chains[c]
            s = pending.pop(0)
            if n + ahead < len(blocks):
                pending.append(score(*blocks[n + ahead]))
            if bias is not None:
                s = s - bias[t][c % len(bias[t])]
            m_prev = m_scr[...]
            m_new = jnp.maximum(m_prev, jnp.max(s, axis=0, keepdims=True))
            alpha = jnp.exp2(m_prev - m_new)
            p = jnp.exp2((s - m_new).astype(BF16))
            acc_scr[...] = alpha * acc_scr[...] + _dot(vts[t], p)
            m_scr[...] = m_new

    def far_pair(i, carry):
        kv_tiles([2 * i, 2 * i + 1], None)
        return carry

    def far_tile(j, carry):
        kv_tiles([j], None)
        return carry

    n_far = jnp.maximum(qi - 1, 0)
    lax.fori_loop(0, n_far // 2, far_pair, 0)
    lax.fori_loop(n_far // 2 * 2, n_far, far_tile, 0)

    @pl.when(qi == 0)
    def _():
        kv_tiles([0], [1])

    @pl.when(qi > 0)
    def _():
        kv_tiles([qi - 1, qi], [0, 1])

    on = jnp.concatenate([acc_scr[0:DV_A] / acc_scr[DV_A:DV_A + 1] for _, _, acc_scr in chains], axis=1)
    lam = lam_ref[0:1, 0:1]
    o_t = on[:, 0:tq] - lam * on[:, tq:2 * tq]
    o_t = o_t * lax.rsqrt(jnp.mean(o_t * o_t, axis=0, keepdims=True) + EPS)
    o_ref[...] = (o_t.T * (g_ref[...] * (1.0 - lam_init))).astype(BF16)


def _attn_prompt(qkv, vt, toe, lam, g_head, *, batch, seq, n_heads, tq, lam_init):
    nq = seq // tq
    koff = n_heads
    cb = min(tq, 2 * LANES)
    return pl.pallas_call(
        functools.partial(_attn_prompt_kernel, tq=tq, cb=cb, ahead=4, lam_init=lam_init),
        grid=(batch, n_heads, nq),
        in_specs=[pl.BlockSpec(lam.shape, lambda b, h, i: (0, 0)),
                  pl.BlockSpec((tq, LANES), lambda b, h, i: (b * nq + i, h)),
                  pl.BlockSpec((seq, LANES), lambda b, h, i: (b, koff + h)),
                  pl.BlockSpec((VT_ROWS, seq), lambda b, h, i: (h, b)),
                  pl.BlockSpec((2, None, LANES, LANES), lambda b, h, i: (0, h, 0, 0)),
                  pl.BlockSpec(g_head.shape, lambda b, h, i: (0, 0))],
        out_specs=pl.BlockSpec((tq, LANES), lambda b, h, i: (b * nq + i, h)),
        out_shape=jax.ShapeDtypeStruct((batch * seq, n_heads * LANES), BF16),
        scratch_shapes=[pltpu.VMEM((2, tq, tq), F32)]
        + [pltpu.VMEM((cb, LANES), BF16), pltpu.VMEM((1, cb), F32),
           pltpu.VMEM((VT_ROWS, cb), F32)] * (2 * tq // cb),
        compiler_params=_cparams("parallel", "parallel", "arbitrary"),
        name="attn_prompt",
    )(lam, qkv, qkv, vt, toe, g_head)


def _attn_sample_kernel(pt_ref, lam_ref, q_ref, kn_ref, vn_ref, sbf_ref, sbl_ref, sbn_ref, g_ref, *rest,
                        pps, bp, lam_init):
    k_refs = rest[:pps]
    v_refs = rest[pps:2 * pps]
    o_ref = rest[2 * pps]
    m_scr, l_scr, acc_scr = rest[2 * pps + 1:]
    g = pl.program_id(1)
    last = pl.num_programs(1) - 1

    @pl.when(g == 0)
    def _():
        m_scr[...] = jnp.full(m_scr.shape, NEG_INF, F32)
        l_scr[...] = jnp.zeros(l_scr.shape, F32)
        acc_scr[...] = jnp.zeros(acc_scr.shape, F32)

    q = q_ref[...]
    cols = sbf_ref.shape[1]
    flat = lambda r: r[...].reshape(cols, r.shape[-1]).astype(BF16)
    pages = lambda refs, b: jnp.concatenate([flat(r) for r in refs[b * bp:(b + 1) * bp]], axis=0)
    scores = [_dot_nt(q, pages(k_refs, b)) for b in range(pps // bp)]
    far = sbf_ref[...]
    tail = jnp.where(g == last, sbl_ref[...], far)
    for b, s in enumerate(scores):
        s = jnp.concatenate([s[:, i * cols:(i + 1) * cols] - (far if b * bp + i < pps - 1 else tail)
                             for i in range(bp)], axis=1)
        _softmax_step(s, pages(v_refs, b), m_scr, l_scr, acc_scr)

    @pl.when(g == last)
    def _():
        pad = jnp.zeros((sbn_ref.shape[1] - kn_ref.shape[0], kn_ref.shape[1]), BF16)
        kn = jnp.concatenate([kn_ref[...], pad], axis=0)
        vn = jnp.concatenate([vn_ref[...], pad], axis=0)
        _softmax_step(_dot_nt(q, kn) - sbn_ref[...], vn, m_scr, l_scr, acc_scr)
        on = acc_scr[...] / l_scr[...]
        half = on.shape[0] // 2
        o = on[0:half] - lam_ref[0:1, 0:1] * on[half:2 * half]
        o_ref[...] = _head_norm(o, g_ref[...], lam_init).astype(BF16)


def _attn_sample(q, kn, vn, cache_k, cache_v, page_table, layer, sb_far, sb_last, sb_new, lam, g_head,
                 *, pps, lam_init):
    nb, rows, feat = q.shape
    n_pages = page_table.shape[1]
    steps = n_pages // pps
    seq_spec = lambda a: pl.BlockSpec((None,) + a.shape[1:], lambda b, g, pt: (b, 0, 0))
    full = lambda a: pl.BlockSpec(a.shape, lambda b, g, pt: (0, 0))

    def page_spec(jj):
        return pl.BlockSpec((None, None) + cache_k.shape[2:],
                            lambda b, g, pt: (layer, pt[b, g * pps + jj], 0, 0, 0))

    grid_spec = pltpu.PrefetchScalarGridSpec(
        num_scalar_prefetch=1,
        grid=(nb, steps),
        in_specs=[full(lam), seq_spec(q), seq_spec(kn), seq_spec(vn), full(sb_far), full(sb_last),
                  full(sb_new), full(g_head)] + [page_spec(jj) for jj in range(pps)] * 2,
        out_specs=pl.BlockSpec((None, rows // 2, feat), lambda b, g, pt: (b, 0, 0)),
        scratch_shapes=[pltpu.VMEM((rows, 1), F32), pltpu.VMEM((rows, 1), F32),
                        pltpu.VMEM((rows, feat), F32)],
    )
    return pl.pallas_call(
        functools.partial(_attn_sample_kernel, pps=pps, bp=min(pps, 2), lam_init=lam_init),
        grid_spec=grid_spec,
        out_shape=jax.ShapeDtypeStruct((nb, rows // 2, feat), BF16),
        compiler_params=_cparams("parallel", "arbitrary"),
        name="attn_sample",
    )(page_table, lam, q, kn, vn, sb_far, sb_last, sb_new, g_head, *([cache_k] * pps), *([cache_v] * pps))


def _gla_chunk(heads, *, sub):
    c_len = heads[0][0].shape[0]
    ri = lax.broadcasted_iota(jnp.int32, (c_len, c_len), 0)
    ci = lax.broadcasted_iota(jnp.int32, (c_len, c_len), 1)
    row = lax.broadcasted_iota(jnp.int32, (c_len, 1), 0)
    tril = (ri >= ci).astype(BF16)
    bs = []
    for _, _, log_a, _, _ in heads:
        a_hi, a_lo = _split_bf16(log_a)
        bs.append(_dot(tril, a_hi) + _dot(tril, a_lo))
    outs, states = [], []
    for (q, k, _, v, st), b in zip(heads, bs):
        b_last = b[c_len - 1:c_len]
        outs.append(_dot_nt((q * jnp.exp(b)).astype(BF16), st.astype(BF16)))
        k_out = (k * jnp.exp(b_last - b)).astype(BF16)
        states.append(st * jnp.exp(b_last) + _dot_tn(v, k_out))
    atts = []
    for (q, k, _, _, _), b in zip(heads, bs):
        rows = []
        for i in range(c_len // sub):
            lo, hi = i * sub, (i + 1) * sub
            b_ref = b[lo:lo + 1]
            qh = (q[lo:hi] * jnp.exp(b[lo:hi] - b_ref)).astype(BF16)
            kh = (k * jnp.exp(jnp.where(row < hi, b_ref - b, NEG_INF))).astype(BF16)
            rows.append(_dot_nt(qh, kh))
        att = jnp.concatenate(rows, axis=0) if len(rows) > 1 else rows[0]
        atts.append(jnp.where(ri >= ci, att, 0.0).astype(BF16))
    outs = [o + _dot(att, v) for o, att, (_, _, _, v, _) in zip(outs, atts, heads)]
    return list(zip(outs, states))


def _gla_out(o, g, r):
    return (o * lax.rsqrt(jnp.mean(o * o, axis=-1, keepdims=True) + EPS) * g * r.astype(F32)).astype(BF16)


def _gla_prompt_kernel(qk_ref, la_ref, v_ref, r_ref, g_ref, o_ref, s_ref, st_scr, *, batch, sub):
    ci = pl.program_id(0)

    @pl.when(ci == 0)
    def _():
        st_scr[...] = jnp.zeros(st_scr.shape, F32)

    nk = H_B * DK_B
    ids = [(b, h) for b in range(batch) for h in range(H_B)]
    ks = lambda h: slice(h * DK_B, (h + 1) * DK_B)
    vs = lambda h: slice(h * DV_B, (h + 1) * DV_B)
    heads = [(qk_ref[b, :, ks(h)], qk_ref[b, :, nk + h * DK_B:nk + (h + 1) * DK_B], la_ref[b, :, ks(h)],
              v_ref[b, :, vs(h)], st_scr[b, h]) for b, h in ids]
    for (b, h), (o, st_new) in zip(ids, _gla_chunk(heads, sub=sub)):
        st_scr[b, h] = st_new
        o_ref[b, :, vs(h)] = _gla_out(o, g_ref[...], r_ref[b, :, vs(h)])

    @pl.when(ci == pl.num_programs(0) - 1)
    def _():
        for b in range(batch):
            for h in range(H_B):
                s_ref[b, h] = st_scr[b, h].T


def _gla_prompt(zf, log_a, zb, g_head, *, batch, seq, chunk, sub):
    nqk = zf.shape[1]
    nv = H_B * DV_B
    zf3 = zf.reshape(batch, seq, nqk)
    la3 = log_a.reshape(batch, seq, log_a.shape[1])
    zb3 = zb.reshape(batch, seq, zb.shape[1])
    blk = lambda n, col: pl.BlockSpec((batch, chunk, n), lambda c: (0, c, col))
    return pl.pallas_call(
        functools.partial(_gla_prompt_kernel, batch=batch, sub=sub),
        grid=(seq // chunk,),
        in_specs=[blk(nqk, 0), blk(la3.shape[2], 0), blk(nv, 0), blk(nv, 1),
                  pl.BlockSpec(g_head.shape, lambda c: (0, 0))],
        out_specs=(blk(nv, 0),
                   pl.BlockSpec((batch, H_B, DK_B, DV_B), lambda c: (0, 0, 0, 0))),
        out_shape=(jax.ShapeDtypeStruct((batch, seq, nv), BF16),
                   jax.ShapeDtypeStruct((batch, H_B, DK_B, DV_B), F32)),
        scratch_shapes=[pltpu.VMEM((batch, H_B, DV_B, DK_B), F32)],
        compiler_params=_cparams("arbitrary"),
        name="gla_prompt",
    )(zf3, la3, zb3, zb3, g_head)


def _gla_sample_kernel(qk_ref, la_ref, v_ref, r_ref, g_ref, s0_ref, o_ref, s_ref, *, sub):
    nk = H_B * DK_B
    ks = lambda h: slice(h * DK_B, (h + 1) * DK_B)
    vs = lambda h: slice(h * DV_B, (h + 1) * DV_B)
    heads = [(qk_ref[:, ks(h)], qk_ref[:, nk + h * DK_B:nk + (h + 1) * DK_B], la_ref[:, ks(h)],
              v_ref[:, vs(h)], s0_ref[h].T) for h in range(H_B)]
    for h, (o, st_new) in enumerate(_gla_chunk(heads, sub=sub)):
        s_ref[h] = st_new.T
        o_ref[:, vs(h)] = _gla_out(o, g_ref[...], r_ref[:, vs(h)])


def _gla_sample(zf3, la3, zb3, g_head, s0):
    nb, tp, nqk = zf3.shape
    nv = H_B * DV_B
    blk = lambda n, col: pl.BlockSpec((None, tp, n), lambda b: (b, 0, col))
    st = pl.BlockSpec((None, H_B, DK_B, DV_B), lambda b: (b, 0, 0, 0))
    return pl.pallas_call(
        functools.partial(_gla_sample_kernel, sub=tp),
        grid=(nb,),
        in_specs=[blk(nqk, 0), blk(la3.shape[2], 0), blk(nv, 0), blk(nv, 1),
                  pl.BlockSpec(g_head.shape, lambda b: (0, 0)), st],
        out_specs=(blk(nv, 0), st),
        out_shape=(jax.ShapeDtypeStruct((nb, tp, nv), BF16),
                   jax.ShapeDtypeStruct(s0.shape, F32)),
        compiler_params=_cparams("parallel"),
        name="gla_sample",
    )(zf3, la3, zb3, zb3, g_head, s0)


def _merge_kernel(oa_ref, ob_ref, sga_ref, sgb_ref, x_ref, wpd_ref, wpg_ref, wo_ref, g1_ref, g2_ref,
                  x1_ref, h2_ref):
    m = (sga_ref[...].astype(F32) * _dot(oa_ref[...], wpd_ref[...])
         + sgb_ref[...].astype(F32) * _dot(ob_ref[...], wpg_ref[...]))
    mix = _dot(m.astype(BF16), wo_ref[...])
    x1 = x_ref[...] + _rms(mix, g1_ref[...])
    x1_ref[...] = x1
    h2_ref[...] = _rms(x1, g2_ref[...]).astype(BF16)


def _merge(oa, ob, zg, x, wpd, wpg, wo, g1, g2, *, tm):
    m, d = x.shape
    na, nb_ = oa.shape[1], ob.shape[1]
    row = lambda n, col=0: pl.BlockSpec((tm, n), lambda i: (i, col))
    return pl.pallas_call(
        _merge_kernel,
        grid=(m // tm,),
        in_specs=[row(na), row(nb_), row(d, 0), row(d, 1), row(d),
                  _const_spec(wpd.shape), _const_spec(wpg.shape), _const_spec(wo.shape),
                  _const_spec(g1.shape), _const_spec(g2.shape)],
        out_specs=(row(d), row(d)),
        out_shape=(jax.ShapeDtypeStruct((m, d), F32), jax.ShapeDtypeStruct((m, d), BF16)),
        compiler_params=_cparams("parallel"),
        name="merge",
    )(oa, ob, zg, zg, x, wpd, wpg, wo, g1, g2)


def _geglu(cg, cv):
    return (jax.nn.gelu(cg, approximate=True) * cv).astype(BF16)


def _ffn_finish(acc, x1_ref, g_ref, y_ref):
    y_ref[...] = x1_ref[...] + _rms(acc, g_ref[...])


def _ffn_prompt_kernel(hp_ref, h_ref, x1_ref, wug_ref, wuv_ref, wcg_ref, wcv_ref, bcg_ref, bcv_ref,
                       wd_ref, g_ref, y_ref, csg_ref, csv_ref, *, tm, rb, halo, cw, seq_tiles):
    i = pl.program_id(0)
    j = pl.program_id(1)

    @pl.when(j == 0)
    def _():
        y_ref[...] = jnp.zeros(y_ref.shape, F32)

    hp = jnp.where(i % seq_tiles == 0, jnp.zeros_like(hp_ref[...]), hp_ref[...])
    n_rb = tm // rb
    lhs = [jnp.concatenate([hp, h_ref[0:rb]], axis=0)] + [h_ref[r * rb:(r + 1) * rb] for r in range(1, n_rb)]
    chunks = [slice(c0, c0 + cw) for c0 in range(0, wug_ref.shape[1], cw)]

    ups = [[(_dot(x, wug_ref[:, cols]), _dot(x, wuv_ref[:, cols])) for cols in chunks] for x in lhs]

    def conv(u, cols, wc_ref, bc_ref):
        return (bc_ref[:, cols] + wc_ref[0:1, cols] * u[halo - 2:halo - 2 + rb]
                + wc_ref[1:2, cols] * u[halo - 1:halo - 1 + rb] + wc_ref[2:3, cols] * u[halo:halo + rb])

    for r in range(n_rb):
        acts = []
        for c, cols in enumerate(chunks):
            ug, uv = ups[r][c]
            if r > 0:
                top = rb if r > 1 else rb + halo
                ug = jnp.concatenate([ups[r - 1][c][0][top - halo:top], ug], axis=0)
                uv = jnp.concatenate([ups[r - 1][c][1][top - halo:top], uv], axis=0)
            if r == n_rb - 1:
                csg_ref[:, cols] = ug[halo + rb - 8:halo + rb]
                csv_ref[:, cols] = uv[halo + rb - 8:halo + rb]
            acts.append(_geglu(conv(ug, cols, wcg_ref, bcg_ref), conv(uv, cols, wcv_ref, bcv_ref)))
        a = jnp.concatenate(acts, axis=1) if len(acts) > 1 else acts[0]
        y_ref[r * rb:(r + 1) * rb] += _dot(a, wd_ref[...])

    @pl.when(j == pl.num_programs(1) - 1)
    def _():
        _ffn_finish(y_ref[...], x1_ref, g_ref, y_ref)


def _ffn_prompt(h2, x1, w_up, w_conv, b_conv, w_down, g, *, seq, tm, tf):
    m, d = x1.shape
    d_ff = w_down.shape[0]
    nf = d_ff // tf
    halo = SUBLANES_BF16
    hb = tm // halo
    col = lambda rows, off: pl.BlockSpec((rows, tf), lambda i, j: (0, j + off))
    tok = lambda n: pl.BlockSpec((tm, n), lambda i, j: (i, 0))
    cs = pl.BlockSpec((8, tf), lambda i, j: (i, j))
    cs_shape = jax.ShapeDtypeStruct((m // tm * 8, d_ff), F32)
    cw = min(tf, 2 * LANES)
    rb = min(tm, 2 * LANES)
    x1_spec = pl.BlockSpec((tm, d), lambda i, j: (i, 0), pipeline_mode=pl.Buffered(1))
    return pl.pallas_call(
        functools.partial(_ffn_prompt_kernel, tm=tm, rb=rb, halo=halo, cw=cw, seq_tiles=seq // tm),
        grid=(m // tm, nf),
        in_specs=[pl.BlockSpec((halo, d), lambda i, j: (jnp.maximum(i * hb - 1, 0), 0)),
                  tok(d), x1_spec, col(d, 0), col(d, nf), col(CONV_W, 0), col(CONV_W, nf),
                  col(1, 0), col(1, nf), pl.BlockSpec((tf, d), lambda i, j: (j, 0)),
                  pl.BlockSpec(g.shape, lambda i, j: (0, 0))],
        out_specs=(tok(d), cs, cs),
        out_shape=(jax.ShapeDtypeStruct((m, d), F32), cs_shape, cs_shape),
        compiler_params=_cparams("parallel", "arbitrary"),
        name="ffn_prompt",
    )(h2, h2, x1, w_up, w_up, w_conv, w_conv, b_conv, b_conv, w_down, g)


def _ffn_sample_kernel(h_ref, x1_ref, ctxg_ref, ctxv_ref, wug_ref, wuv_ref, wcg_ref, wcv_ref,
                       bcg_ref, bcv_ref, wd_ref, g_ref, y_ref, csg_ref, csv_ref, acc_scr, *, nb):
    j = pl.program_id(0)
    m = h_ref.shape[0]
    h = h_ref[...]

    def conv(w_up_ref, ctx_ref, wc_ref, bc_ref, cs_ref):
        u = _dot(h, w_up_ref[...])
        up = jnp.concatenate([ctx_ref[0], ctx_ref[1], u], axis=0)
        cs_ref[0] = u[m - 2 * nb:m - nb]
        cs_ref[1] = u[m - nb:m]
        return (bc_ref[...] + wc_ref[0:1] * up[0:m] + wc_ref[1:2] * up[nb:nb + m]
                + wc_ref[2:3] * up[2 * nb:2 * nb + m])

    a = _geglu(conv(wug_ref, ctxg_ref, wcg_ref, bcg_ref, csg_ref),
               conv(wuv_ref, ctxv_ref, wcv_ref, bcv_ref, csv_ref))
    part = _dot(a, wd_ref[...])

    @pl.when(j == 0)
    def _():
        acc_scr[...] = part

    @pl.when(j > 0)
    def _():
        acc_scr[...] += part

    @pl.when(j == pl.num_programs(0) - 1)
    def _():
        _ffn_finish(acc_scr[...], x1_ref, g_ref, y_ref)


def _ffn_sample(h2, x1, ctx, w_up, w_conv, b_conv, w_down, g, *, tf):
    m, d = x1.shape
    nb = ctx.shape[1]
    d_ff = w_down.shape[0]
    nf = d_ff // tf
    col = lambda rows, off: pl.BlockSpec((rows, tf), lambda j: (0, j + off))
    full = lambda a: pl.BlockSpec(a.shape, lambda j: (0, 0))
    ctx_spec = lambda off: pl.BlockSpec((CONV_W - 1, nb, tf), lambda j: (0, 0, j + off))
    cs = pl.BlockSpec((CONV_W - 1, nb, tf), lambda j: (0, 0, j))
    cs_shape = jax.ShapeDtypeStruct((CONV_W - 1, nb, d_ff), F32)
    return pl.pallas_call(
        functools.partial(_ffn_sample_kernel, nb=nb),
        grid=(nf,),
        in_specs=[full(h2), full(x1), ctx_spec(0), ctx_spec(nf), col(d, 0), col(d, nf),
                  col(CONV_W, 0), col(CONV_W, nf), col(1, 0), col(1, nf),
                  pl.BlockSpec((tf, d), lambda j: (j, 0)), full(g)],
        out_specs=(full(x1), cs, cs),
        out_shape=(jax.ShapeDtypeStruct((m, d), F32), cs_shape, cs_shape),
        scratch_shapes=[pltpu.VMEM((m, d), F32)],
        compiler_params=_cparams("arbitrary"),
        name="ffn_sample",
    )(h2, x1, ctx, ctx, w_up, w_up, w_conv, w_conv, b_conv, b_conv, w_down, g)


def _pick(n, candidates):
    for c in candidates:
        if n % c == 0:
            return c
    raise ValueError(f"no tile in {candidates} divides {n}")


def _layer_weights(l, p):
    d, n_in = p['w_in'].shape[1:]
    n_attn = p['cache_heads'] * 2 * DH_A
    w = p['w_in'][l]
    o_gla = 3 * n_attn
    o_lr = o_gla + 2 * H_B * DK_B + 2 * H_B * DV_B
    o_gate = o_lr + GATE_RANK
    assert n_in == o_gate + 2 * d
    w_in_b, w_gate = _cast_w_in(w, o_gate)
    wg = jnp.concatenate([p['w_gla_gate'][l].astype(F32),
                          jnp.zeros((LANES - GATE_RANK, H_B * DK_B), F32)], axis=0)
    wg_hi = wg.astype(BF16)
    wg_lo = (wg - wg_hi.astype(F32)).astype(BF16)
    row = lambda a: a[l].reshape(1, -1).astype(F32)
    return dict(
        w_in=w_in_b, w_gate=w_gate, n_attn=n_attn, o_vr=o_gla + 2 * H_B * DK_B, o_lr=o_lr,
        wg_hi=wg_hi, wg_lo=wg_lo, bg=row(p['b_gla_gate']),
        g_pre_mix=row(p['g_pre_mix']), g_head_diff=row(p['g_head_diff']), g_head_gla=row(p['g_head_gla']),
        wpd=p['w_proj_diff'][l].astype(BF16), wpg=p['w_proj_gla'][l].astype(BF16),
        wo=p['w_out'][l].astype(BF16), g_post_mix=row(p['g_post_mix']), g_pre_ffn=row(p['g_pre_ffn']),
        w_up=p['w_up'][l].astype(BF16), w_conv=p['w_conv'][l].astype(F32), b_conv=row(p['b_conv']),
        w_down=p['w_down'][l].astype(BF16), g_post_ffn=row(p['g_post_ffn']))


def _token_stages(x, lw, *, tm_proj, tm_mix, transposed_v):
    h, *proj = _attn_proj(x, lw['g_pre_mix'], lw['w_in'], lw['o_lr'], lw['wg_hi'], lw['wg_lo'], lw['bg'],
                          n_attn=lw['n_attn'], tm=tm_proj, transposed_v=transposed_v)
    nv = H_B * DV_B
    zb = _act_proj(h, lw['w_in'], lw['o_vr'], 2 * nv, ('id',) * 4 + ('silu',) * 4,
                   tm=tm_mix, tn=2 * nv, name="gla_proj")
    zg = _act_proj(h, lw['w_gate'], 0, lw['w_gate'].shape[1], ('sigmoid',) * 4,
                   tm=tm_mix, tn=nv, name="gate_proj")
    return proj, zb, zg


def kernel(x_prompt, x_sample, cache_k, cache_v, page_table, state_gla, state_conv, rel_bias,
           g_pre_mix, w_in, lambda_q1, lambda_k1, lambda_q2, lambda_k2, g_head_diff,
           w_gla_gate, b_gla_gate, g_head_gla, w_proj_diff, w_proj_gla, w_out, g_post_mix,
           g_pre_ffn, w_up, w_conv, b_conv, w_down, g_post_ffn):
    batch, seq, d = x_prompt.shape
    nb, t_dec, _ = x_sample.shape
    depth, n_pool, page, n_heads, kw = cache_k.shape
    assert page == PAGE_SIZE and kw == 2 * DH_A and cache_v.shape[-1] == DV_A
    d_ff = w_down.shape[1]
    width = n_heads * DV_A
    params = dict(w_in=w_in, w_gla_gate=w_gla_gate, b_gla_gate=b_gla_gate, g_pre_mix=g_pre_mix,
                  g_head_diff=g_head_diff, g_head_gla=g_head_gla, w_proj_diff=w_proj_diff,
                  w_proj_gla=w_proj_gla, w_out=w_out, g_post_mix=g_post_mix, g_pre_ffn=g_pre_ffn,
                  w_up=w_up, w_conv=w_conv, b_conv=b_conv, w_down=w_down, g_post_ffn=g_post_ffn,
                  cache_heads=n_heads)

    mp = batch * seq
    ms = nb * t_dec
    tq = _pick(seq, (512, 256, 128))
    tm_proj = _pick(mp, (256, 128))
    tm_mix = _pick(mp, (1024, 512, 256, 128))
    tm_ffn = _pick(seq, (1024, 512, 256, 128))
    tf = _pick(d_ff, (512, 256, 128))
    chunk = _pick(seq, (GLA_CHUNK,))
    pps = _pick(page_table.shape[1], (16, 8, 4, 2, 1))
    t_pad = SUBLANES_BF16

    page_table = page_table.astype(jnp.int32)
    eye_2 = jnp.eye(2, dtype=BF16)

    yp = x_prompt.reshape(mp, d)
    ys = x_sample.reshape(ms, d)
    outs = [[] for _ in range(8)]
    for l in range(depth):
        lam0 = _lambda_init(l)
        lw = _layer_weights(l, params)
        toe, sb_far, sb_last, sb_new, lam = _bias_tables(rel_bias, lambda_q1[l], lambda_k1[l], lambda_q2[l],
                                                 lambda_k2[l], n_heads=n_heads, t_dec=t_dec, lam_init=lam0)

        (qkv, k32, v32, zf, log_a, vt), zb, zg = _token_stages(yp, lw, tm_proj=tm_proj, tm_mix=tm_mix,
                                                               transposed_v=True)
        oa = _attn_prompt(qkv, vt, toe, lam, lw['g_head_diff'], batch=batch, seq=seq, n_heads=n_heads,
                          tq=tq, lam_init=lam0)
        ob, s_p = _gla_prompt(zf, log_a, zb, lw['g_head_gla'], batch=batch, seq=seq, chunk=chunk,
                              sub=GLA_SUB)
        x1, h2 = _merge(oa, ob.reshape(mp, -1), zg, yp, lw['wpd'], lw['wpg'], lw['wo'],
                        lw['g_post_mix'], lw['g_pre_ffn'], tm=tm_proj)
        yp, csg, csv = _ffn_prompt(h2, x1, lw['w_up'], lw['w_conv'], lw['b_conv'], lw['w_down'],
                                   lw['g_post_ffn'], seq=seq, tm=tm_ffn, tf=tf)
        cs = jnp.concatenate([csg, csv], axis=1).reshape(batch, seq // tm_ffn, 8, 2 * d_ff)
        outs[0].append(k32.reshape(batch, seq, n_heads, 2 * DH_A))
        outs[1].append(v32.reshape(batch, seq, n_heads, DV_A))
        outs[2].append(s_p)
        outs[3].append(cs[:, -1, 8 - (CONV_W - 1):])

        (qkv, k32, v32, zf, log_a), zb, zg = _token_stages(ys, lw, tm_proj=ms, tm_mix=ms,
                                                           transposed_v=False)
        q5 = qkv[:, :width].reshape(nb, t_dec, n_heads, 2, DH_A)
        q_rows = jnp.einsum('bthnd,mn->bmthnd', q5, eye_2).reshape(nb, 2 * t_dec * n_heads, 2 * DH_A)
        per_head = lambda a: a.reshape(nb, t_dec * n_heads, -1)
        pad_t = lambda a: jnp.pad(a.reshape(nb, t_dec, -1), ((0, 0), (0, t_pad - t_dec), (0, 0)))
        oa = _attn_sample(q_rows, per_head(qkv[:, width:2 * width]), per_head(qkv[:, 2 * width:]),
                          cache_k, cache_v, page_table, l, sb_far, sb_last, sb_new, lam,
                          lw['g_head_diff'], pps=pps, lam_init=lam0)
        ob, s_s = _gla_sample(pad_t(zf), pad_t(log_a), pad_t(zb), lw['g_head_gla'], state_gla[l])
        x1, h2 = _merge(oa.reshape(ms, -1), ob[:, :t_dec].reshape(ms, -1), zg, ys, lw['wpd'], lw['wpg'],
                        lw['wo'], lw['g_post_mix'], lw['g_pre_ffn'], tm=ms)
        tmajor = lambda a: a.reshape(nb, t_dec, -1).transpose(1, 0, 2).reshape(ms, -1)
        y_t, csg, csv = _ffn_sample(tmajor(h2), tmajor(x1), state_conv[l].transpose(1, 0, 2),
                                    lw['w_up'], lw['w_conv'], lw['b_conv'], lw['w_down'],
                                    lw['g_post_ffn'], tf=tf)
        ys = y_t.reshape(t_dec, nb, d).transpose(1, 0, 2).reshape(ms, d)
        outs[4].append(k32.reshape(nb, t_dec, n_heads, 2 * DH_A))
        outs[5].append(v32.reshape(nb, t_dec, n_heads, DV_A))
        outs[6].append(s_s)
        outs[7].append(jnp.concatenate([csg, csv], axis=2).transpose(1, 0, 2))

    return (yp.reshape(batch, seq, d), ys.reshape(nb, t_dec, d)) + tuple(jnp.stack(o) for o in outs)
```

```python
import functools
import math

import numpy as np
import jax
import jax.numpy as jnp
from jax import lax
from jax.experimental import pallas as pl
from jax.experimental.pallas import tpu as pltpu

F32 = jnp.float32
BF16 = jnp.bfloat16

DH_A = 64
DV_A = 128
DK_B = 128
DV_B = 256
H_B = 4
GATE_RANK = 16
GATE_TAU = 16.0
NUM_BUCKETS = 32
MAX_DISTANCE = 128
CONV_W = 3
PAGE_SIZE = 128
EPS = 1e-6
NEG_INF = -1e30
LOG2_E = math.log2(math.e)

LANES = 128
SUBLANES_BF16 = 16
VT_ROWS = DV_A + SUBLANES_BF16
VMEM_LIMIT_BYTES = 56 * 1024 * 1024

GLA_CHUNK = 64
GLA_SUB = 16


def _lambda_init(layer):
    return 0.8 - 0.6 * math.exp(-0.3 * layer)


def _cparams(*sem):
    return pltpu.CompilerParams(dimension_semantics=sem, vmem_limit_bytes=VMEM_LIMIT_BYTES)


def _const_spec(shape):
    nd = len(shape)
    return pl.BlockSpec(shape, lambda *_: (0,) * nd, pipeline_mode=pl.Buffered(1))


def _rms(xf, g):
    return xf * lax.rsqrt(jnp.mean(xf * xf, axis=-1, keepdims=True) + EPS) * g


def _dot(a, b):
    return jnp.dot(a, b, preferred_element_type=F32)


def _dot_nt(a, b):
    return lax.dot_general(a, b, (((1,), (1,)), ((), ())), preferred_element_type=F32)


def _dot_tn(a, b):
    return lax.dot_general(a, b, (((0,), (0,)), ((), ())), preferred_element_type=F32)


def _split_bf16(x):
    hi = x.astype(BF16)
    lo = (x - hi.astype(F32)).astype(BF16)
    return hi, lo


def _t5_bucket_np(rel):
    n = np.maximum(rel, 0)
    max_exact = NUM_BUCKETS // 2
    nf = np.maximum(n, 1).astype(np.float32)
    large = max_exact + (np.log(nf / max_exact) / math.log(MAX_DISTANCE / max_exact)
                         * (NUM_BUCKETS - max_exact)).astype(np.int32)
    large = np.minimum(large, NUM_BUCKETS - 1)
    return np.where(n < max_exact, n, large).astype(np.int32)


def _bias_tables_kernel(relb_ref, bk_toe_ref, bk_far_ref, bk_last_ref, bk_new_ref,
                        lq1_ref, lk1_ref, lq2_ref, lk2_ref,
                        toe_ref, sb_far_ref, sb_last_ref, sb_new_ref, lam_ref, *, n_heads, lam_init):
    far = NUM_BUCKETS - 1

    def lookup(bk, h):
        out = jnp.full(bk.shape, NEG_INF, F32)
        for b in range(NUM_BUCKETS):
            out = jnp.where(bk == b, relb_ref[b, h] - relb_ref[far, h], out)
        return out

    for h in range(n_heads):
        toe_ref[0, h] = lookup(bk_toe_ref[0], h)
        toe_ref[1, h] = lookup(bk_toe_ref[1], h)

    def per_row_head(bk_ref, out_ref):
        row_h = lax.broadcasted_iota(jnp.int32, bk_ref.shape, 0) % n_heads
        out = jnp.zeros(bk_ref.shape, F32)
        for h in range(n_heads):
            out = jnp.where(row_h == h, lookup(bk_ref[...], h), out)
        out_ref[...] = -out

    per_row_head(bk_far_ref, sb_far_ref)
    per_row_head(bk_last_ref, sb_last_ref)
    per_row_head(bk_new_ref, sb_new_ref)
    d1 = jnp.sum(lq1_ref[...] * lk1_ref[...], axis=-1, keepdims=True)
    d2 = jnp.sum(lq2_ref[...] * lk2_ref[...], axis=-1, keepdims=True)
    lam = jnp.exp(d1) - jnp.exp(d2) + lam_init
    lam_ref[...] = jnp.broadcast_to(lam, lam_ref.shape)


def _bias_tables(rel_bias, lq1, lk1, lq2, lk2, *, n_heads, t_dec, lam_init):
    i = np.arange(LANES)[:, None]
    j = np.arange(LANES)[None, :]
    bk_toe = np.stack([np.where(j >= i, _t5_bucket_np(j - i), -1),
                       _t5_bucket_np(LANES + j - i)]).astype(np.int32)
    rows = 2 * t_dec * n_heads
    r = np.arange(rows)[:, None]
    t, h = (r // n_heads) % t_dec, r % n_heads
    c = np.arange(PAGE_SIZE * n_heads)[None, :]
    key, hk = c // n_heads, c % n_heads
    bk_far = np.where(hk == h, NUM_BUCKETS - 1, -1).astype(np.int32)
    bk_last = np.where(hk == h, _t5_bucket_np(PAGE_SIZE + t - key), -1).astype(np.int32)
    cn = c[:, :LANES]
    key, hk = cn // n_heads, cn % n_heads
    bk_new = np.where((hk == h) & (key <= t) & (key < t_dec), _t5_bucket_np(t - key), -1).astype(np.int32)
    vec = lambda a: a.reshape(1, -1).astype(F32)
    smem = pl.BlockSpec(memory_space=pltpu.SMEM)
    tab = lambda a: jax.ShapeDtypeStruct(a.shape, F32)
    return pl.pallas_call(
        functools.partial(_bias_tables_kernel, n_heads=n_heads, lam_init=lam_init),
        out_shape=(jax.ShapeDtypeStruct((2, n_heads, LANES, LANES), F32),
                   tab(bk_far), tab(bk_last), tab(bk_new),
                   jax.ShapeDtypeStruct((8, LANES), F32)),
        in_specs=[smem] + [pl.BlockSpec(memory_space=pltpu.VMEM)] * 8,
        name="bias_tables",
    )(rel_bias.astype(F32), jnp.asarray(bk_toe), jnp.asarray(bk_far), jnp.asarray(bk_last),
      jnp.asarray(bk_new), vec(lq1), vec(lk1), vec(lq2), vec(lk2))


def _attn_proj_kernel(x_ref, g_ref, w_ref, wlr_ref, wgh_ref, wgl_ref, bg_ref,
                      h_ref, qkv_ref, k32_ref, v32_ref, zf_ref, loga_ref, *maybe_vt_ref, n_attn, n_qk):
    h = _rms(x_ref[...], g_ref[...]).astype(BF16)
    h_ref[...] = h
    q = _dot_nt(h, w_ref[0:n_attn]) * DH_A ** -0.5
    qkv_ref[:, 0:n_attn] = q.astype(BF16)
    k = _dot_nt(h, w_ref[n_attn:2 * n_attn])
    k32_ref[...] = k
    qkv_ref[:, n_attn:2 * n_attn] = k.astype(BF16)
    v = _dot_nt(h, w_ref[2 * n_attn:3 * n_attn])
    v32_ref[...] = v
    qkv_ref[:, 2 * n_attn:3 * n_attn] = v.astype(BF16)
    for vt_ref in maybe_vt_ref:
        vt = v.T.astype(BF16)
        ones = jnp.ones((VT_ROWS - DV_A, vt.shape[1]), BF16)
        vt_ref[...] = jnp.concatenate(
            [part for h in range(n_attn // DV_A) for part in (vt[h * DV_A:(h + 1) * DV_A], ones)], axis=0)
    zf_ref[:, 0:n_qk] = _dot_nt(h, w_ref[3 * n_attn:3 * n_attn + n_qk]) * DK_B ** -0.5
    zf_ref[:, n_qk:2 * n_qk] = _dot_nt(h, w_ref[3 * n_attn + n_qk:3 * n_attn + 2 * n_qk])
    lr = _dot_nt(h, wlr_ref[...])
    lr_hi, lr_lo = _split_bf16(lr)
    pre = (_dot(lr_hi, wgh_ref[...]) + _dot(lr_hi, wgl_ref[...]) + _dot(lr_lo, wgh_ref[...])
           + bg_ref[...])
    log_sig = jnp.minimum(pre, 0.0) - jnp.log1p(jnp.exp(-jnp.abs(pre)))
    loga_ref[...] = log_sig / GATE_TAU


def _attn_proj(x, g, w_t, o_lr, wgh, wgl, bg, *, n_attn, tm, transposed_v):
    m, d = x.shape
    n_qk = H_B * DK_B
    n_lead = 3 * n_attn + 2 * n_qk
    assert o_lr % LANES == 0
    row = lambda n: pl.BlockSpec((tm, n), lambda i: (i, 0))
    out_specs = [row(d), row(3 * n_attn), row(n_attn), row(n_attn), row(2 * n_qk), row(n_qk)]
    out_shape = [jax.ShapeDtypeStruct((m, d), BF16),
                 jax.ShapeDtypeStruct((m, 3 * n_attn), BF16),
                 jax.ShapeDtypeStruct((m, n_attn), F32),
                 jax.ShapeDtypeStruct((m, n_attn), F32),
                 jax.ShapeDtypeStruct((m, 2 * n_qk), F32),
                 jax.ShapeDtypeStruct((m, n_qk), F32)]
    if transposed_v:
        vt_rows = n_attn // DV_A * VT_ROWS
        out_specs.append(pl.BlockSpec((vt_rows, tm), lambda i: (0, i)))
        out_shape.append(jax.ShapeDtypeStruct((vt_rows, m), BF16))
    once = pl.Buffered(1)
    return pl.pallas_call(
        functools.partial(_attn_proj_kernel, n_attn=n_attn, n_qk=n_qk),
        grid=(m // tm,),
        in_specs=[row(d), _const_spec(g.shape),
                  pl.BlockSpec((n_lead, d), lambda i: (0, 0), pipeline_mode=once),
                  pl.BlockSpec((LANES, d), lambda i: (o_lr // LANES, 0), pipeline_mode=once),
                  _const_spec(wgh.shape), _const_spec(wgl.shape), _const_spec(bg.shape)],
        out_specs=tuple(out_specs),
        out_shape=tuple(out_shape),
        compiler_params=_cparams("parallel"),
        name="attn_proj",
    )(x, g, w_t, w_t, wgh, wgl, bg)


def _sigmoid(x):
    return 0.5 * jnp.tanh(0.5 * x) + 0.5


_ACTS = {
    'id': lambda x: x,
    'silu': lambda x: x * _sigmoid(x),
    'sigmoid': _sigmoid,
}


def _act_proj_kernel(h_ref, w_ref, o_ref, *, acts, cw):
    h = h_ref[...]
    accs = [_dot_nt(h, w_ref[c * cw:(c + 1) * cw]) for c in range(len(acts))]
    for c, (act, acc) in enumerate(zip(acts, accs)):
        o_ref[:, c * cw:(c + 1) * cw] = _ACTS[act](acc).astype(BF16)


def _act_proj(h, w_t, row0, n, acts, *, tm, tn, name):
    m, d = h.shape
    cw = tn // len(acts)
    assert row0 % tn == 0 and n % tn == 0 and cw % LANES == 0
    j0 = row0 // tn
    nj = n // tn
    w_mode = dict(pipeline_mode=pl.Buffered(1)) if nj == 1 else {}
    return pl.pallas_call(
        functools.partial(_act_proj_kernel, acts=acts, cw=cw),
        grid=(m // tm, nj),
        in_specs=[pl.BlockSpec((tm, d), lambda i, j: (i, 0)),
                  pl.BlockSpec((tn, d), lambda i, j: (j0 + j, 0), **w_mode)],
        out_specs=pl.BlockSpec((tm, tn), lambda i, j: (i, j)),
        out_shape=jax.ShapeDtypeStruct((m, n), BF16),
        compiler_params=_cparams("parallel", "arbitrary"),
        name=name,
    )(h, w_t)


def _softmax_step(s, v, m_scr, l_scr, acc_scr):
    m_prev = m_scr[...]
    m_new = jnp.maximum(m_prev, jnp.max(s, axis=-1, keepdims=True))
    alpha = jnp.exp(m_prev - m_new)
    p = jnp.exp(s - m_new)
    l_scr[...] = alpha * l_scr[...] + jnp.sum(p, axis=-1, keepdims=True)
    acc_scr[...] = alpha * acc_scr[...] + _dot(p.astype(BF16), v)
    m_scr[...] = m_new


def _head_norm(o, g, lam_init):
    return o * lax.rsqrt(jnp.mean(o * o, axis=-1, keepdims=True) + EPS) * g * (1.0 - lam_init)


def _attn_prompt_kernel(lam_ref, q_ref, k_ref, vt_ref, toe_ref, g_ref, o_ref,
                        bias_scr, *scr, tq, cb, ahead, lam_init):
    qi = pl.program_id(2)
    nb = tq // LANES

    @pl.when(qi == 0)
    def _():
        for r in range(nb):
            for c in range(nb):
                blk = (slice(r * LANES, (r + 1) * LANES), slice(c * LANES, (c + 1) * LANES))
                for t, d in ((0, tq + (c - r) * LANES), (1, (c - r) * LANES)):
                    if d < 0:
                        bias_scr[(t,) + blk] = jnp.full((LANES, LANES), -NEG_INF, F32)
                    elif d == 0:
                        bias_scr[(t,) + blk] = toe_ref[0] * -LOG2_E
                    elif d == LANES:
                        bias_scr[(t,) + blk] = toe_ref[1] * -LOG2_E
                    else:
                        bias_scr[(t,) + blk] = jnp.zeros((LANES, LANES), F32)

    n_cb = 2 * tq // cb
    chains = [scr[3 * c:3 * c + 3] for c in range(n_cb)]
    q = (q_ref[...].astype(F32) * LOG2_E).astype(BF16)
    lane = lax.broadcasted_iota(jnp.int32, (cb, q.shape[1]), 1)
    for c, (q2_scr, m_scr, acc_scr) in enumerate(chains):
        q_blk = q[(c * cb) % tq:(c * cb) % tq + cb]
        keep = (lane < DH_A) if c * cb < tq else (lane >= DH_A)
        q2_scr[...] = jnp.where(keep, q_blk, jnp.zeros_like(q_blk))
        m_scr[...] = jnp.full(m_scr.shape, NEG_INF, F32)
        acc_scr[...] = jnp.zeros(acc_scr.shape, F32)

    def kv_tiles(js, bias_idx):
        offs = [pl.multiple_of(j * tq, tq) for j in js]
        ks = [k_ref[pl.ds(off, tq), :] for off in offs]
        vts = [vt_ref[:, pl.ds(off, tq)] for off in offs]
        bias = None if bias_idx is None else [
            [bias_scr[bi, :, b0:b0 + cb] for b0 in range(0, tq, cb)] for bi in bias_idx]
        blocks = [(t, c) for t in range(len(js)) for c in range(n_cb)]
        score = lambda t, c: _dot_nt(ks[t], chains[c][0][...])
        pending = [score(t, c) for t, c in blocks[:ahead]]
        for n, (t, c) in enumerate(blocks):
            _, m_scr, acc_scr = chains[c]
            s = pending.pop(0)
            if n + ahead < len(blocks):
                pending.append(score(*blocks[n + ahead]))
            if bias is not None:
                s = s - bias[t][c % len(bias[t])]
            m_prev = m_scr[...]
            m_new = jnp.maximum(m_prev, jnp.max(s, axis=0, keepdims=True))
            alpha = jnp.exp2(m_prev - m_new)
            p = jnp.exp2((s - m_new).astype(BF16))
            acc_scr[...] = alpha * acc_scr[...] + _dot(vts[t], p)
            m_scr[...] = m_new

    def far_pair(i, carry):
        kv_tiles([2 * i, 2 * i + 1], None)
        return carry

    def far_tile(j, carry):
        kv_tiles([j], None)
        return carry

    n_far = jnp.maximum(qi - 1, 0)
    lax.fori_loop(0, n_far // 2, far_pair, 0)
    lax.fori_loop(n_far // 2 * 2, n_far, far_tile, 0)

    @pl.when(qi == 0)
    def _():
        kv_tiles([0], [1])

    @pl.when(qi > 0)
    def _():
        kv_tiles([qi - 1, qi], [0, 1])

    on = jnp.concatenate([acc_scr[0:DV_A] / acc_scr[DV_A:DV_A + 1] for _, _, acc_scr in chains], axis=1)
    lam = lam_ref[0:1, 0:1]
    o_t = on[:, 0:tq] - lam * on[:, tq:2 * tq]
    o_t = o_t * lax.rsqrt(jnp.mean(o_t * o_t, axis=0, keepdims=True) + EPS)
    o_ref[...] = (o_t.T * (g_ref[...] * (1.0 - lam_init))).astype(BF16)


def _attn_prompt(qkv, vt, toe, lam, g_head, *, batch, seq, n_heads, tq, lam_init):
    nq = seq // tq
    koff = n_heads
    cb = min(tq, 2 * LANES)
    return pl.pallas_call(
        functools.partial(_attn_prompt_kernel, tq=tq, cb=cb, ahead=4, lam_init=lam_init),
        grid=(batch, n_heads, nq),
        in_specs=[pl.BlockSpec(lam.shape, lambda b, h, i: (0, 0)),
                  pl.BlockSpec((tq, LANES), lambda b, h, i: (b * nq + i, h)),
                  pl.BlockSpec((seq, LANES), lambda b, h, i: (b, koff + h)),
                  pl.BlockSpec((VT_ROWS, seq), lambda b, h, i: (h, b)),
                  pl.BlockSpec((2, None, LANES, LANES), lambda b, h, i: (0, h, 0, 0)),
                  pl.BlockSpec(g_head.shape, lambda b, h, i: (0, 0))],
        out_specs=pl.BlockSpec((tq, LANES), lambda b, h, i: (b * nq + i, h)),
        out_shape=jax.ShapeDtypeStruct((batch * seq, n_heads * LANES), BF16),
        scratch_shapes=[pltpu.VMEM((2, tq, tq), F32)]
        + [pltpu.VMEM((cb, LANES), BF16), pltpu.VMEM((1, cb), F32),
           pltpu.VMEM((VT_ROWS, cb), F32)] * (2 * tq // cb),
        compiler_params=_cparams("parallel", "parallel", "arbitrary"),
        name="attn_prompt",
    )(lam, qkv, qkv, vt, toe, g_head)


def _attn_sample_kernel(pt_ref, lam_ref, q_ref, kn_ref, vn_ref, sbf_ref, sbl_ref, sbn_ref, g_ref, *rest,
                        pps, bp, lam_init):
    k_refs = rest[:pps]
    v_refs = rest[pps:2 * pps]
    o_ref = rest[2 * pps]
    m_scr, l_scr, acc_scr = rest[2 * pps + 1:]
    g = pl.program_id(1)
    last = pl.num_programs(1) - 1

    @pl.when(g == 0)
    def _():
        m_scr[...] = jnp.full(m_scr.shape, NEG_INF, F32)
        l_scr[...] = jnp.zeros(l_scr.shape, F32)
        acc_scr[...] = jnp.zeros(acc_scr.shape, F32)

    q = q_ref[...]
    cols = sbf_ref.shape[1]
    flat = lambda r: r[...].reshape(cols, r.shape[-1]).astype(BF16)
    pages = lambda refs, b: jnp.concatenate([flat(r) for r in refs[b * bp:(b + 1) * bp]], axis=0)
    scores = [_dot_nt(q, pages(k_refs, b)) for b in range(pps // bp)]
    far = sbf_ref[...]
    tail = jnp.where(g == last, sbl_ref[...], far)
    for b, s in enumerate(scores):
        s = jnp.concatenate([s[:, i * cols:(i + 1) * cols] - (far if b * bp + i < pps - 1 else tail)
                             for i in range(bp)], axis=1)
        _softmax_step(s, pages(v_refs, b), m_scr, l_scr, acc_scr)

    @pl.when(g == last)
    def _():
        pad = jnp.zeros((sbn_ref.shape[1] - kn_ref.shape[0], kn_ref.shape[1]), BF16)
        kn = jnp.concatenate([kn_ref[...], pad], axis=0)
        vn = jnp.concatenate([vn_ref[...], pad], axis=0)
        _softmax_step(_dot_nt(q, kn) - sbn_ref[...], vn, m_scr, l_scr, acc_scr)
        on = acc_scr[...] / l_scr[...]
        half = on.shape[0] // 2
        o = on[0:half] - lam_ref[0:1, 0:1] * on[half:2 * half]
        o_ref[...] = _head_norm(o, g_ref[...], lam_init).astype(BF16)


def _attn_sample(q, kn, vn, cache_k, cache_v, page_table, layer, sb_far, sb_last, sb_new, lam, g_head,
                 *, pps, lam_init):
    nb, rows, feat = q.shape
    n_pages = page_table.shape[1]
    steps = n_pages // pps
    seq_spec = lambda a: pl.BlockSpec((None,) + a.shape[1:], lambda b, g, pt: (b, 0, 0))
    full = lambda a: pl.BlockSpec(a.shape, lambda b, g, pt: (0, 0))

    def page_spec(jj):
        return pl.BlockSpec((None, None) + cache_k.shape[2:],
                            lambda b, g, pt: (layer, pt[b, g * pps + jj], 0, 0, 0))

    grid_spec = pltpu.PrefetchScalarGridSpec(
        num_scalar_prefetch=1,
        grid=(nb, steps),
        in_specs=[full(lam), seq_spec(q), seq_spec(kn), seq_spec(vn), full(sb_far), full(sb_last),
                  full(sb_new), full(g_head)] + [page_spec(jj) for jj in range(pps)] * 2,
        out_specs=pl.BlockSpec((None, rows // 2, feat), lambda b, g, pt: (b, 0, 0)),
        scratch_shapes=[pltpu.VMEM((rows, 1), F32), pltpu.VMEM((rows, 1), F32),
                        pltpu.VMEM((rows, feat), F32)],
    )
    return pl.pallas_call(
        functools.partial(_attn_sample_kernel, pps=pps, bp=min(pps, 2), lam_init=lam_init),
        grid_spec=grid_spec,
        out_shape=jax.ShapeDtypeStruct((nb, rows // 2, feat), BF16),
        compiler_params=_cparams("parallel", "arbitrary"),
        name="attn_sample",
    )(page_table, lam, q, kn, vn, sb_far, sb_last, sb_new, g_head, *([cache_k] * pps), *([cache_v] * pps))


def _gla_chunk(heads, *, sub):
    c_len = heads[0][0].shape[0]
    ri = lax.broadcasted_iota(jnp.int32, (c_len, c_len), 0)
    ci = lax.broadcasted_iota(jnp.int32, (c_len, c_len), 1)
    row = lax.broadcasted_iota(jnp.int32, (c_len, 1), 0)
    tril = (ri >= ci).astype(BF16)
    bs = []
    for _, _, log_a, _, _ in heads:
        a_hi, a_lo = _split_bf16(log_a)
        bs.append(_dot(tril, a_hi) + _dot(tril, a_lo))
    outs, states = [], []
    for (q, k, _, v, st), b in zip(heads, bs):
        b_last = b[c_len - 1:c_len]
        outs.append(_dot_nt((q * jnp.exp(b)).astype(BF16), st.astype(BF16)))
        k_out = (k * jnp.exp(b_last - b)).astype(BF16)
        states.append(st * jnp.exp(b_last) + _dot_tn(v, k_out))
    atts = []
    for (q, k, _, _, _), b in zip(heads, bs):
        rows = []
        for i in range(c_len // sub):
            lo, hi = i * sub, (i + 1) * sub
            b_ref = b[lo:lo + 1]
            qh = (q[lo:hi] * jnp.exp(b[lo:hi] - b_ref)).astype(BF16)
            kh = (k * jnp.exp(jnp.where(row < hi, b_ref - b, NEG_INF))).astype(BF16)
            rows.append(_dot_nt(qh, kh))
        att = jnp.concatenate(rows, axis=0) if len(rows) > 1 else rows[0]
        atts.append(jnp.where(ri >= ci, att, 0.0).astype(BF16))
    outs = [o + _dot(att, v) for o, att, (_, _, _, v, _) in zip(outs, atts, heads)]
    return list(zip(outs, states))


def _gla_out(o, g, r):
    return (o * lax.rsqrt(jnp.mean(o * o, axis=-1, keepdims=True) + EPS) * g * r.astype(F32)).astype(BF16)


def _gla_prompt_kernel(qk_ref, la_ref, v_ref, r_ref, g_ref, o_ref, s_ref, st_scr, *, batch, sub):
    ci = pl.program_id(0)

    @pl.when(ci == 0)
    def _():
        st_scr[...] = jnp.zeros(st_scr.shape, F32)

    nk = H_B * DK_B
    ids = [(b, h) for b in range(batch) for h in range(H_B)]
    ks = lambda h: slice(h * DK_B, (h + 1) * DK_B)
    vs = lambda h: slice(h * DV_B, (h + 1) * DV_B)
    heads = [(qk_ref[b, :, ks(h)], qk_ref[b, :, nk + h * DK_B:nk + (h + 1) * DK_B], la_ref[b, :, ks(h)],
              v_ref[b, :, vs(h)], st_scr[b, h]) for b, h in ids]
    for (b, h), (o, st_new) in zip(ids, _gla_chunk(heads, sub=sub)):
        st_scr[b, h] = st_new
        o_ref[b, :, vs(h)] = _gla_out(o, g_ref[...], r_ref[b, :, vs(h)])

    @pl.when(ci == pl.num_programs(0) - 1)
    def _():
        for b in range(batch):
            for h in range(H_B):
                s_ref[b, h] = st_scr[b, h].T


def _gla_prompt(zf, log_a, zb, g_head, *, batch, seq, chunk, sub):
    nqk = zf.shape[1]
    nv = H_B * DV_B
    zf3 = zf.reshape(batch, seq, nqk)
    la3 = log_a.reshape(batch, seq, log_a.shape[1])
    zb3 = zb.reshape(batch, seq, zb.shape[1])
    blk = lambda n, col: pl.BlockSpec((batch, chunk, n), lambda c: (0, c, col))
    return pl.pallas_call(
        functools.partial(_gla_prompt_kernel, batch=batch, sub=sub),
        grid=(seq // chunk,),
        in_specs=[blk(nqk, 0), blk(la3.shape[2], 0), blk(nv, 0), blk(nv, 1),
                  pl.BlockSpec(g_head.shape, lambda c: (0, 0))],
        out_specs=(blk(nv, 0),
                   pl.BlockSpec((batch, H_B, DK_B, DV_B), lambda c: (0, 0, 0, 0))),
        out_shape=(jax.ShapeDtypeStruct((batch, seq, nv), BF16),
                   jax.ShapeDtypeStruct((batch, H_B, DK_B, DV_B), F32)),
        scratch_shapes=[pltpu.VMEM((batch, H_B, DV_B, DK_B), F32)],
        compiler_params=_cparams("arbitrary"),
        name="gla_prompt",
    )(zf3, la3, zb3, zb3, g_head)


def _gla_sample_kernel(qk_ref, la_ref, v_ref, r_ref, g_ref, s0_ref, o_ref, s_ref, *, sub):
    nk = H_B * DK_B
    ks = lambda h: slice(h * DK_B, (h + 1) * DK_B)
    vs = lambda h: slice(h * DV_B, (h + 1) * DV_B)
    heads = [(qk_ref[:, ks(h)], qk_ref[:, nk + h * DK_B:nk + (h + 1) * DK_B], la_ref[:, ks(h)],
              v_ref[:, vs(h)], s0_ref[h].T) for h in range(H_B)]
    for h, (o, st_new) in enumerate(_gla_chunk(heads, sub=sub)):
        s_ref[h] = st_new.T
        o_ref[:, vs(h)] = _gla_out(o, g_ref[...], r_ref[:, vs(h)])


def _gla_sample(zf3, la3, zb3, g_head, s0):
    nb, tp, nqk = zf3.shape
    nv = H_B * DV_B
    blk = lambda n, col: pl.BlockSpec((None, tp, n), lambda b: (b, 0, col))
    st = pl.BlockSpec((None, H_B, DK_B, DV_B), lambda b: (b, 0, 0, 0))
    return pl.pallas_call(
        functools.partial(_gla_sample_kernel, sub=tp),
        grid=(nb,),
        in_specs=[blk(nqk, 0), blk(la3.shape[2], 0), blk(nv, 0), blk(nv, 1),
                  pl.BlockSpec(g_head.shape, lambda b: (0, 0)), st],
        out_specs=(blk(nv, 0), st),
        out_shape=(jax.ShapeDtypeStruct((nb, tp, nv), BF16),
                   jax.ShapeDtypeStruct(s0.shape, F32)),
        compiler_params=_cparams("parallel"),
        name="gla_sample",
    )(zf3, la3, zb3, zb3, g_head, s0)


def _merge_kernel(oa_ref, ob_ref, sga_ref, sgb_ref, x_ref, wpd_ref, wpg_ref, wo_ref, g1_ref, g2_ref,
                  x1_ref, h2_ref):
    m = (sga_ref[...].astype(F32) * _dot(oa_ref[...], wpd_ref[...])
         + sgb_ref[...].astype(F32) * _dot(ob_ref[...], wpg_ref[...]))
    mix = _dot(m.astype(BF16), wo_ref[...])
    x1 = x_ref[...] + _rms(mix, g1_ref[...])
    x1_ref[...] = x1
    h2_ref[...] = _rms(x1, g2_ref[...]).astype(BF16)


def _merge(oa, ob, zg, x, wpd, wpg, wo, g1, g2, *, tm):
    m, d = x.shape
    na, nb_ = oa.shape[1], ob.shape[1]
    row = lambda n, col=0: pl.BlockSpec((tm, n), lambda i: (i, col))
    return pl.pallas_call(
        _merge_kernel,
        grid=(m // tm,),
        in_specs=[row(na), row(nb_), row(d, 0), row(d, 1), row(d),
                  _const_spec(wpd.shape), _const_spec(wpg.shape), _const_spec(wo.shape),
                  _const_spec(g1.shape), _const_spec(g2.shape)],
        out_specs=(row(d), row(d)),
        out_shape=(jax.ShapeDtypeStruct((m, d), F32), jax.ShapeDtypeStruct((m, d), BF16)),
        compiler_params=_cparams("parallel"),
        name="merge",
    )(oa, ob, zg, zg, x, wpd, wpg, wo, g1, g2)


def _geglu(cg, cv):
    return (jax.nn.gelu(cg, approximate=True) * cv).astype(BF16)


def _ffn_finish(acc, x1_ref, g_ref, y_ref):
    y_ref[...] = x1_ref[...] + _rms(acc, g_ref[...])


def _ffn_prompt_kernel(hp_ref, h_ref, x1_ref, wug_ref, wuv_ref, wcg_ref, wcv_ref, bcg_ref, bcv_ref,
                       wd_ref, g_ref, y_ref, csg_ref, csv_ref, *, tm, rb, halo, cw, seq_tiles):
    i = pl.program_id(0)
    j = pl.program_id(1)

    @pl.when(j == 0)
    def _():
        y_ref[...] = jnp.zeros(y_ref.shape, F32)

    hp = jnp.where(i % seq_tiles == 0, jnp.zeros_like(hp_ref[...]), hp_ref[...])
    n_rb = tm // rb
    lhs = [jnp.concatenate([hp, h_ref[0:rb]], axis=0)] + [h_ref[r * rb:(r + 1) * rb] for r in range(1, n_rb)]
    chunks = [slice(c0, c0 + cw) for c0 in range(0, wug_ref.shape[1], cw)]

    ups = [[(_dot(x, wug_ref[:, cols]), _dot(x, wuv_ref[:, cols])) for cols in chunks] for x in lhs]

    def conv(u, cols, wc_ref, bc_ref):
        return (bc_ref[:, cols] + wc_ref[0:1, cols] * u[halo - 2:halo - 2 + rb]
                + wc_ref[1:2, cols] * u[halo - 1:halo - 1 + rb] + wc_ref[2:3, cols] * u[halo:halo + rb])

    for r in range(n_rb):
        acts = []
        for c, cols in enumerate(chunks):
            ug, uv = ups[r][c]
            if r > 0:
                top = rb if r > 1 else rb + halo
                ug = jnp.concatenate([ups[r - 1][c][0][top - halo:top], ug], axis=0)
                uv = jnp.concatenate([ups[r - 1][c][1][top - halo:top], uv], axis=0)
            if r == n_rb - 1:
                csg_ref[:, cols] = ug[halo + rb - 8:halo + rb]
                csv_ref[:, cols] = uv[halo + rb - 8:halo + rb]
            acts.append(_geglu(conv(ug, cols, wcg_ref, bcg_ref), conv(uv, cols, wcv_ref, bcv_ref)))
        a = jnp.concatenate(acts, axis=1) if len(acts) > 1 else acts[0]
        y_ref[r * rb:(r + 1) * rb] += _dot(a, wd_ref[...])

    @pl.when(j == pl.num_programs(1) - 1)
    def _():
        _ffn_finish(y_ref[...], x1_ref, g_ref, y_ref)


def _ffn_prompt(h2, x1, w_up, w_conv, b_conv, w_down, g, *, seq, tm, tf):
    m, d = x1.shape
    d_ff = w_down.shape[0]
    nf = d_ff // tf
    halo = SUBLANES_BF16
    hb = tm // halo
    col = lambda rows, off: pl.BlockSpec((rows, tf), lambda i, j: (0, j + off))
    tok = lambda n: pl.BlockSpec((tm, n), lambda i, j: (i, 0))
    cs = pl.BlockSpec((8, tf), lambda i, j: (i, j))
    cs_shape = jax.ShapeDtypeStruct((m // tm * 8, d_ff), F32)
    cw = min(tf, 2 * LANES)
    rb = min(tm, 2 * LANES)
    x1_spec = pl.BlockSpec((tm, d), lambda i, j: (i, 0), pipeline_mode=pl.Buffered(1))
    return pl.pallas_call(
        functools.partial(_ffn_prompt_kernel, tm=tm, rb=rb, halo=halo, cw=cw, seq_tiles=seq // tm),
        grid=(m // tm, nf),
        in_specs=[pl.BlockSpec((halo, d), lambda i, j: (jnp.maximum(i * hb - 1, 0), 0)),
                  tok(d), x1_spec, col(d, 0), col(d, nf), col(CONV_W, 0), col(CONV_W, nf),
                  col(1, 0), col(1, nf), pl.BlockSpec((tf, d), lambda i, j: (j, 0)),
                  pl.BlockSpec(g.shape, lambda i, j: (0, 0))],
        out_specs=(tok(d), cs, cs),
        out_shape=(jax.ShapeDtypeStruct((m, d), F32), cs_shape, cs_shape),
        compiler_params=_cparams("parallel", "arbitrary"),
        name="ffn_prompt",
    )(h2, h2, x1, w_up, w_up, w_conv, w_conv, b_conv, b_conv, w_down, g)


def _ffn_sample_kernel(h_ref, x1_ref, ctxg_ref, ctxv_ref, wug_ref, wuv_ref, wcg_ref, wcv_ref,
                       bcg_ref, bcv_ref, wd_ref, g_ref, y_ref, csg_ref, csv_ref, acc_scr, *, nb):
    j = pl.program_id(0)
    m = h_ref.shape[0]
    h = h_ref[...]

    def conv(w_up_ref, ctx_ref, wc_ref, bc_ref, cs_ref):
        u = _dot(h, w_up_ref[...])
        up = jnp.concatenate([ctx_ref[0], ctx_ref[1], u], axis=0)
        cs_ref[0] = u[m - 2 * nb:m - nb]
        cs_ref[1] = u[m - nb:m]
        return (bc_ref[...] + wc_ref[0:1] * up[0:m] + wc_ref[1:2] * up[nb:nb + m]
                + wc_ref[2:3] * up[2 * nb:2 * nb + m])

    a = _geglu(conv(wug_ref, ctxg_ref, wcg_ref, bcg_ref, csg_ref),
               conv(wuv_ref, ctxv_ref, wcv_ref, bcv_ref, csv_ref))
    part = _dot(a, wd_ref[...])

    @pl.when(j == 0)
    def _():
        acc_scr[...] = part

    @pl.when(j > 0)
    def _():
        acc_scr[...] += part

    @pl.when(j == pl.num_programs(0) - 1)
    def _():
        _ffn_finish(acc_scr[...], x1_ref, g_ref, y_ref)


def _ffn_sample(h2, x1, ctx, w_up, w_conv, b_conv, w_down, g, *, tf):
    m, d = x1.shape
    nb = ctx.shape[1]
    d_ff = w_down.shape[0]
    nf = d_ff // tf
    col = lambda rows, off: pl.BlockSpec((rows, tf), lambda j: (0, j + off))
    full = lambda a: pl.BlockSpec(a.shape, lambda j: (0, 0))
    ctx_spec = lambda off: pl.BlockSpec((CONV_W - 1, nb, tf), lambda j: (0, 0, j + off))
    cs = pl.BlockSpec((CONV_W - 1, nb, tf), lambda j: (0, 0, j))
    cs_shape = jax.ShapeDtypeStruct((CONV_W - 1, nb, d_ff), F32)
    return pl.pallas_call(
        functools.partial(_ffn_sample_kernel, nb=nb),
        grid=(nf,),
        in_specs=[full(h2), full(x1), ctx_spec(0), ctx_spec(nf), col(d, 0), col(d, nf),
                  col(CONV_W, 0), col(CONV_W, nf), col(1, 0), col(1, nf),
                  pl.BlockSpec((tf, d), lambda j: (j, 0)), full(g)],
        out_specs=(full(x1), cs, cs),
        out_shape=(jax.ShapeDtypeStruct((m, d), F32), cs_shape, cs_shape),
        scratch_shapes=[pltpu.VMEM((m, d), F32)],
        compiler_params=_cparams("arbitrary"),
        name="ffn_sample",
    )(h2, x1, ctx, ctx, w_up, w_up, w_conv, w_conv, b_conv, b_conv, w_down, g)


def _pick(n, candidates):
    for c in candidates:
        if n % c == 0:
            return c
    raise ValueError(f"no tile in {candidates} divides {n}")


def _layer_weights(l, p):
    d, n_in = p['w_in'].shape[1:]
    n_attn = p['cache_heads'] * 2 * DH_A
    w = p['w_in'][l]
    o_gla = 3 * n_attn
    o_lr = o_gla + 2 * H_B * DK_B + 2 * H_B * DV_B
    o_gate = o_lr + GATE_RANK
    assert n_in == o_gate + 2 * d
    w_lead_t = w.T.astype(BF16)
    w_gate_t = w_lead_t[o_gate:]
    wg = jnp.concatenate([p['w_gla_gate'][l].astype(F32),
                          jnp.zeros((LANES - GATE_RANK, H_B * DK_B), F32)], axis=0)
    wg_hi = wg.astype(BF16)
    wg_lo = (wg - wg_hi.astype(F32)).astype(BF16)
    row = lambda a: a[l].reshape(1, -1).astype(F32)
    return dict(
        w_lead_t=w_lead_t, w_gate_t=w_gate_t, n_attn=n_attn, o_vr=o_gla + 2 * H_B * DK_B, o_lr=o_lr,
        wg_hi=wg_hi, wg_lo=wg_lo, bg=row(p['b_gla_gate']),
        g_pre_mix=row(p['g_pre_mix']), g_head_diff=row(p['g_head_diff']), g_head_gla=row(p['g_head_gla']),
        wpd=p['w_proj_diff'][l].astype(BF16), wpg=p['w_proj_gla'][l].astype(BF16),
        wo=p['w_out'][l].astype(BF16), g_post_mix=row(p['g_post_mix']), g_pre_ffn=row(p['g_pre_ffn']),
        w_up=p['w_up'][l].astype(BF16), w_conv=p['w_conv'][l].astype(F32), b_conv=row(p['b_conv']),
        w_down=p['w_down'][l].astype(BF16), g_post_ffn=row(p['g_post_ffn']))


def _token_stages(x, lw, *, tm_proj, tm_mix, transposed_v):
    h, *proj = _attn_proj(x, lw['g_pre_mix'], lw['w_lead_t'], lw['o_lr'], lw['wg_hi'], lw['wg_lo'],
                          lw['bg'], n_attn=lw['n_attn'], tm=tm_proj, transposed_v=transposed_v)
    nv = H_B * DV_B
    zb = _act_proj(h, lw['w_lead_t'], lw['o_vr'], 2 * nv, ('id',) * 4 + ('silu',) * 4,
                   tm=tm_mix, tn=2 * nv, name="gla_proj")
    zg = _act_proj(h, lw['w_gate_t'], 0, lw['w_gate_t'].shape[0], ('sigmoid',) * 4,
                   tm=tm_mix, tn=nv, name="gate_proj")
    return proj, zb, zg


def kernel(x_prompt, x_sample, cache_k, cache_v, page_table, state_gla, state_conv, rel_bias,
           g_pre_mix, w_in, lambda_q1, lambda_k1, lambda_q2, lambda_k2, g_head_diff,
           w_gla_gate, b_gla_gate, g_head_gla, w_proj_diff, w_proj_gla, w_out, g_post_mix,
           g_pre_ffn, w_up, w_conv, b_conv, w_down, g_post_ffn):
    batch, seq, d = x_prompt.shape
    nb, t_dec, _ = x_sample.shape
    depth, n_pool, page, n_heads, kw = cache_k.shape
    assert page == PAGE_SIZE and kw == 2 * DH_A and cache_v.shape[-1] == DV_A
    d_ff = w_down.shape[1]
    width = n_heads * DV_A
    params = dict(w_in=w_in, w_gla_gate=w_gla_gate, b_gla_gate=b_gla_gate, g_pre_mix=g_pre_mix,
                  g_head_diff=g_head_diff, g_head_gla=g_head_gla, w_proj_diff=w_proj_diff,
                  w_proj_gla=w_proj_gla, w_out=w_out, g_post_mix=g_post_mix, g_pre_ffn=g_pre_ffn,
                  w_up=w_up, w_conv=w_conv, b_conv=b_conv, w_down=w_down, g_post_ffn=g_post_ffn,
                  cache_heads=n_heads)

    mp = batch * seq
    ms = nb * t_dec
    tq = _pick(seq, (512, 256, 128))
    tm_proj = _pick(mp, (256, 128))
    tm_mix = _pick(mp, (1024, 512, 256, 128))
    tm_ffn = _pick(seq, (1024, 512, 256, 128))
    tf = _pick(d_ff, (512, 256, 128))
    chunk = _pick(seq, (GLA_CHUNK,))
    pps = _pick(page_table.shape[1], (16, 8, 4, 2, 1))
    t_pad = SUBLANES_BF16

    page_table = page_table.astype(jnp.int32)
    eye_2 = jnp.eye(2, dtype=BF16)

    yp = x_prompt.reshape(mp, d)
    ys = x_sample.reshape(ms, d)
    outs = [[] for _ in range(8)]
    for l in range(depth):
        lam0 = _lambda_init(l)
        lw = _layer_weights(l, params)
        toe, sb_far, sb_last, sb_new, lam = _bias_tables(rel_bias, lambda_q1[l], lambda_k1[l], lambda_q2[l],
                                                 lambda_k2[l], n_heads=n_heads, t_dec=t_dec, lam_init=lam0)

        (qkv, k32, v32, zf, log_a, vt), zb, zg = _token_stages(yp, lw, tm_proj=tm_proj, tm_mix=tm_mix,
                                                               transposed_v=True)
        oa = _attn_prompt(qkv, vt, toe, lam, lw['g_head_diff'], batch=batch, seq=seq, n_heads=n_heads,
                          tq=tq, lam_init=lam0)
        ob, s_p = _gla_prompt(zf, log_a, zb, lw['g_head_gla'], batch=batch, seq=seq, chunk=chunk,
                              sub=GLA_SUB)
        x1, h2 = _merge(oa, ob.reshape(mp, -1), zg, yp, lw['wpd'], lw['wpg'], lw['wo'],
                        lw['g_post_mix'], lw['g_pre_ffn'], tm=tm_proj)
        yp, csg, csv = _ffn_prompt(h2, x1, lw['w_up'], lw['w_conv'], lw['b_conv'], lw['w_down'],
                                   lw['g_post_ffn'], seq=seq, tm=tm_ffn, tf=tf)
        cs = jnp.concatenate([csg, csv], axis=1).reshape(batch, seq // tm_ffn, 8, 2 * d_ff)
        outs[0].append(k32.reshape(batch, seq, n_heads, 2 * DH_A))
        outs[1].append(v32.reshape(batch, seq, n_heads, DV_A))
        outs[2].append(s_p)
        outs[3].append(cs[:, -1, 8 - (CONV_W - 1):])

        (qkv, k32, v32, zf, log_a), zb, zg = _token_stages(ys, lw, tm_proj=ms, tm_mix=ms,
                                                           transposed_v=False)
        q5 = qkv[:, :width].reshape(nb, t_dec, n_heads, 2, DH_A)
        q_rows = jnp.einsum('bthnd,mn->bmthnd', q5, eye_2).reshape(nb, 2 * t_dec * n_heads, 2 * DH_A)
        per_head = lambda a: a.reshape(nb, t_dec * n_heads, -1)
        pad_t = lambda a: jnp.pad(a.reshape(nb, t_dec, -1), ((0, 0), (0, t_pad - t_dec), (0, 0)))
        oa = _attn_sample(q_rows, per_head(qkv[:, width:2 * width]), per_head(qkv[:, 2 * width:]),
                          cache_k, cache_v, page_table, l, sb_far, sb_last, sb_new, lam,
                          lw['g_head_diff'], pps=pps, lam_init=lam0)
        ob, s_s = _gla_sample(pad_t(zf), pad_t(log_a), pad_t(zb), lw['g_head_gla'], state_gla[l])
        x1, h2 = _merge(oa.reshape(ms, -1), ob[:, :t_dec].reshape(ms, -1), zg, ys, lw['wpd'], lw['wpg'],
                        lw['wo'], lw['g_post_mix'], lw['g_pre_ffn'], tm=ms)
        tmajor = lambda a: a.reshape(nb, t_dec, -1).transpose(1, 0, 2).reshape(ms, -1)
        y_t, csg, csv = _ffn_sample(tmajor(h2), tmajor(x1), state_conv[l].transpose(1, 0, 2),
                                    lw['w_up'], lw['w_conv'], lw['b_conv'], lw['w_down'],
                                    lw['g_post_ffn'], tf=tf)
        ys = y_t.reshape(t_dec, nb, d).transpose(1, 0, 2).reshape(ms, d)
        outs[4].append(k32.reshape(nb, t_dec, n_heads, 2 * DH_A))
        outs[5].append(v32.reshape(nb, t_dec, n_heads, DV_A))
        outs[6].append(s_s)
        outs[7].append(jnp.concatenate([csg, csv], axis=2).transpose(1, 0, 2))

    return (yp.reshape(batch, seq, d), ys.reshape(nb, t_dec, d)) + tuple(jnp.stack(o) for o in outs)
```

```python
import functools
import math

import numpy as np
import jax
import jax.numpy as jnp
from jax import lax
from jax.experimental import pallas as pl
from jax.experimental.pallas import tpu as pltpu

F32 = jnp.float32
BF16 = jnp.bfloat16

DH_A = 64
DV_A = 128
DK_B = 128
DV_B = 256
H_B = 4
GATE_RANK = 16
GATE_TAU = 16.0
NUM_BUCKETS = 32
MAX_DISTANCE = 128
CONV_W = 3
PAGE_SIZE = 128
EPS = 1e-6
NEG_INF = -1e30
LOG2_E = math.log2(math.e)

LANES = 128
SUBLANES_BF16 = 16
VT_ROWS = DV_A + SUBLANES_BF16
VMEM_LIMIT_BYTES = 56 * 1024 * 1024

GLA_CHUNK = 64
GLA_SUB = 16


def _lambda_init(layer):
    return 0.8 - 0.6 * math.exp(-0.3 * layer)


def _cparams(*sem):
    return pltpu.CompilerParams(dimension_semantics=sem, vmem_limit_bytes=VMEM_LIMIT_BYTES)


def _const_spec(shape):
    nd = len(shape)
    return pl.BlockSpec(shape, lambda *_: (0,) * nd, pipeline_mode=pl.Buffered(1))


def _rms(xf, g):
    return xf * lax.rsqrt(jnp.mean(xf * xf, axis=-1, keepdims=True) + EPS) * g


def _dot(a, b):
    return jnp.dot(a, b, preferred_element_type=F32)


def _dot_nt(a, b):
    return lax.dot_general(a, b, (((1,), (1,)), ((), ())), preferred_element_type=F32)


def _dot_tn(a, b):
    return lax.dot_general(a, b, (((0,), (0,)), ((), ())), preferred_element_type=F32)


def _split_bf16(x):
    hi = x.astype(BF16)
    lo = (x - hi.astype(F32)).astype(BF16)
    return hi, lo


def _t5_bucket_np(rel):
    n = np.maximum(rel, 0)
    max_exact = NUM_BUCKETS // 2
    nf = np.maximum(n, 1).astype(np.float32)
    large = max_exact + (np.log(nf / max_exact) / math.log(MAX_DISTANCE / max_exact)
                         * (NUM_BUCKETS - max_exact)).astype(np.int32)
    large = np.minimum(large, NUM_BUCKETS - 1)
    return np.where(n < max_exact, n, large).astype(np.int32)


def _bias_tables_kernel(relb_ref, bk_toe_ref, bk_far_ref, bk_last_ref, bk_new_ref,
                        lq1_ref, lk1_ref, lq2_ref, lk2_ref,
                        toe_ref, sb_far_ref, sb_last_ref, sb_new_ref, lam_ref, *, n_heads, lam_init):
    far = NUM_BUCKETS - 1

    def lookup(bk, h):
        out = jnp.full(bk.shape, NEG_INF, F32)
        for b in range(NUM_BUCKETS):
            out = jnp.where(bk == b, relb_ref[b, h] - relb_ref[far, h], out)
        return out

    for h in range(n_heads):
        toe_ref[0, h] = lookup(bk_toe_ref[0], h)
        toe_ref[1, h] = lookup(bk_toe_ref[1], h)

    def per_row_head(bk_ref, out_ref):
        row_h = lax.broadcasted_iota(jnp.int32, bk_ref.shape, 0) % n_heads
        out = jnp.zeros(bk_ref.shape, F32)
        for h in range(n_heads):
            out = jnp.where(row_h == h, lookup(bk_ref[...], h), out)
        out_ref[...] = -out

    per_row_head(bk_far_ref, sb_far_ref)
    per_row_head(bk_last_ref, sb_last_ref)
    per_row_head(bk_new_ref, sb_new_ref)
    d1 = jnp.sum(lq1_ref[...] * lk1_ref[...], axis=-1, keepdims=True)
    d2 = jnp.sum(lq2_ref[...] * lk2_ref[...], axis=-1, keepdims=True)
    lam = jnp.exp(d1) - jnp.exp(d2) + lam_init
    lam_ref[...] = jnp.broadcast_to(lam, lam_ref.shape)


def _bias_tables(rel_bias, lq1, lk1, lq2, lk2, *, n_heads, t_dec, lam_init):
    i = np.arange(LANES)[:, None]
    j = np.arange(LANES)[None, :]
    bk_toe = np.stack([np.where(j >= i, _t5_bucket_np(j - i), -1),
                       _t5_bucket_np(LANES + j - i)]).astype(np.int32)
    rows = 2 * t_dec * n_heads
    r = np.arange(rows)[:, None]
    t, h = (r // n_heads) % t_dec, r % n_heads
    c = np.arange(PAGE_SIZE * n_heads)[None, :]
    key, hk = c // n_heads, c % n_heads
    bk_far = np.where(hk == h, NUM_BUCKETS - 1, -1).astype(np.int32)
    bk_last = np.where(hk == h, _t5_bucket_np(PAGE_SIZE + t - key), -1).astype(np.int32)
    cn = c[:, :LANES]
    key, hk = cn // n_heads, cn % n_heads
    bk_new = np.where((hk == h) & (key <= t) & (key < t_dec), _t5_bucket_np(t - key), -1).astype(np.int32)
    vec = lambda a: a.reshape(1, -1).astype(F32)
    smem = pl.BlockSpec(memory_space=pltpu.SMEM)
    tab = lambda a: jax.ShapeDtypeStruct(a.shape, F32)
    return pl.pallas_call(
        functools.partial(_bias_tables_kernel, n_heads=n_heads, lam_init=lam_init),
        out_shape=(jax.ShapeDtypeStruct((2, n_heads, LANES, LANES), F32),
                   tab(bk_far), tab(bk_last), tab(bk_new),
                   jax.ShapeDtypeStruct((8, LANES), F32)),
        in_specs=[smem] + [pl.BlockSpec(memory_space=pltpu.VMEM)] * 8,
        name="bias_tables",
    )(rel_bias.astype(F32), jnp.asarray(bk_toe), jnp.asarray(bk_far), jnp.asarray(bk_last),
      jnp.asarray(bk_new), vec(lq1), vec(lk1), vec(lq2), vec(lk2))


def _attn_proj_kernel(x_ref, g_ref, w_ref, wlr_ref, wgh_ref, wgl_ref, bg_ref,
                      h_ref, qkv_ref, k32_ref, v32_ref, zf_ref, loga_ref, *maybe_vt_ref, n_attn, n_qk):
    h = _rms(x_ref[...], g_ref[...]).astype(BF16)
    h_ref[...] = h
    q = _dot_nt(h, w_ref[0:n_attn]) * DH_A ** -0.5
    qkv_ref[:, 0:n_attn] = q.astype(BF16)
    k = _dot_nt(h, w_ref[n_attn:2 * n_attn])
    k32_ref[...] = k
    qkv_ref[:, n_attn:2 * n_attn] = k.astype(BF16)
    v = _dot_nt(h, w_ref[2 * n_attn:3 * n_attn])
    v32_ref[...] = v
    qkv_ref[:, 2 * n_attn:3 * n_attn] = v.astype(BF16)
    for vt_ref in maybe_vt_ref:
        vt = v.T.astype(BF16)
        ones = jnp.ones((VT_ROWS - DV_A, vt.shape[1]), BF16)
        vt_ref[...] = jnp.concatenate(
            [part for h in range(n_attn // DV_A) for part in (vt[h * DV_A:(h + 1) * DV_A], ones)], axis=0)
    zf_ref[:, 0:n_qk] = _dot_nt(h, w_ref[3 * n_attn:3 * n_attn + n_qk]) * DK_B ** -0.5
    zf_ref[:, n_qk:2 * n_qk] = _dot_nt(h, w_ref[3 * n_attn + n_qk:3 * n_attn + 2 * n_qk])
    lr = _dot_nt(h, wlr_ref[...])
    lr_hi, lr_lo = _split_bf16(lr)
    pre = (_dot(lr_hi, wgh_ref[...]) + _dot(lr_hi, wgl_ref[...]) + _dot(lr_lo, wgh_ref[...])
           + bg_ref[...])
    log_sig = jnp.minimum(pre, 0.0) - jnp.log1p(jnp.exp(-jnp.abs(pre)))
    loga_ref[...] = log_sig / GATE_TAU


def _attn_proj(x, g, w_t, o_lr, wgh, wgl, bg, *, n_attn, tm, transposed_v):
    m, d = x.shape
    n_qk = H_B * DK_B
    n_lead = 3 * n_attn + 2 * n_qk
    assert o_lr % LANES == 0
    row = lambda n: pl.BlockSpec((tm, n), lambda i: (i, 0))
    out_specs = [row(d), row(3 * n_attn), row(n_attn), row(n_attn), row(2 * n_qk), row(n_qk)]
    out_shape = [jax.ShapeDtypeStruct((m, d), BF16),
                 jax.ShapeDtypeStruct((m, 3 * n_attn), BF16),
                 jax.ShapeDtypeStruct((m, n_attn), F32),
                 jax.ShapeDtypeStruct((m, n_attn), F32),
                 jax.ShapeDtypeStruct((m, 2 * n_qk), F32),
                 jax.ShapeDtypeStruct((m, n_qk), F32)]
    if transposed_v:
        vt_rows = n_attn // DV_A * VT_ROWS
        out_specs.append(pl.BlockSpec((vt_rows, tm), lambda i: (0, i)))
        out_shape.append(jax.ShapeDtypeStruct((vt_rows, m), BF16))
    once = pl.Buffered(1)
    return pl.pallas_call(
        functools.partial(_attn_proj_kernel, n_attn=n_attn, n_qk=n_qk),
        grid=(m // tm,),
        in_specs=[row(d), _const_spec(g.shape),
                  pl.BlockSpec((n_lead, d), lambda i: (0, 0), pipeline_mode=once),
                  pl.BlockSpec((LANES, d), lambda i: (o_lr // LANES, 0), pipeline_mode=once),
                  _const_spec(wgh.shape), _const_spec(wgl.shape), _const_spec(bg.shape)],
        out_specs=tuple(out_specs),
        out_shape=tuple(out_shape),
        compiler_params=_cparams("parallel"),
        name="attn_proj",
    )(x, g, w_t, w_t, wgh, wgl, bg)


def _sigmoid(x):
    return 0.5 * jnp.tanh(0.5 * x) + 0.5


_ACTS = {
    'id': lambda x: x,
    'silu': lambda x: x * _sigmoid(x),
    'sigmoid': _sigmoid,
}


def _act_proj_kernel(h_ref, w_ref, o_ref, *, acts, cw):
    h = h_ref[...]
    accs = [_dot_nt(h, w_ref[c * cw:(c + 1) * cw]) for c in range(len(acts))]
    for c, (act, acc) in enumerate(zip(acts, accs)):
        o_ref[:, c * cw:(c + 1) * cw] = _ACTS[act](acc).astype(BF16)


def _act_proj(h, w_t, row0, n, acts, *, tm, tn, name):
    m, d = h.shape
    cw = tn // len(acts)
    assert row0 % tn == 0 and n % tn == 0 and cw % LANES == 0
    j0 = row0 // tn
    nj = n // tn
    w_mode = dict(pipeline_mode=pl.Buffered(1)) if nj == 1 else {}
    return pl.pallas_call(
        functools.partial(_act_proj_kernel, acts=acts, cw=cw),
        grid=(m // tm, nj),
        in_specs=[pl.BlockSpec((tm, d), lambda i, j: (i, 0)),
                  pl.BlockSpec((tn, d), lambda i, j: (j0 + j, 0), **w_mode)],
        out_specs=pl.BlockSpec((tm, tn), lambda i, j: (i, j)),
        out_shape=jax.ShapeDtypeStruct((m, n), BF16),
        compiler_params=_cparams("parallel", "arbitrary"),
        name=name,
    )(h, w_t)


def _softmax_step(s, v, m_scr, l_scr, acc_scr):
    m_prev = m_scr[...]
    m_new = jnp.maximum(m_prev, jnp.max(s, axis=-1, keepdims=True))
    alpha = jnp.exp(m_prev - m_new)
    p = jnp.exp(s - m_new)
    l_scr[...] = alpha * l_scr[...] + jnp.sum(p, axis=-1, keepdims=True)
    acc_scr[...] = alpha * acc_scr[...] + _dot(p.astype(BF16), v)
    m_scr[...] = m_new


def _head_norm(o, g, lam_init):
    return o * lax.rsqrt(jnp.mean(o * o, axis=-1, keepdims=True) + EPS) * g * (1.0 - lam_init)


def _attn_prompt_kernel(lam_ref, q_ref, k_ref, vt_ref, toe_ref, g_ref, o_ref,
                        bias_scr, *scr, tq, cb, ahead, lam_init):
    qi = pl.program_id(2)
    nb = tq // LANES

    @pl.when(qi == 0)
    def _():
        for r in range(nb):
            for c in range(nb):
                blk = (slice(r * LANES, (r + 1) * LANES), slice(c * LANES, (c + 1) * LANES))
                for t, d in ((0, tq + (c - r) * LANES), (1, (c - r) * LANES)):
                    if d < 0:
                        bias_scr[(t,) + blk] = jnp.full((LANES, LANES), -NEG_INF, F32)
                    elif d == 0:
                        bias_scr[(t,) + blk] = toe_ref[0] * -LOG2_E
                    elif d == LANES:
                        bias_scr[(t,) + blk] = toe_ref[1] * -LOG2_E
                    else:
                        bias_scr[(t,) + blk] = jnp.zeros((LANES, LANES), F32)

    n_cb = 2 * tq // cb
    chains = [scr[3 * c:3 * c + 3] for c in range(n_cb)]
    q = (q_ref[...].astype(F32) * LOG2_E).astype(BF16)
    lane = lax.broadcasted_iota(jnp.int32, (cb, q.shape[1]), 1)
    for c, (q2_scr, m_scr, acc_scr) in enumerate(chains):
        q_blk = q[(c * cb) % tq:(c * cb) % tq + cb]
        keep = (lane < DH_A) if c * cb < tq else (lane >= DH_A)
        q2_scr[...] = jnp.where(keep, q_blk, jnp.zeros_like(q_blk))
        m_scr[...] = jnp.full(m_scr.shape, NEG_INF, F32)
        acc_scr[...] = jnp.zeros(acc_scr.shape, F32)

    def kv_tiles(js, bias_idx):
        offs = [pl.multiple_of(j * tq, tq) for j in js]
        ks = [k_ref[pl.ds(off, tq), :] for off in offs]
        vts = [vt_ref[:, pl.ds(off, tq)] for off in offs]
        bias = None if bias_idx is None else [
            [bias_scr[bi, :, b0:b0 + cb] for b0 in range(0, tq, cb)] for bi in bias_idx]
        blocks = [(t, c) for t in range(len(js)) for c in range(n_cb)]
        score = lambda t, c: _dot_nt(ks[t], chains[c][0][...])
        pending = [score(t, c) for t, c in blocks[:ahead]]
        for n, (t, c) in enumerate(blocks):
            _, m_scr, acc_scr = chains[c]
            s = pending.pop(0)
            if n + ahead < len(blocks):
                pending.append(score(*blocks[n + ahead]))
            if bias is not None:
                s = s - bias[t][c % len(bias[t])]
            m_prev = m_scr[...]
            m_new = jnp.maximum(m_prev, jnp.max(s, axis=0, keepdims=True))
            alpha = jnp.exp2(m_prev - m_new)
            p = jnp.exp2((s - m_new).astype(BF16))
            acc_scr[...] = alpha * acc_scr[...] + _dot(vts[t], p)
            m_scr[...] = m_new

    def far_pair(i, carry):
        kv_tiles([2 * i, 2 * i + 1], None)
        return carry

    def far_tile(j, carry):
        kv_tiles([j], None)
        return carry

    n_far = jnp.maximum(qi - 1, 0)
    lax.fori_loop(0, n_far // 2, far_pair, 0)
    lax.fori_loop(n_far // 2 * 2, n_far, far_tile, 0)

    @pl.when(qi == 0)
    def _():
        kv_tiles([0], [1])

    @pl.when(qi > 0)
    def _():
        kv_tiles([qi - 1, qi], [0, 1])

    on = jnp.concatenate([acc_scr[0:DV_A] / acc_scr[DV_A:DV_A + 1] for _, _, acc_scr in chains], axis=1)
    lam = lam_ref[0:1, 0:1]
    o_t = on[:, 0:tq] - lam * on[:, tq:2 * tq]
    o_t = o_t * lax.rsqrt(jnp.mean(o_t * o_t, axis=0, keepdims=True) + EPS)
    o_ref[...] = (o_t.T * (g_ref[...] * (1.0 - lam_init))).astype(BF16)


def _attn_prompt(qkv, vt, toe, lam, g_head, *, batch, seq, n_heads, tq, lam_init):
    nq = seq // tq
    koff = n_heads
    cb = min(tq, 2 * LANES)
    return pl.pallas_call(
        functools.partial(_attn_prompt_kernel, tq=tq, cb=cb, ahead=4, lam_init=lam_init),
        grid=(batch, n_heads, nq),
        in_specs=[pl.BlockSpec(lam.shape, lambda b, h, i: (0, 0)),
                  pl.BlockSpec((tq, LANES), lambda b, h, i: (b * nq + i, h)),
                  pl.BlockSpec((seq, LANES), lambda b, h, i: (b, koff + h)),
                  pl.BlockSpec((VT_ROWS, seq), lambda b, h, i: (h, b)),
                  pl.BlockSpec((2, None, LANES, LANES), lambda b, h, i: (0, h, 0, 0)),
                  pl.BlockSpec(g_head.shape, lambda b, h, i: (0, 0))],
        out_specs=pl.BlockSpec((tq, LANES), lambda b, h, i: (b * nq + i, h)),
        out_shape=jax.ShapeDtypeStruct((batch * seq, n_heads * LANES), BF16),
        scratch_shapes=[pltpu.VMEM((2, tq, tq), F32)]
        + [pltpu.VMEM((cb, LANES), BF16), pltpu.VMEM((1, cb), F32),
           pltpu.VMEM((VT_ROWS, cb), F32)] * (2 * tq // cb),
        compiler_params=_cparams("parallel", "parallel", "arbitrary"),
        name="attn_prompt",
    )(lam, qkv, qkv, vt, toe, g_head)


def _attn_sample_kernel(pt_ref, lam_ref, q_ref, kn_ref, vn_ref, sbf_ref, sbl_ref, sbn_ref, g_ref, *rest,
                        pps, bp, lam_init, g=None, last=None):
    k_refs = rest[:pps]
    v_refs = rest[pps:2 * pps]
    o_ref = rest[2 * pps]
    m_scr, l_scr, acc_scr = rest[2 * pps + 1:]
    if g is None:
        g = pl.program_id(1)
        last = pl.num_programs(1) - 1

    @pl.when(g == 0)
    def _():
        m_scr[...] = jnp.full(m_scr.shape, NEG_INF, F32)
        l_scr[...] = jnp.zeros(l_scr.shape, F32)
        acc_scr[...] = jnp.zeros(acc_scr.shape, F32)

    q = q_ref[...]
    cols = sbf_ref.shape[1]
    flat = lambda r: r[...].reshape(cols, r.shape[-1]).astype(BF16)
    pages = lambda refs, b: jnp.concatenate([flat(r) for r in refs[b * bp:(b + 1) * bp]], axis=0)
    scores = [_dot_nt(q, pages(k_refs, b)) for b in range(pps // bp)]
    far = sbf_ref[...]
    tail = jnp.where(g == last, sbl_ref[...], far)
    for b, s in enumerate(scores):
        s = jnp.concatenate([s[:, i * cols:(i + 1) * cols] - (far if b * bp + i < pps - 1 else tail)
                             for i in range(bp)], axis=1)
        _softmax_step(s, pages(v_refs, b), m_scr, l_scr, acc_scr)

    @pl.when(g == last)
    def _():
        pad = jnp.zeros((sbn_ref.shape[1] - kn_ref.shape[0], kn_ref.shape[1]), BF16)
        kn = jnp.concatenate([kn_ref[...], pad], axis=0)
        vn = jnp.concatenate([vn_ref[...], pad], axis=0)
        _softmax_step(_dot_nt(q, kn) - sbn_ref[...], vn, m_scr, l_scr, acc_scr)
        on = acc_scr[...] / l_scr[...]
        half = on.shape[0] // 2
        o = on[0:half] - lam_ref[0:1, 0:1] * on[half:2 * half]
        o_ref[...] = _head_norm(o, g_ref[...], lam_init).astype(BF16)


def _attn_sample(q, kn, vn, cache_k, cache_v, page_table, layer, sb_far, sb_last, sb_new, lam, g_head,
                 *, pps, lam_init):
    nb, rows, feat = q.shape
    n_pages = page_table.shape[1]
    steps = n_pages // pps
    seq_spec = lambda a: pl.BlockSpec((None,) + a.shape[1:], lambda b, g, pt: (b, 0, 0))
    full = lambda a: pl.BlockSpec(a.shape, lambda b, g, pt: (0, 0))

    def page_spec(jj):
        return pl.BlockSpec((None, None) + cache_k.shape[2:],
                            lambda b, g, pt: (layer, pt[b, g * pps + jj], 0, 0, 0))

    grid_spec = pltpu.PrefetchScalarGridSpec(
        num_scalar_prefetch=1,
        grid=(nb, steps),
        in_specs=[full(lam), seq_spec(q), seq_spec(kn), seq_spec(vn), full(sb_far), full(sb_last),
                  full(sb_new), full(g_head)] + [page_spec(jj) for jj in range(pps)] * 2,
        out_specs=pl.BlockSpec((None, rows // 2, feat), lambda b, g, pt: (b, 0, 0)),
        scratch_shapes=[pltpu.VMEM((rows, 1), F32), pltpu.VMEM((rows, 1), F32),
                        pltpu.VMEM((rows, feat), F32)],
    )
    return pl.pallas_call(
        functools.partial(_attn_sample_kernel, pps=pps, bp=min(pps, 2), lam_init=lam_init),
        grid_spec=grid_spec,
        out_shape=jax.ShapeDtypeStruct((nb, rows // 2, feat), BF16),
        compiler_params=_cparams("parallel", "arbitrary"),
        name="attn_sample",
    )(page_table, lam, q, kn, vn, sb_far, sb_last, sb_new, g_head, *([cache_k] * pps), *([cache_v] * pps))


def _attn_fused_kernel(pt_ref, lam_ref, q_ref, k_ref, vt_ref, toe_ref, g_ref,
                       qs_ref, kn_ref, vn_ref, sbf_ref, sbl_ref, sbn_ref, *rest,
                       pps, bp, steps_s, n_prompt_scr, tq, cb, ahead, lam_init):
    pages = rest[:2 * pps]
    o_ref, os_ref = rest[2 * pps:2 * pps + 2]
    scr = rest[2 * pps + 2:]
    step = (pl.program_id(0) * pl.num_programs(1) + pl.program_id(1)) * pl.num_programs(2) + pl.program_id(2)
    _attn_sample_kernel(pt_ref, lam_ref, qs_ref, kn_ref, vn_ref, sbf_ref, sbl_ref, sbn_ref, g_ref,
                        *pages, os_ref, *scr[n_prompt_scr:], pps=pps, bp=bp, lam_init=lam_init,
                        g=step % steps_s, last=steps_s - 1)
    _attn_prompt_kernel(lam_ref, q_ref, k_ref, vt_ref, toe_ref, g_ref, o_ref, *scr[:n_prompt_scr],
                        tq=tq, cb=cb, ahead=ahead, lam_init=lam_init)


def _attn_fused(qkv, vt, toe, qs, kn, vn, cache_k, cache_v, page_table, layer, sb_far, sb_last, sb_new,
                lam, g_head, *, batch, seq, n_heads, tq, pps, lam_init):
    nq = seq // tq
    koff = n_heads
    cb = min(tq, 2 * LANES)
    nb, rows, feat = qs.shape
    steps_s = page_table.shape[1] // pps
    assert batch * n_heads * nq == nb * steps_s
    flat = lambda b, h, i: (b * n_heads + h) * nq + i
    full = lambda a: pl.BlockSpec(a.shape, lambda b, h, i, pt: (0,) * a.ndim)
    seq_spec = lambda a: pl.BlockSpec((None,) + a.shape[1:], lambda b, h, i, pt: (flat(b, h, i) // steps_s, 0, 0))

    def page_spec(jj):
        def index(b, h, i, pt):
            s = flat(b, h, i)
            return (layer, pt[s // steps_s, (s % steps_s) * pps + jj], 0, 0, 0)
        return pl.BlockSpec((None, None) + cache_k.shape[2:], index)

    prompt_scr = ([pltpu.VMEM((2, tq, tq), F32)]
                  + [pltpu.VMEM((cb, LANES), BF16), pltpu.VMEM((1, cb), F32),
                     pltpu.VMEM((VT_ROWS, cb), F32)] * (2 * tq // cb))
    sample_scr = [pltpu.VMEM((rows, 1), F32), pltpu.VMEM((rows, 1), F32), pltpu.VMEM((rows, feat), F32)]
    grid_spec = pltpu.PrefetchScalarGridSpec(
        num_scalar_prefetch=1,
        grid=(batch, n_heads, nq),
        in_specs=[full(lam),
                  pl.BlockSpec((tq, LANES), lambda b, h, i, pt: (b * nq + i, h)),
                  pl.BlockSpec((seq, LANES), lambda b, h, i, pt: (b, koff + h)),
                  pl.BlockSpec((VT_ROWS, seq), lambda b, h, i, pt: (h, b)),
                  pl.BlockSpec((2, None, LANES, LANES), lambda b, h, i, pt: (0, h, 0, 0)),
                  full(g_head), seq_spec(qs), seq_spec(kn), seq_spec(vn), full(sb_far), full(sb_last),
                  full(sb_new)] + [page_spec(jj) for jj in range(pps)] * 2,
        out_specs=(pl.BlockSpec((tq, LANES), lambda b, h, i, pt: (b * nq + i, h)),
                   pl.BlockSpec((None, rows // 2, feat), lambda b, h, i, pt: (flat(b, h, i) // steps_s, 0, 0))),
        scratch_shapes=prompt_scr + sample_scr,
    )
    return pl.pallas_call(
        functools.partial(_attn_fused_kernel, pps=pps, bp=min(pps, 2), steps_s=steps_s,
                          n_prompt_scr=len(prompt_scr), tq=tq, cb=cb, ahead=4, lam_init=lam_init),
        grid_spec=grid_spec,
        out_shape=(jax.ShapeDtypeStruct((batch * seq, n_heads * LANES), BF16),
                   jax.ShapeDtypeStruct((nb, rows // 2, feat), BF16)),
        compiler_params=_cparams("arbitrary", "arbitrary", "arbitrary"),
        name="attn_fused",
    )(page_table, lam, qkv, qkv, vt, toe, g_head, qs, kn, vn, sb_far, sb_last, sb_new,
      *([cache_k] * pps), *([cache_v] * pps))


def _gla_chunk(heads, *, sub):
    c_len = heads[0][0].shape[0]
    ri = lax.broadcasted_iota(jnp.int32, (c_len, c_len), 0)
    ci = lax.broadcasted_iota(jnp.int32, (c_len, c_len), 1)
    row = lax.broadcasted_iota(jnp.int32, (c_len, 1), 0)
    tril = (ri >= ci).astype(BF16)
    bs = []
    for _, _, log_a, _, _ in heads:
        a_hi, a_lo = _split_bf16(log_a)
        bs.append(_dot(tril, a_hi) + _dot(tril, a_lo))
    outs, states = [], []
    for (q, k, _, v, st), b in zip(heads, bs):
        b_last = b[c_len - 1:c_len]
        outs.append(_dot_nt((q * jnp.exp(b)).astype(BF16), st.astype(BF16)))
        k_out = (k * jnp.exp(b_last - b)).astype(BF16)
        states.append(st * jnp.exp(b_last) + _dot_tn(v, k_out))
    atts = []
    for (q, k, _, _, _), b in zip(heads, bs):
        rows = []
        for i in range(c_len // sub):
            lo, hi = i * sub, (i + 1) * sub
            b_ref = b[lo:lo + 1]
            qh = (q[lo:hi] * jnp.exp(b[lo:hi] - b_ref)).astype(BF16)
            kh = (k * jnp.exp(jnp.where(row < hi, b_ref - b, NEG_INF))).astype(BF16)
            rows.append(_dot_nt(qh, kh))
        att = jnp.concatenate(rows, axis=0) if len(rows) > 1 else rows[0]
        atts.append(jnp.where(ri >= ci, att, 0.0).astype(BF16))
    outs = [o + _dot(att, v) for o, att, (_, _, _, v, _) in zip(outs, atts, heads)]
    return list(zip(outs, states))


def _gla_out(o, g, r):
    return (o * lax.rsqrt(jnp.mean(o * o, axis=-1, keepdims=True) + EPS) * g * r.astype(F32)).astype(BF16)


def _gla_prompt_kernel(qk_ref, la_ref, v_ref, r_ref, g_ref, o_ref, s_ref, st_scr, *, batch, sub):
    ci = pl.program_id(0)

    @pl.when(ci == 0)
    def _():
        st_scr[...] = jnp.zeros(st_scr.shape, F32)

    nk = H_B * DK_B
    ids = [(b, h) for b in range(batch) for h in range(H_B)]
    ks = lambda h: slice(h * DK_B, (h + 1) * DK_B)
    vs = lambda h: slice(h * DV_B, (h + 1) * DV_B)
    heads = [(qk_ref[b, :, ks(h)], qk_ref[b, :, nk + h * DK_B:nk + (h + 1) * DK_B], la_ref[b, :, ks(h)],
              v_ref[b, :, vs(h)], st_scr[b, h]) for b, h in ids]
    for (b, h), (o, st_new) in zip(ids, _gla_chunk(heads, sub=sub)):
        st_scr[b, h] = st_new
        o_ref[b, :, vs(h)] = _gla_out(o, g_ref[...], r_ref[b, :, vs(h)])

    @pl.when(ci == pl.num_programs(0) - 1)
    def _():
        for b in range(batch):
            for h in range(H_B):
                s_ref[b, h] = st_scr[b, h].T


def _gla_prompt(zf, log_a, zb, g_head, *, batch, seq, chunk, sub):
    nqk = zf.shape[1]
    nv = H_B * DV_B
    zf3 = zf.reshape(batch, seq, nqk)
    la3 = log_a.reshape(batch, seq, log_a.shape[1])
    zb3 = zb.reshape(batch, seq, zb.shape[1])
    blk = lambda n, col: pl.BlockSpec((batch, chunk, n), lambda c: (0, c, col))
    return pl.pallas_call(
        functools.partial(_gla_prompt_kernel, batch=batch, sub=sub),
        grid=(seq // chunk,),
        in_specs=[blk(nqk, 0), blk(la3.shape[2], 0), blk(nv, 0), blk(nv, 1),
                  pl.BlockSpec(g_head.shape, lambda c: (0, 0))],
        out_specs=(blk(nv, 0),
                   pl.BlockSpec((batch, H_B, DK_B, DV_B), lambda c: (0, 0, 0, 0))),
        out_shape=(jax.ShapeDtypeStruct((batch, seq, nv), BF16),
                   jax.ShapeDtypeStruct((batch, H_B, DK_B, DV_B), F32)),
        scratch_shapes=[pltpu.VMEM((batch, H_B, DV_B, DK_B), F32)],
        compiler_params=_cparams("arbitrary"),
        name="gla_prompt",
    )(zf3, la3, zb3, zb3, g_head)


def _gla_sample_kernel(qk_ref, la_ref, v_ref, r_ref, g_ref, s0_ref, o_ref, s_ref, *, sub):
    nk = H_B * DK_B
    ks = lambda h: slice(h * DK_B, (h + 1) * DK_B)
    vs = lambda h: slice(h * DV_B, (h + 1) * DV_B)
    heads = [(qk_ref[:, ks(h)], qk_ref[:, nk + h * DK_B:nk + (h + 1) * DK_B], la_ref[:, ks(h)],
              v_ref[:, vs(h)], s0_ref[h].T) for h in range(H_B)]
    for h, (o, st_new) in enumerate(_gla_chunk(heads, sub=sub)):
        s_ref[h] = st_new.T
        o_ref[:, vs(h)] = _gla_out(o, g_ref[...], r_ref[:, vs(h)])


def _gla_sample(zf3, la3, zb3, g_head, s0):
    nb, tp, nqk = zf3.shape
    nv = H_B * DV_B
    blk = lambda n, col: pl.BlockSpec((None, tp, n), lambda b: (b, 0, col))
    st = pl.BlockSpec((None, H_B, DK_B, DV_B), lambda b: (b, 0, 0, 0))
    return pl.pallas_call(
        functools.partial(_gla_sample_kernel, sub=tp),
        grid=(nb,),
        in_specs=[blk(nqk, 0), blk(la3.shape[2], 0), blk(nv, 0), blk(nv, 1),
                  pl.BlockSpec(g_head.shape, lambda b: (0, 0)), st],
        out_specs=(blk(nv, 0), st),
        out_shape=(jax.ShapeDtypeStruct((nb, tp, nv), BF16),
                   jax.ShapeDtypeStruct(s0.shape, F32)),
        compiler_params=_cparams("parallel"),
        name="gla_sample",
    )(zf3, la3, zb3, zb3, g_head, s0)


def _merge_kernel(oa_ref, ob_ref, sga_ref, sgb_ref, x_ref, wpd_ref, wpg_ref, wo_ref, g1_ref, g2_ref,
                  x1_ref, h2_ref):
    m = (sga_ref[...].astype(F32) * _dot(oa_ref[...], wpd_ref[...])
         + sgb_ref[...].astype(F32) * _dot(ob_ref[...], wpg_ref[...]))
    mix = _dot(m.astype(BF16), wo_ref[...])
    x1 = x_ref[...] + _rms(mix, g1_ref[...])
    x1_ref[...] = x1
    h2_ref[...] = _rms(x1, g2_ref[...]).astype(BF16)


def _merge(oa, ob, zg, x, wpd, wpg, wo, g1, g2, *, tm):
    m, d = x.shape
    na, nb_ = oa.shape[1], ob.shape[1]
    row = lambda n, col=0: pl.BlockSpec((tm, n), lambda i: (i, col))
    return pl.pallas_call(
        _merge_kernel,
        grid=(m // tm,),
        in_specs=[row(na), row(nb_), row(d, 0), row(d, 1), row(d),
                  _const_spec(wpd.shape), _const_spec(wpg.shape), _const_spec(wo.shape),
                  _const_spec(g1.shape), _const_spec(g2.shape)],
        out_specs=(row(d), row(d)),
        out_shape=(jax.ShapeDtypeStruct((m, d), F32), jax.ShapeDtypeStruct((m, d), BF16)),
        compiler_params=_cparams("parallel"),
        name="merge",
    )(oa, ob, zg, zg, x, wpd, wpg, wo, g1, g2)


def _geglu(cg, cv):
    return (jax.nn.gelu(cg, approximate=True) * cv).astype(BF16)


def _ffn_finish(acc, x1_ref, g_ref, y_ref):
    y_ref[...] = x1_ref[...] + _rms(acc, g_ref[...])


def _ffn_prompt_kernel(hp_ref, h_ref, x1_ref, wug_ref, wuv_ref, wcg_ref, wcv_ref, bcg_ref, bcv_ref,
                       wd_ref, g_ref, y_ref, csg_ref, csv_ref, *, tm, rb, halo, cw, seq_tiles):
    i = pl.program_id(0)
    j = pl.program_id(1)

    @pl.when(j == 0)
    def _():
        y_ref[...] = jnp.zeros(y_ref.shape, F32)

    hp = jnp.where(i % seq_tiles == 0, jnp.zeros_like(hp_ref[...]), hp_ref[...])
    n_rb = tm // rb
    lhs = [jnp.concatenate([hp, h_ref[0:rb]], axis=0)] + [h_ref[r * rb:(r + 1) * rb] for r in range(1, n_rb)]
    chunks = [slice(c0, c0 + cw) for c0 in range(0, wug_ref.shape[1], cw)]

    ups = [[(_dot(x, wug_ref[:, cols]), _dot(x, wuv_ref[:, cols])) for cols in chunks] for x in lhs]

    def conv(u, cols, wc_ref, bc_ref):
        return (bc_ref[:, cols] + wc_ref[0:1, cols] * u[halo - 2:halo - 2 + rb]
                + wc_ref[1:2, cols] * u[halo - 1:halo - 1 + rb] + wc_ref[2:3, cols] * u[halo:halo + rb])

    for r in range(n_rb):
        acts = []
        for c, cols in enumerate(chunks):
            ug, uv = ups[r][c]
            if r > 0:
                top = rb if r > 1 else rb + halo
                ug = jnp.concatenate([ups[r - 1][c][0][top - halo:top], ug], axis=0)
                uv = jnp.concatenate([ups[r - 1][c][1][top - halo:top], uv], axis=0)
            if r == n_rb - 1:
                csg_ref[:, cols] = ug[halo + rb - 8:halo + rb]
                csv_ref[:, cols] = uv[halo + rb - 8:halo + rb]
            acts.append(_geglu(conv(ug, cols, wcg_ref, bcg_ref), conv(uv, cols, wcv_ref, bcv_ref)))
        a = jnp.concatenate(acts, axis=1) if len(acts) > 1 else acts[0]
        y_ref[r * rb:(r + 1) * rb] += _dot(a, wd_ref[...])

    @pl.when(j == pl.num_programs(1) - 1)
    def _():
        _ffn_finish(y_ref[...], x1_ref, g_ref, y_ref)


def _ffn_prompt(h2, x1, w_up, w_conv, b_conv, w_down, g, *, seq, tm, tf):
    m, d = x1.shape
    d_ff = w_down.shape[0]
    nf = d_ff // tf
    halo = SUBLANES_BF16
    hb = tm // halo
    col = lambda rows, off: pl.BlockSpec((rows, tf), lambda i, j: (0, j + off))
    tok = lambda n: pl.BlockSpec((tm, n), lambda i, j: (i, 0))
    cs = pl.BlockSpec((8, tf), lambda i, j: (i, j))
    cs_shape = jax.ShapeDtypeStruct((m // tm * 8, d_ff), F32)
    cw = min(tf, 2 * LANES)
    rb = min(tm, 2 * LANES)
    x1_spec = pl.BlockSpec((tm, d), lambda i, j: (i, 0), pipeline_mode=pl.Buffered(1))
    return pl.pallas_call(
        functools.partial(_ffn_prompt_kernel, tm=tm, rb=rb, halo=halo, cw=cw, seq_tiles=seq // tm),
        grid=(m // tm, nf),
        in_specs=[pl.BlockSpec((halo, d), lambda i, j: (jnp.maximum(i * hb - 1, 0), 0)),
                  tok(d), x1_spec, col(d, 0), col(d, nf), col(CONV_W, 0), col(CONV_W, nf),
                  col(1, 0), col(1, nf), pl.BlockSpec((tf, d), lambda i, j: (j, 0)),
                  pl.BlockSpec(g.shape, lambda i, j: (0, 0))],
        out_specs=(tok(d), cs, cs),
        out_shape=(jax.ShapeDtypeStruct((m, d), F32), cs_shape, cs_shape),
        compiler_params=_cparams("parallel", "arbitrary"),
        name="ffn_prompt",
    )(h2, h2, x1, w_up, w_up, w_conv, w_conv, b_conv, b_conv, w_down, g)


def _ffn_sample_kernel(h_ref, x1_ref, ctxg_ref, ctxv_ref, wug_ref, wuv_ref, wcg_ref, wcv_ref,
                       bcg_ref, bcv_ref, wd_ref, g_ref, y_ref, csg_ref, csv_ref, acc_scr, *, nb):
    j = pl.program_id(0)
    m = h_ref.shape[0]
    h = h_ref[...]

    def conv(w_up_ref, ctx_ref, wc_ref, bc_ref, cs_ref):
        u = _dot(h, w_up_ref[...])
        up = jnp.concatenate([ctx_ref[0], ctx_ref[1], u], axis=0)
        cs_ref[0] = u[m - 2 * nb:m - nb]
        cs_ref[1] = u[m - nb:m]
        return (bc_ref[...] + wc_ref[0:1] * up[0:m] + wc_ref[1:2] * up[nb:nb + m]
                + wc_ref[2:3] * up[2 * nb:2 * nb + m])

    a = _geglu(conv(wug_ref, ctxg_ref, wcg_ref, bcg_ref, csg_ref),
               conv(wuv_ref, ctxv_ref, wcv_ref, bcv_ref, csv_ref))
    part = _dot(a, wd_ref[...])

    @pl.when(j == 0)
    def _():
        acc_scr[...] = part

    @pl.when(j > 0)
    def _():
        acc_scr[...] += part

    @pl.when(j == pl.num_programs(0) - 1)
    def _():
        _ffn_finish(acc_scr[...], x1_ref, g_ref, y_ref)


def _ffn_sample(h2, x1, ctx, w_up, w_conv, b_conv, w_down, g, *, tf):
    m, d = x1.shape
    nb = ctx.shape[1]
    d_ff = w_down.shape[0]
    nf = d_ff // tf
    col = lambda rows, off: pl.BlockSpec((rows, tf), lambda j: (0, j + off))
    full = lambda a: pl.BlockSpec(a.shape, lambda j: (0, 0))
    ctx_spec = lambda off: pl.BlockSpec((CONV_W - 1, nb, tf), lambda j: (0, 0, j + off))
    cs = pl.BlockSpec((CONV_W - 1, nb, tf), lambda j: (0, 0, j))
    cs_shape = jax.ShapeDtypeStruct((CONV_W - 1, nb, d_ff), F32)
    return pl.pallas_call(
        functools.partial(_ffn_sample_kernel, nb=nb),
        grid=(nf,),
        in_specs=[full(h2), full(x1), ctx_spec(0), ctx_spec(nf), col(d, 0), col(d, nf),
                  col(CONV_W, 0), col(CONV_W, nf), col(1, 0), col(1, nf),
                  pl.BlockSpec((tf, d), lambda j: (j, 0)), full(g)],
        out_specs=(full(x1), cs, cs),
        out_shape=(jax.ShapeDtypeStruct((m, d), F32), cs_shape, cs_shape),
        scratch_shapes=[pltpu.VMEM((m, d), F32)],
        compiler_params=_cparams("arbitrary"),
        name="ffn_sample",
    )(h2, x1, ctx, ctx, w_up, w_up, w_conv, w_conv, b_conv, b_conv, w_down, g)


def _pick(n, candidates):
    for c in candidates:
        if n % c == 0:
            return c
    raise ValueError(f"no tile in {candidates} divides {n}")


def _layer_weights(l, p):
    d, n_in = p['w_in'].shape[1:]
    n_attn = p['cache_heads'] * 2 * DH_A
    w = p['w_in'][l]
    o_gla = 3 * n_attn
    o_lr = o_gla + 2 * H_B * DK_B + 2 * H_B * DV_B
    o_gate = o_lr + GATE_RANK
    assert n_in == o_gate + 2 * d
    w_lead_t = w.T.astype(BF16)
    w_gate_t = w_lead_t[o_gate:]
    wg = jnp.concatenate([p['w_gla_gate'][l].astype(F32),
                          jnp.zeros((LANES - GATE_RANK, H_B * DK_B), F32)], axis=0)
    wg_hi = wg.astype(BF16)
    wg_lo = (wg - wg_hi.astype(F32)).astype(BF16)
    row = lambda a: a[l].reshape(1, -1).astype(F32)
    return dict(
        w_lead_t=w_lead_t, w_gate_t=w_gate_t, n_attn=n_attn, o_vr=o_gla + 2 * H_B * DK_B, o_lr=o_lr,
        wg_hi=wg_hi, wg_lo=wg_lo, bg=row(p['b_gla_gate']),
        g_pre_mix=row(p['g_pre_mix']), g_head_diff=row(p['g_head_diff']), g_head_gla=row(p['g_head_gla']),
        wpd=p['w_proj_diff'][l].astype(BF16), wpg=p['w_proj_gla'][l].astype(BF16),
        wo=p['w_out'][l].astype(BF16), g_post_mix=row(p['g_post_mix']), g_pre_ffn=row(p['g_pre_ffn']),
        w_up=p['w_up'][l].astype(BF16), w_conv=p['w_conv'][l].astype(F32), b_conv=row(p['b_conv']),
        w_down=p['w_down'][l].astype(BF16), g_post_ffn=row(p['g_post_ffn']))


def _token_stages(x, lw, *, tm_proj, tm_mix, transposed_v):
    h, *proj = _attn_proj(x, lw['g_pre_mix'], lw['w_lead_t'], lw['o_lr'], lw['wg_hi'], lw['wg_lo'],
                          lw['bg'], n_attn=lw['n_attn'], tm=tm_proj, transposed_v=transposed_v)
    nv = H_B * DV_B
    zb = _act_proj(h, lw['w_lead_t'], lw['o_vr'], 2 * nv, ('id',) * 4 + ('silu',) * 4,
                   tm=tm_mix, tn=2 * nv, name="gla_proj")
    zg = _act_proj(h, lw['w_gate_t'], 0, lw['w_gate_t'].shape[0], ('sigmoid',) * 4,
                   tm=tm_mix, tn=nv, name="gate_proj")
    return proj, zb, zg


def kernel(x_prompt, x_sample, cache_k, cache_v, page_table, state_gla, state_conv, rel_bias,
           g_pre_mix, w_in, lambda_q1, lambda_k1, lambda_q2, lambda_k2, g_head_diff,
           w_gla_gate, b_gla_gate, g_head_gla, w_proj_diff, w_proj_gla, w_out, g_post_mix,
           g_pre_ffn, w_up, w_conv, b_conv, w_down, g_post_ffn):
    batch, seq, d = x_prompt.shape
    nb, t_dec, _ = x_sample.shape
    depth, n_pool, page, n_heads, kw = cache_k.shape
    assert page == PAGE_SIZE and kw == 2 * DH_A and cache_v.shape[-1] == DV_A
    d_ff = w_down.shape[1]
    width = n_heads * DV_A
    params = dict(w_in=w_in, w_gla_gate=w_gla_gate, b_gla_gate=b_gla_gate, g_pre_mix=g_pre_mix,
                  g_head_diff=g_head_diff, g_head_gla=g_head_gla, w_proj_diff=w_proj_diff,
                  w_proj_gla=w_proj_gla, w_out=w_out, g_post_mix=g_post_mix, g_pre_ffn=g_pre_ffn,
                  w_up=w_up, w_conv=w_conv, b_conv=b_conv, w_down=w_down, g_post_ffn=g_post_ffn,
                  cache_heads=n_heads)

    mp = batch * seq
    ms = nb * t_dec
    tq = _pick(seq, (512, 256, 128))
    tm_proj = _pick(mp, (256, 128))
    tm_mix = _pick(mp, (1024, 512, 256, 128))
    tm_ffn = _pick(seq, (1024, 512, 256, 128))
    tf = _pick(d_ff, (512, 256, 128))
    chunk = _pick(seq, (GLA_CHUNK,))
    pps = _pick(page_table.shape[1], (16, 8, 4, 2, 1))
    prompt_steps = batch * n_heads * (seq // tq)
    pps_fused = next((c for c in (16, 8, 4, 2, 1) if page_table.shape[1] % c == 0
                      and nb * (page_table.shape[1] // c) == prompt_steps), 0)
    t_pad = SUBLANES_BF16

    page_table = page_table.astype(jnp.int32)
    eye_2 = jnp.eye(2, dtype=BF16)

    yp = x_prompt.reshape(mp, d)
    ys = x_sample.reshape(ms, d)
    outs = [[] for _ in range(8)]
    for l in range(depth):
        lam0 = _lambda_init(l)
        lw = _layer_weights(l, params)
        toe, sb_far, sb_last, sb_new, lam = _bias_tables(rel_bias, lambda_q1[l], lambda_k1[l], lambda_q2[l],
                                                 lambda_k2[l], n_heads=n_heads, t_dec=t_dec, lam_init=lam0)

        (qkv, k32, v32, zf, log_a, vt), zb, zg = _token_stages(yp, lw, tm_proj=tm_proj, tm_mix=tm_mix,
                                                               transposed_v=True)
        (qkv_s, k32_s, v32_s, zf_s, log_a_s), zb_s, zg_s = _token_stages(ys, lw, tm_proj=ms, tm_mix=ms,
                                                                         transposed_v=False)
        q5 = qkv_s[:, :width].reshape(nb, t_dec, n_heads, 2, DH_A)
        q_rows = jnp.einsum('bthnd,mn->bmthnd', q5, eye_2).reshape(nb, 2 * t_dec * n_heads, 2 * DH_A)
        per_head = lambda a: a.reshape(nb, t_dec * n_heads, -1)
        kn, vn = per_head(qkv_s[:, width:2 * width]), per_head(qkv_s[:, 2 * width:])
        if pps_fused:
            oa, oa_s = _attn_fused(qkv, vt, toe, q_rows, kn, vn, cache_k, cache_v, page_table, l,
                                   sb_far, sb_last, sb_new, lam, lw['g_head_diff'], batch=batch, seq=seq,
                                   n_heads=n_heads, tq=tq, pps=pps_fused, lam_init=lam0)
        else:
            oa = _attn_prompt(qkv, vt, toe, lam, lw['g_head_diff'], batch=batch, seq=seq,
                              n_heads=n_heads, tq=tq, lam_init=lam0)
            oa_s = _attn_sample(q_rows, kn, vn, cache_k, cache_v, page_table, l, sb_far, sb_last, sb_new,
                                lam, lw['g_head_diff'], pps=pps, lam_init=lam0)

        ob, s_p = _gla_prompt(zf, log_a, zb, lw['g_head_gla'], batch=batch, seq=seq, chunk=chunk,
                              sub=GLA_SUB)
        x1, h2 = _merge(oa, ob.reshape(mp, -1), zg, yp, lw['wpd'], lw['wpg'], lw['wo'],
                        lw['g_post_mix'], lw['g_pre_ffn'], tm=tm_proj)
        yp, csg, csv = _ffn_prompt(h2, x1, lw['w_up'], lw['w_conv'], lw['b_conv'], lw['w_down'],
                                   lw['g_post_ffn'], seq=seq, tm=tm_ffn, tf=tf)
        cs = jnp.concatenate([csg, csv], axis=1).reshape(batch, seq // tm_ffn, 8, 2 * d_ff)
        outs[0].append(k32.reshape(batch, seq, n_heads, 2 * DH_A))
        outs[1].append(v32.reshape(batch, seq, n_heads, DV_A))
        outs[2].append(s_p)
        outs[3].append(cs[:, -1, 8 - (CONV_W - 1):])

        pad_t = lambda a: jnp.pad(a.reshape(nb, t_dec, -1), ((0, 0), (0, t_pad - t_dec), (0, 0)))
        ob, s_s = _gla_sample(pad_t(zf_s), pad_t(log_a_s), pad_t(zb_s), lw['g_head_gla'], state_gla[l])
        x1, h2 = _merge(oa_s.reshape(ms, -1), ob[:, :t_dec].reshape(ms, -1), zg_s, ys, lw['wpd'],
                        lw['wpg'], lw['wo'], lw['g_post_mix'], lw['g_pre_ffn'], tm=ms)
        tmajor = lambda a: a.reshape(nb, t_dec, -1).transpose(1, 0, 2).reshape(ms, -1)
        y_t, csg, csv = _ffn_sample(tmajor(h2), tmajor(x1), state_conv[l].transpose(1, 0, 2),
                                    lw['w_up'], lw['w_conv'], lw['b_conv'], lw['w_down'],
                                    lw['g_post_ffn'], tf=tf)
        ys = y_t.reshape(t_dec, nb, d).transpose(1, 0, 2).reshape(ms, d)
        outs[4].append(k32_s.reshape(nb, t_dec, n_heads, 2 * DH_A))
        outs[5].append(v32_s.reshape(nb, t_dec, n_heads, DV_A))
        outs[6].append(s_s)
        outs[7].append(jnp.concatenate([csg, csv], axis=2).transpose(1, 0, 2))

    return (yp.reshape(batch, seq, d), ys.reshape(nb, t_dec, d)) + tuple(jnp.stack(o) for o in outs)
```

```python
import functools
import math

import numpy as np
import jax
import jax.numpy as jnp
from jax import lax
from jax.experimental import pallas as pl
from jax.experimental.pallas import tpu as pltpu

F32 = jnp.float32
BF16 = jnp.bfloat16

DH_A = 64
DV_A = 128
DK_B = 128
DV_B = 256
H_B = 4
GATE_RANK = 16
GATE_TAU = 16.0
NUM_BUCKETS = 32
MAX_DISTANCE = 128
CONV_W = 3
PAGE_SIZE = 128
EPS = 1e-6
NEG_INF = -1e30
LOG2_E = math.log2(math.e)

LANES = 128
SUBLANES_BF16 = 16
VT_ROWS = DV_A + SUBLANES_BF16
VMEM_LIMIT_BYTES = 56 * 1024 * 1024

SAMPLE_BLOCK_COLS = 1024
GLA_CHUNK = 64
GLA_SUB = 16


def _lambda_init(layer):
    return 0.8 - 0.6 * math.exp(-0.3 * layer)


def _cparams(*sem):
    return pltpu.CompilerParams(dimension_semantics=sem, vmem_limit_bytes=VMEM_LIMIT_BYTES)


def _const_spec(shape):
    nd = len(shape)
    return pl.BlockSpec(shape, lambda *_: (0,) * nd, pipeline_mode=pl.Buffered(1))


def _rms(xf, g):
    return xf * lax.rsqrt(jnp.mean(xf * xf, axis=-1, keepdims=True) + EPS) * g


def _dot(a, b):
    return jnp.dot(a, b, preferred_element_type=F32)


def _dot_nt(a, b):
    return lax.dot_general(a, b, (((1,), (1,)), ((), ())), preferred_element_type=F32)


def _dot_tn(a, b):
    return lax.dot_general(a, b, (((0,), (0,)), ((), ())), preferred_element_type=F32)


def _split_bf16(x):
    hi = x.astype(BF16)
    lo = (x - hi.astype(F32)).astype(BF16)
    return hi, lo


def _t5_bucket_np(rel):
    n = np.maximum(rel, 0)
    max_exact = NUM_BUCKETS // 2
    nf = np.maximum(n, 1).astype(np.float32)
    large = max_exact + (np.log(nf / max_exact) / math.log(MAX_DISTANCE / max_exact)
                         * (NUM_BUCKETS - max_exact)).astype(np.int32)
    large = np.minimum(large, NUM_BUCKETS - 1)
    return np.where(n < max_exact, n, large).astype(np.int32)


def _bias_tables_kernel(relb_ref, bk_toe_ref, bk_far_ref, bk_last_ref, bk_new_ref,
                        lq1_ref, lk1_ref, lq2_ref, lk2_ref,
                        toe_ref, sb_far_ref, sb_last_ref, sb_new_ref, lam_ref, *, n_heads, lam_init):
    far = NUM_BUCKETS - 1

    def lookup(bk, h):
        out = jnp.full(bk.shape, NEG_INF, F32)
        for b in range(NUM_BUCKETS):
            out = jnp.where(bk == b, relb_ref[b, h] - relb_ref[far, h], out)
        return out

    for h in range(n_heads):
        toe_ref[0, h] = lookup(bk_toe_ref[0], h)
        toe_ref[1, h] = lookup(bk_toe_ref[1], h)

    def per_row_head(bk_ref, out_ref):
        row_h = lax.broadcasted_iota(jnp.int32, bk_ref.shape, 0) % n_heads
        out = jnp.zeros(bk_ref.shape, F32)
        for h in range(n_heads):
            out = jnp.where(row_h == h, lookup(bk_ref[...], h), out)
        out_ref[...] = out * -LOG2_E

    per_row_head(bk_far_ref, sb_far_ref)
    per_row_head(bk_last_ref, sb_last_ref)
    per_row_head(bk_new_ref, sb_new_ref)
    d1 = jnp.sum(lq1_ref[...] * lk1_ref[...], axis=-1, keepdims=True)
    d2 = jnp.sum(lq2_ref[...] * lk2_ref[...], axis=-1, keepdims=True)
    lam = jnp.exp(d1) - jnp.exp(d2) + lam_init
    lam_ref[...] = jnp.broadcast_to(lam, lam_ref.shape)


def _bias_tables(rel_bias, lq1, lk1, lq2, lk2, *, n_heads, t_dec, lam_init):
    i = np.arange(LANES)[:, None]
    j = np.arange(LANES)[None, :]
    bk_toe = np.stack([np.where(j >= i, _t5_bucket_np(j - i), -1),
                       _t5_bucket_np(LANES + j - i)]).astype(np.int32)
    rows = 2 * t_dec * n_heads
    r = np.arange(rows)[:, None]
    t, h = (r // n_heads) % t_dec, r % n_heads
    c = np.arange(PAGE_SIZE * n_heads)[None, :]
    key, hk = c // n_heads, c % n_heads
    bk_far = np.where(hk == h, NUM_BUCKETS - 1, -1).astype(np.int32)
    bk_last = np.where(hk == h, _t5_bucket_np(PAGE_SIZE + t - key), -1).astype(np.int32)
    cn = c[:, :LANES]
    key, hk = cn // n_heads, cn % n_heads
    bk_new = np.where((hk == h) & (key <= t) & (key < t_dec), _t5_bucket_np(t - key), -1).astype(np.int32)
    vec = lambda a: a.reshape(1, -1).astype(F32)
    smem = pl.BlockSpec(memory_space=pltpu.SMEM)
    tab = lambda a: jax.ShapeDtypeStruct(a.shape, F32)
    return pl.pallas_call(
        functools.partial(_bias_tables_kernel, n_heads=n_heads, lam_init=lam_init),
        out_shape=(jax.ShapeDtypeStruct((2, n_heads, LANES, LANES), F32),
                   tab(bk_far), tab(bk_last), tab(bk_new),
                   jax.ShapeDtypeStruct((8, LANES), F32)),
        in_specs=[smem] + [pl.BlockSpec(memory_space=pltpu.VMEM)] * 8,
        name="bias_tables",
    )(rel_bias.astype(F32), jnp.asarray(bk_toe), jnp.asarray(bk_far), jnp.asarray(bk_last),
      jnp.asarray(bk_new), vec(lq1), vec(lk1), vec(lq2), vec(lk2))


def _attn_proj_kernel(x_ref, g_ref, w_ref, wlr_ref, wgh_ref, wgl_ref, bg_ref,
                      h_ref, qkv_ref, k32_ref, v32_ref, zf_ref, loga_ref, *maybe_vt_ref, n_attn, n_qk):
    h = _rms(x_ref[...], g_ref[...]).astype(BF16)
    h_ref[...] = h
    q = _dot_nt(h, w_ref[0:n_attn]) * DH_A ** -0.5
    qkv_ref[:, 0:n_attn] = q.astype(BF16)
    k = _dot_nt(h, w_ref[n_attn:2 * n_attn])
    k32_ref[...] = k
    qkv_ref[:, n_attn:2 * n_attn] = k.astype(BF16)
    v = _dot_nt(h, w_ref[2 * n_attn:3 * n_attn])
    v32_ref[...] = v
    qkv_ref[:, 2 * n_attn:3 * n_attn] = v.astype(BF16)
    for vt_ref in maybe_vt_ref:
        vt = v.T.astype(BF16)
        ones = jnp.ones((VT_ROWS - DV_A, vt.shape[1]), BF16)
        vt_ref[...] = jnp.concatenate(
            [part for h in range(n_attn // DV_A) for part in (vt[h * DV_A:(h + 1) * DV_A], ones)], axis=0)
    zf_ref[:, 0:n_qk] = _dot_nt(h, w_ref[3 * n_attn:3 * n_attn + n_qk]) * DK_B ** -0.5
    zf_ref[:, n_qk:2 * n_qk] = _dot_nt(h, w_ref[3 * n_attn + n_qk:3 * n_attn + 2 * n_qk])
    lr = _dot_nt(h, wlr_ref[...])
    lr_hi, lr_lo = _split_bf16(lr)
    pre = (_dot(lr_hi, wgh_ref[...]) + _dot(lr_hi, wgl_ref[...]) + _dot(lr_lo, wgh_ref[...])
           + bg_ref[...])
    log_sig = jnp.minimum(pre, 0.0) - jnp.log1p(jnp.exp(-jnp.abs(pre)))
    loga_ref[...] = log_sig / GATE_TAU


def _attn_proj(x, g, w_t, o_lr, wgh, wgl, bg, *, n_attn, tm, transposed_v):
    m, d = x.shape
    n_qk = H_B * DK_B
    n_lead = 3 * n_attn + 2 * n_qk
    assert o_lr % LANES == 0
    row = lambda n: pl.BlockSpec((tm, n), lambda i: (i, 0))
    out_specs = [row(d), row(3 * n_attn), row(n_attn), row(n_attn), row(2 * n_qk), row(n_qk)]
    out_shape = [jax.ShapeDtypeStruct((m, d), BF16),
                 jax.ShapeDtypeStruct((m, 3 * n_attn), BF16),
                 jax.ShapeDtypeStruct((m, n_attn), F32),
                 jax.ShapeDtypeStruct((m, n_attn), F32),
                 jax.ShapeDtypeStruct((m, 2 * n_qk), F32),
                 jax.ShapeDtypeStruct((m, n_qk), F32)]
    if transposed_v:
        vt_rows = n_attn // DV_A * VT_ROWS
        out_specs.append(pl.BlockSpec((vt_rows, tm), lambda i: (0, i)))
        out_shape.append(jax.ShapeDtypeStruct((vt_rows, m), BF16))
    once = pl.Buffered(1)
    return pl.pallas_call(
        functools.partial(_attn_proj_kernel, n_attn=n_attn, n_qk=n_qk),
        grid=(m // tm,),
        in_specs=[row(d), _const_spec(g.shape),
                  pl.BlockSpec((n_lead, d), lambda i: (0, 0), pipeline_mode=once),
                  pl.BlockSpec((LANES, d), lambda i: (o_lr // LANES, 0), pipeline_mode=once),
                  _const_spec(wgh.shape), _const_spec(wgl.shape), _const_spec(bg.shape)],
        out_specs=tuple(out_specs),
        out_shape=tuple(out_shape),
        compiler_params=_cparams("parallel"),
        name="attn_proj",
    )(x, g, w_t, w_t, wgh, wgl, bg)


def _sigmoid(x):
    return 0.5 * jnp.tanh(0.5 * x) + 0.5


_ACTS = {
    'id': lambda x: x,
    'silu': lambda x: x * _sigmoid(x),
    'sigmoid': _sigmoid,
}


def _act_proj_kernel(h_ref, w_ref, o_ref, *, acts, cw):
    h = h_ref[...]
    accs = [_dot_nt(h, w_ref[c * cw:(c + 1) * cw]) for c in range(len(acts))]
    for c, (act, acc) in enumerate(zip(acts, accs)):
        o_ref[:, c * cw:(c + 1) * cw] = _ACTS[act](acc).astype(BF16)


def _act_proj(h, w_t, row0, n, acts, *, tm, tn, name):
    m, d = h.shape
    cw = tn // len(acts)
    assert row0 % tn == 0 and n % tn == 0 and cw % LANES == 0
    j0 = row0 // tn
    nj = n // tn
    w_mode = dict(pipeline_mode=pl.Buffered(1)) if nj == 1 else {}
    return pl.pallas_call(
        functools.partial(_act_proj_kernel, acts=acts, cw=cw),
        grid=(m // tm, nj),
        in_specs=[pl.BlockSpec((tm, d), lambda i, j: (i, 0)),
                  pl.BlockSpec((tn, d), lambda i, j: (j0 + j, 0), **w_mode)],
        out_specs=pl.BlockSpec((tm, tn), lambda i, j: (i, j)),
        out_shape=jax.ShapeDtypeStruct((m, n), BF16),
        compiler_params=_cparams("parallel", "arbitrary"),
        name=name,
    )(h, w_t)


def _softmax_step(s, v, state):
    m_prev, l_prev, acc_prev = state
    m_new = jnp.maximum(m_prev, jnp.max(s, axis=-1, keepdims=True))
    alpha = jnp.exp2(m_prev - m_new)
    p = jnp.exp2(s - m_new)
    return (m_new, alpha * l_prev + jnp.sum(p, axis=-1, keepdims=True),
            alpha * acc_prev + _dot(p.astype(BF16), v))


def _head_norm(o, g, lam_init):
    return o * lax.rsqrt(jnp.mean(o * o, axis=-1, keepdims=True) + EPS) * g * (1.0 - lam_init)


def _attn_prompt_kernel(lam_ref, q_ref, k_ref, vt_ref, toe_ref, g_ref, o_ref,
                        bias_scr, *scr, tq, cb, ahead, lam_init):
    qi = pl.program_id(2)
    nb = tq // LANES

    @pl.when(qi == 0)
    def _():
        for r in range(nb):
            for c in range(nb):
                blk = (slice(r * LANES, (r + 1) * LANES), slice(c * LANES, (c + 1) * LANES))
                for t, d in ((0, tq + (c - r) * LANES), (1, (c - r) * LANES)):
                    if d < 0:
                        bias_scr[(t,) + blk] = jnp.full((LANES, LANES), -NEG_INF, F32)
                    elif d == 0:
                        bias_scr[(t,) + blk] = toe_ref[0] * -LOG2_E
                    elif d == LANES:
                        bias_scr[(t,) + blk] = toe_ref[1] * -LOG2_E
                    else:
                        bias_scr[(t,) + blk] = jnp.zeros((LANES, LANES), F32)

    n_cb = 2 * tq // cb
    chains = [scr[3 * c:3 * c + 3] for c in range(n_cb)]
    q = (q_ref[...].astype(F32) * LOG2_E).astype(BF16)
    lane = lax.broadcasted_iota(jnp.int32, (cb, q.shape[1]), 1)
    for c, (q2_scr, m_scr, acc_scr) in enumerate(chains):
        q_blk = q[(c * cb) % tq:(c * cb) % tq + cb]
        keep = (lane < DH_A) if c * cb < tq else (lane >= DH_A)
        q2_scr[...] = jnp.where(keep, q_blk, jnp.zeros_like(q_blk))
        m_scr[...] = jnp.full(m_scr.shape, NEG_INF, F32)
        acc_scr[...] = jnp.zeros(acc_scr.shape, F32)

    def kv_tiles(js, bias_idx):
        offs = [pl.multiple_of(j * tq, tq) for j in js]
        ks = [k_ref[pl.ds(off, tq), :] for off in offs]
        vts = [vt_ref[:, pl.ds(off, tq)] for off in offs]
        bias = None if bias_idx is None else [
            [bias_scr[bi, :, b0:b0 + cb] for b0 in range(0, tq, cb)] for bi in bias_idx]
        blocks = [(t, c) for t in range(len(js)) for c in range(n_cb)]
        score = lambda t, c: _dot_nt(ks[t], chains[c][0][...])
        pending = [score(t, c) for t, c in blocks[:ahead]]
        for n, (t, c) in enumerate(blocks):
            _, m_scr, acc_scr = chains[c]
            s = pending.pop(0)
            if n + ahead < len(blocks):
                pending.append(score(*blocks[n + ahead]))
            if bias is not None:
                s = s - bias[t][c % len(bias[t])]
            m_prev = m_scr[...]
            m_new = jnp.maximum(m_prev, jnp.max(s, axis=0, keepdims=True))
            alpha = jnp.exp2(m_prev - m_new)
            p = jnp.exp2((s - m_new).astype(BF16))
            acc_scr[...] = alpha * acc_scr[...] + _dot(vts[t], p)
            m_scr[...] = m_new

    def far_pair(i, carry):
        kv_tiles([2 * i, 2 * i + 1], None)
        return carry

    def far_tile(j, carry):
        kv_tiles([j], None)
        return carry

    n_far = jnp.maximum(qi - 1, 0)
    lax.fori_loop(0, n_far // 2, far_pair, 0)
    lax.fori_loop(n_far // 2 * 2, n_far, far_tile, 0)

    @pl.when(qi == 0)
    def _():
        kv_tiles([0], [1])

    @pl.when(qi > 0)
    def _():
        kv_tiles([qi - 1, qi], [0, 1])

    on = jnp.concatenate([acc_scr[0:DV_A] / acc_scr[DV_A:DV_A + 1] for _, _, acc_scr in chains], axis=1)
    lam = lam_ref[0:1, 0:1]
    o_t = on[:, 0:tq] - lam * on[:, tq:2 * tq]
    o_t = o_t * lax.rsqrt(jnp.mean(o_t * o_t, axis=0, keepdims=True) + EPS)
    o_ref[...] = (o_t.T * (g_ref[...] * (1.0 - lam_init))).astype(BF16)


def _attn_prompt(qkv, vt, toe, lam, g_head, *, batch, seq, n_heads, tq, lam_init):
    nq = seq // tq
    koff = n_heads
    cb = min(tq, 2 * LANES)
    return pl.pallas_call(
        functools.partial(_attn_prompt_kernel, tq=tq, cb=cb, ahead=4, lam_init=lam_init),
        grid=(batch, n_heads, nq),
        in_specs=[pl.BlockSpec(lam.shape, lambda b, h, i: (0, 0)),
                  pl.BlockSpec((tq, LANES), lambda b, h, i: (b * nq + i, h)),
                  pl.BlockSpec((seq, LANES), lambda b, h, i: (b, koff + h)),
                  pl.BlockSpec((VT_ROWS, seq), lambda b, h, i: (h, b)),
                  pl.BlockSpec((2, None, LANES, LANES), lambda b, h, i: (0, h, 0, 0)),
                  pl.BlockSpec(g_head.shape, lambda b, h, i: (0, 0))],
        out_specs=pl.BlockSpec((tq, LANES), lambda b, h, i: (b * nq + i, h)),
        out_shape=jax.ShapeDtypeStruct((batch * seq, n_heads * LANES), BF16),
        scratch_shapes=[pltpu.VMEM((2, tq, tq), F32)]
        + [pltpu.VMEM((cb, LANES), BF16), pltpu.VMEM((1, cb), F32),
           pltpu.VMEM((VT_ROWS, cb), F32)] * (2 * tq // cb),
        compiler_params=_cparams("parallel", "parallel", "arbitrary"),
        name="attn_prompt",
    )(lam, qkv, qkv, vt, toe, g_head)


def _attn_sample_kernel(pt_ref, lam_ref, q_ref, kn_ref, vn_ref, sbf_ref, sbl_ref, sbn_ref, g_ref, *rest,
                        pps, bc, lam_init, g=None, last=None):
    k_refs = rest[:pps]
    v_refs = rest[pps:2 * pps]
    o_ref = rest[2 * pps]
    m_scr, l_scr, acc_scr = rest[2 * pps + 1:]
    if g is None:
        g = pl.program_id(1)
        last = pl.num_programs(1) - 1

    @pl.when(g == 0)
    def _():
        m_scr[...] = jnp.full(m_scr.shape, NEG_INF, F32)
        l_scr[...] = jnp.zeros(l_scr.shape, F32)
        acc_scr[...] = jnp.zeros(acc_scr.shape, F32)

    q = q_ref[...]
    cols = sbf_ref.shape[1]
    blocks = [(p, c0) for p in range(pps) for c0 in range(0, cols, bc)]
    rows_of = lambda refs, blk: (
        refs[blk[0]][...].reshape(cols, refs[blk[0]].shape[-1])[blk[1]:blk[1] + bc].astype(BF16))
    ahead = 3
    pending = [_dot_nt(q, rows_of(k_refs, blk)) for blk in blocks[:ahead]]
    far_tile = sbf_ref[0:8, 0:LANES]
    far = jnp.tile(far_tile, (q.shape[0] // 8, bc // LANES))
    state = (m_scr[...], l_scr[...], acc_scr[...])
    for n, blk in enumerate(blocks):
        s = pending.pop(0)
        if n + ahead < len(blocks):
            pending.append(_dot_nt(q, rows_of(k_refs, blocks[n + ahead])))
        if blk[0] == pps - 1:
            s = s - jnp.where(g == last, sbl_ref[:, blk[1]:blk[1] + bc], far)
        else:
            s = s - far
        state = _softmax_step(s, rows_of(v_refs, blk), state)
    m_scr[...], l_scr[...], acc_scr[...] = state

    @pl.when(g == last)
    def _():
        pad = jnp.zeros((sbn_ref.shape[1] - kn_ref.shape[0], kn_ref.shape[1]), BF16)
        kn = jnp.concatenate([kn_ref[...], pad], axis=0)
        vn = jnp.concatenate([vn_ref[...], pad], axis=0)
        _, l_fin, acc_fin = _softmax_step(_dot_nt(q, kn) - sbn_ref[...], vn, state)
        on = acc_fin / l_fin
        half = on.shape[0] // 2
        o = on[0:half] - lam_ref[0:1, 0:1] * on[half:2 * half]
        o_ref[...] = _head_norm(o, g_ref[...], lam_init).astype(BF16)


def _attn_sample(q, kn, vn, cache_k, cache_v, page_table, layer, sb_far, sb_last, sb_new, lam, g_head,
                 *, pps, lam_init):
    nb, rows, feat = q.shape
    n_pages = page_table.shape[1]
    steps = n_pages // pps
    seq_spec = lambda a: pl.BlockSpec((None,) + a.shape[1:], lambda b, g, pt: (b, 0, 0))
    full = lambda a: pl.BlockSpec(a.shape, lambda b, g, pt: (0, 0))

    def page_spec(jj):
        return pl.BlockSpec((None, None) + cache_k.shape[2:],
                            lambda b, g, pt: (layer, pt[b, g * pps + jj], 0, 0, 0))

    grid_spec = pltpu.PrefetchScalarGridSpec(
        num_scalar_prefetch=1,
        grid=(nb, steps),
        in_specs=[full(lam), seq_spec(q), seq_spec(kn), seq_spec(vn), full(sb_far), full(sb_last),
                  full(sb_new), full(g_head)] + [page_spec(jj) for jj in range(pps)] * 2,
        out_specs=pl.BlockSpec((None, rows // 2, feat), lambda b, g, pt: (b, 0, 0)),
        scratch_shapes=[pltpu.VMEM((rows, 1), F32), pltpu.VMEM((rows, 1), F32),
                        pltpu.VMEM((rows, feat), F32)],
    )
    return pl.pallas_call(
        functools.partial(_attn_sample_kernel, pps=pps, bc=SAMPLE_BLOCK_COLS, lam_init=lam_init),
        grid_spec=grid_spec,
        out_shape=jax.ShapeDtypeStruct((nb, rows // 2, feat), BF16),
        compiler_params=_cparams("parallel", "arbitrary"),
        name="attn_sample",
    )(page_table, lam, q, kn, vn, sb_far, sb_last, sb_new, g_head, *([cache_k] * pps), *([cache_v] * pps))


def _attn_fused_kernel(pt_ref, lam_ref, q_ref, k_ref, vt_ref, toe_ref, g_ref,
                       qs_ref, kn_ref, vn_ref, sbf_ref, sbl_ref, sbn_ref, *rest,
                       pps, bc, steps_s, n_prompt_scr, tq, cb, ahead, lam_init):
    pages = rest[:2 * pps]
    o_ref, os_ref = rest[2 * pps:2 * pps + 2]
    scr = rest[2 * pps + 2:]
    step = (pl.program_id(0) * pl.num_programs(1) + pl.program_id(1)) * pl.num_programs(2) + pl.program_id(2)
    _attn_sample_kernel(pt_ref, lam_ref, qs_ref, kn_ref, vn_ref, sbf_ref, sbl_ref, sbn_ref, g_ref,
                        *pages, os_ref, *scr[n_prompt_scr:], pps=pps, bc=bc, lam_init=lam_init,
                        g=step % steps_s, last=steps_s - 1)
    _attn_prompt_kernel(lam_ref, q_ref, k_ref, vt_ref, toe_ref, g_ref, o_ref, *scr[:n_prompt_scr],
                        tq=tq, cb=cb, ahead=ahead, lam_init=lam_init)


def _attn_fused(qkv, vt, toe, qs, kn, vn, cache_k, cache_v, page_table, layer, sb_far, sb_last, sb_new,
                lam, g_head, *, batch, seq, n_heads, tq, pps, lam_init):
    nq = seq // tq
    koff = n_heads
    cb = min(tq, 2 * LANES)
    nb, rows, feat = qs.shape
    steps_s = page_table.shape[1] // pps
    assert batch * n_heads * nq == nb * steps_s
    flat = lambda b, h, i: (b * n_heads + h) * nq + i
    full = lambda a: pl.BlockSpec(a.shape, lambda b, h, i, pt: (0,) * a.ndim)
    seq_spec = lambda a: pl.BlockSpec((None,) + a.shape[1:], lambda b, h, i, pt: (flat(b, h, i) // steps_s, 0, 0))

    def page_spec(jj):
        def index(b, h, i, pt):
            s = flat(b, h, i)
            return (layer, pt[s // steps_s, (s % steps_s) * pps + jj], 0, 0, 0)
        return pl.BlockSpec((None, None) + cache_k.shape[2:], index)

    prompt_scr = ([pltpu.VMEM((2, tq, tq), F32)]
                  + [pltpu.VMEM((cb, LANES), BF16), pltpu.VMEM((1, cb), F32),
                     pltpu.VMEM((VT_ROWS, cb), F32)] * (2 * tq // cb))
    sample_scr = [pltpu.VMEM((rows, 1), F32), pltpu.VMEM((rows, 1), F32), pltpu.VMEM((rows, feat), F32)]
    grid_spec = pltpu.PrefetchScalarGridSpec(
        num_scalar_prefetch=1,
        grid=(batch, n_heads, nq),
        in_specs=[full(lam),
                  pl.BlockSpec((tq, LANES), lambda b, h, i, pt: (b * nq + i, h)),
                  pl.BlockSpec((seq, LANES), lambda b, h, i, pt: (b, koff + h)),
                  pl.BlockSpec((VT_ROWS, seq), lambda b, h, i, pt: (h, b)),
                  pl.BlockSpec((2, None, LANES, LANES), lambda b, h, i, pt: (0, h, 0, 0)),
                  full(g_head), seq_spec(qs), seq_spec(kn), seq_spec(vn), full(sb_far), full(sb_last),
                  full(sb_new)] + [page_spec(jj) for jj in range(pps)] * 2,
        out_specs=(pl.BlockSpec((tq, LANES), lambda b, h, i, pt: (b * nq + i, h)),
                   pl.BlockSpec((None, rows // 2, feat), lambda b, h, i, pt: (flat(b, h, i) // steps_s, 0, 0))),
        scratch_shapes=prompt_scr + sample_scr,
    )
    return pl.pallas_call(
        functools.partial(_attn_fused_kernel, pps=pps, bc=SAMPLE_BLOCK_COLS, steps_s=steps_s,
                          n_prompt_scr=len(prompt_scr), tq=tq, cb=cb, ahead=4, lam_init=lam_init),
        grid_spec=grid_spec,
        out_shape=(jax.ShapeDtypeStruct((batch * seq, n_heads * LANES), BF16),
                   jax.ShapeDtypeStruct((nb, rows // 2, feat), BF16)),
        compiler_params=_cparams("arbitrary", "arbitrary", "arbitrary"),
        name="attn_fused",
    )(page_table, lam, qkv, qkv, vt, toe, g_head, qs, kn, vn, sb_far, sb_last, sb_new,
      *([cache_k] * pps), *([cache_v] * pps))


def _gla_chunk(heads, *, sub):
    c_len = heads[0][0].shape[0]
    ri = lax.broadcasted_iota(jnp.int32, (c_len, c_len), 0)
    ci = lax.broadcasted_iota(jnp.int32, (c_len, c_len), 1)
    row = lax.broadcasted_iota(jnp.int32, (c_len, 1), 0)
    tril = (ri >= ci).astype(BF16)
    bs = []
    for _, _, log_a, _, _ in heads:
        a_hi, a_lo = _split_bf16(log_a)
        bs.append(_dot(tril, a_hi) + _dot(tril, a_lo))
    outs, states = [], []
    for (q, k, _, v, st), b in zip(heads, bs):
        b_last = b[c_len - 1:c_len]
        outs.append(_dot_nt((q * jnp.exp(b)).astype(BF16), st.astype(BF16)))
        k_out = (k * jnp.exp(b_last - b)).astype(BF16)
        states.append(st * jnp.exp(b_last) + _dot_tn(v, k_out))
    atts = []
    for (q, k, _, _, _), b in zip(heads, bs):
        rows = []
        for i in range(c_len // sub):
            lo, hi = i * sub, (i + 1) * sub
            b_ref = b[lo:lo + 1]
            qh = (q[lo:hi] * jnp.exp(b[lo:hi] - b_ref)).astype(BF16)
            kh = (k * jnp.exp(jnp.where(row < hi, b_ref - b, NEG_INF))).astype(BF16)
            rows.append(_dot_nt(qh, kh))
        att = jnp.concatenate(rows, axis=0) if len(rows) > 1 else rows[0]
        atts.append(jnp.where(ri >= ci, att, 0.0).astype(BF16))
    outs = [o + _dot(att, v) for o, att, (_, _, _, v, _) in zip(outs, atts, heads)]
    return list(zip(outs, states))


def _gla_out(o, g, r):
    return (o * lax.rsqrt(jnp.mean(o * o, axis=-1, keepdims=True) + EPS) * g * r.astype(F32)).astype(BF16)


def _gla_prompt_kernel(qk_ref, la_ref, v_ref, r_ref, g_ref, o_ref, s_ref, st_scr, *, batch, sub):
    ci = pl.program_id(0)

    @pl.when(ci == 0)
    def _():
        st_scr[...] = jnp.zeros(st_scr.shape, F32)

    nk = H_B * DK_B
    ids = [(b, h) for b in range(batch) for h in range(H_B)]
    ks = lambda h: slice(h * DK_B, (h + 1) * DK_B)
    vs = lambda h: slice(h * DV_B, (h + 1) * DV_B)
    heads = [(qk_ref[b, :, ks(h)], qk_ref[b, :, nk + h * DK_B:nk + (h + 1) * DK_B], la_ref[b, :, ks(h)],
              v_ref[b, :, vs(h)], st_scr[b, h]) for b, h in ids]
    for (b, h), (o, st_new) in zip(ids, _gla_chunk(heads, sub=sub)):
        st_scr[b, h] = st_new
        o_ref[b, :, vs(h)] = _gla_out(o, g_ref[...], r_ref[b, :, vs(h)])

    @pl.when(ci == pl.num_programs(0) - 1)
    def _():
        for b in range(batch):
            for h in range(H_B):
                s_ref[b, h] = st_scr[b, h].T


def _gla_prompt(zf, log_a, zb, g_head, *, batch, seq, chunk, sub):
    nqk = zf.shape[1]
    nv = H_B * DV_B
    zf3 = zf.reshape(batch, seq, nqk)
    la3 = log_a.reshape(batch, seq, log_a.shape[1])
    zb3 = zb.reshape(batch, seq, zb.shape[1])
    blk = lambda n, col: pl.BlockSpec((batch, chunk, n), lambda c: (0, c, col))
    return pl.pallas_call(
        functools.partial(_gla_prompt_kernel, batch=batch, sub=sub),
        grid=(seq // chunk,),
        in_specs=[blk(nqk, 0), blk(la3.shape[2], 0), blk(nv, 0), blk(nv, 1),
                  pl.BlockSpec(g_head.shape, lambda c: (0, 0))],
        out_specs=(blk(nv, 0),
                   pl.BlockSpec((batch, H_B, DK_B, DV_B), lambda c: (0, 0, 0, 0))),
        out_shape=(jax.ShapeDtypeStruct((batch, seq, nv), BF16),
                   jax.ShapeDtypeStruct((batch, H_B, DK_B, DV_B), F32)),
        scratch_shapes=[pltpu.VMEM((batch, H_B, DV_B, DK_B), F32)],
        compiler_params=_cparams("arbitrary"),
        name="gla_prompt",
    )(zf3, la3, zb3, zb3, g_head)


def _gla_sample_kernel(qk_ref, la_ref, v_ref, r_ref, g_ref, s0_ref, o_ref, s_ref, *, sub):
    nk = H_B * DK_B
    ks = lambda h: slice(h * DK_B, (h + 1) * DK_B)
    vs = lambda h: slice(h * DV_B, (h + 1) * DV_B)
    heads = [(qk_ref[:, ks(h)], qk_ref[:, nk + h * DK_B:nk + (h + 1) * DK_B], la_ref[:, ks(h)],
              v_ref[:, vs(h)], s0_ref[h].T) for h in range(H_B)]
    for h, (o, st_new) in enumerate(_gla_chunk(heads, sub=sub)):
        s_ref[h] = st_new.T
        o_ref[:, vs(h)] = _gla_out(o, g_ref[...], r_ref[:, vs(h)])


def _gla_sample(zf3, la3, zb3, g_head, s0):
    nb, tp, nqk = zf3.shape
    nv = H_B * DV_B
    blk = lambda n, col: pl.BlockSpec((None, tp, n), lambda b: (b, 0, col))
    st = pl.BlockSpec((None, H_B, DK_B, DV_B), lambda b: (b, 0, 0, 0))
    return pl.pallas_call(
        functools.partial(_gla_sample_kernel, sub=tp),
        grid=(nb,),
        in_specs=[blk(nqk, 0), blk(la3.shape[2], 0), blk(nv, 0), blk(nv, 1),
                  pl.BlockSpec(g_head.shape, lambda b: (0, 0)), st],
        out_specs=(blk(nv, 0), st),
        out_shape=(jax.ShapeDtypeStruct((nb, tp, nv), BF16),
                   jax.ShapeDtypeStruct(s0.shape, F32)),
        compiler_params=_cparams("parallel"),
        name="gla_sample",
    )(zf3, la3, zb3, zb3, g_head, s0)


def _merge_kernel(oa_ref, ob_ref, sga_ref, sgb_ref, x_ref, wpd_ref, wpg_ref, wo_ref, g1_ref, g2_ref,
                  x1_ref, h2_ref):
    m = (sga_ref[...].astype(F32) * _dot(oa_ref[...], wpd_ref[...])
         + sgb_ref[...].astype(F32) * _dot(ob_ref[...], wpg_ref[...]))
    mix = _dot(m.astype(BF16), wo_ref[...])
    x1 = x_ref[...] + _rms(mix, g1_ref[...])
    x1_ref[...] = x1
    h2_ref[...] = _rms(x1, g2_ref[...]).astype(BF16)


def _merge(oa, ob, zg, x, wpd, wpg, wo, g1, g2, *, tm):
    m, d = x.shape
    na, nb_ = oa.shape[1], ob.shape[1]
    row = lambda n, col=0: pl.BlockSpec((tm, n), lambda i: (i, col))
    return pl.pallas_call(
        _merge_kernel,
        grid=(m // tm,),
        in_specs=[row(na), row(nb_), row(d, 0), row(d, 1), row(d),
                  _const_spec(wpd.shape), _const_spec(wpg.shape), _const_spec(wo.shape),
                  _const_spec(g1.shape), _const_spec(g2.shape)],
        out_specs=(row(d), row(d)),
        out_shape=(jax.ShapeDtypeStruct((m, d), F32), jax.ShapeDtypeStruct((m, d), BF16)),
        compiler_params=_cparams("parallel"),
        name="merge",
    )(oa, ob, zg, zg, x, wpd, wpg, wo, g1, g2)


def _geglu(cg, cv):
    return (jax.nn.gelu(cg, approximate=True) * cv).astype(BF16)


def _ffn_finish(acc, x1_ref, g_ref, y_ref):
    y_ref[...] = x1_ref[...] + _rms(acc, g_ref[...])


def _ffn_prompt_kernel(hp_ref, h_ref, x1_ref, wug_ref, wuv_ref, wcg_ref, wcv_ref, bcg_ref, bcv_ref,
                       wd_ref, g_ref, y_ref, csg_ref, csv_ref, *, tm, rb, halo, cw, seq_tiles):
    i = pl.program_id(0)
    j = pl.program_id(1)

    @pl.when(j == 0)
    def _():
        y_ref[...] = jnp.zeros(y_ref.shape, F32)

    hp = jnp.where(i % seq_tiles == 0, jnp.zeros_like(hp_ref[...]), hp_ref[...])
    n_rb = tm // rb
    lhs = [jnp.concatenate([hp, h_ref[0:rb]], axis=0)] + [h_ref[r * rb:(r + 1) * rb] for r in range(1, n_rb)]
    chunks = [slice(c0, c0 + cw) for c0 in range(0, wug_ref.shape[1], cw)]

    ups = [[(_dot(x, wug_ref[:, cols]), _dot(x, wuv_ref[:, cols])) for cols in chunks] for x in lhs]

    def conv(u, cols, wc_ref, bc_ref):
        return (bc_ref[:, cols] + wc_ref[0:1, cols] * u[halo - 2:halo - 2 + rb]
                + wc_ref[1:2, cols] * u[halo - 1:halo - 1 + rb] + wc_ref[2:3, cols] * u[halo:halo + rb])

    for r in range(n_rb):
        acts = []
        for c, cols in enumerate(chunks):
            ug, uv = ups[r][c]
            if r > 0:
                top = rb if r > 1 else rb + halo
                ug = jnp.concatenate([ups[r - 1][c][0][top - halo:top], ug], axis=0)
                uv = jnp.concatenate([ups[r - 1][c][1][top - halo:top], uv], axis=0)
            if r == n_rb - 1:
                csg_ref[:, cols] = ug[halo + rb - 8:halo + rb]
                csv_ref[:, cols] = uv[halo + rb - 8:halo + rb]
            acts.append(_geglu(conv(ug, cols, wcg_ref, bcg_ref), conv(uv, cols, wcv_ref, bcv_ref)))
        a = jnp.concatenate(acts, axis=1) if len(acts) > 1 else acts[0]
        y_ref[r * rb:(r + 1) * rb] += _dot(a, wd_ref[...])

    @pl.when(j == pl.num_programs(1) - 1)
    def _():
        _ffn_finish(y_ref[...], x1_ref, g_ref, y_ref)


def _ffn_prompt(h2, x1, w_up, w_conv, b_conv, w_down, g, *, seq, tm, tf):
    m, d = x1.shape
    d_ff = w_down.shape[0]
    nf = d_ff // tf
    halo = SUBLANES_BF16
    hb = tm // halo
    col = lambda rows, off: pl.BlockSpec((rows, tf), lambda i, j: (0, j + off))
    tok = lambda n: pl.BlockSpec((tm, n), lambda i, j: (i, 0))
    cs = pl.BlockSpec((8, tf), lambda i, j: (i, j))
    cs_shape = jax.ShapeDtypeStruct((m // tm * 8, d_ff), F32)
    cw = min(tf, 2 * LANES)
    rb = min(tm, 2 * LANES)
    x1_spec = pl.BlockSpec((tm, d), lambda i, j: (i, 0), pipeline_mode=pl.Buffered(1))
    return pl.pallas_call(
        functools.partial(_ffn_prompt_kernel, tm=tm, rb=rb, halo=halo, cw=cw, seq_tiles=seq // tm),
        grid=(m // tm, nf),
        in_specs=[pl.BlockSpec((halo, d), lambda i, j: (jnp.maximum(i * hb - 1, 0), 0)),
                  tok(d), x1_spec, col(d, 0), col(d, nf), col(CONV_W, 0), col(CONV_W, nf),
                  col(1, 0), col(1, nf), pl.BlockSpec((tf, d), lambda i, j: (j, 0)),
                  pl.BlockSpec(g.shape, lambda i, j: (0, 0))],
        out_specs=(tok(d), cs, cs),
        out_shape=(jax.ShapeDtypeStruct((m, d), F32), cs_shape, cs_shape),
        compiler_params=_cparams("parallel", "arbitrary"),
        name="ffn_prompt",
    )(h2, h2, x1, w_up, w_up, w_conv, w_conv, b_conv, b_conv, w_down, g)


def _ffn_sample_kernel(h_ref, x1_ref, ctxg_ref, ctxv_ref, wug_ref, wuv_ref, wcg_ref, wcv_ref,
                       bcg_ref, bcv_ref, wd_ref, g_ref, y_ref, csg_ref, csv_ref, acc_scr, *, nb):
    j = pl.program_id(0)
    m = h_ref.shape[0]
    h = h_ref[...]

    def conv(w_up_ref, ctx_ref, wc_ref, bc_ref, cs_ref):
        u = _dot(h, w_up_ref[...])
        up = jnp.concatenate([ctx_ref[0], ctx_ref[1], u], axis=0)
        cs_ref[0] = u[m - 2 * nb:m - nb]
        cs_ref[1] = u[m - nb:m]
        return (bc_ref[...] + wc_ref[0:1] * up[0:m] + wc_ref[1:2] * up[nb:nb + m]
                + wc_ref[2:3] * up[2 * nb:2 * nb + m])

    a = _geglu(conv(wug_ref, ctxg_ref, wcg_ref, bcg_ref, csg_ref),
               conv(wuv_ref, ctxv_ref, wcv_ref, bcv_ref, csv_ref))
    part = _dot(a, wd_ref[...])

    @pl.when(j == 0)
    def _():
        acc_scr[...] = part

    @pl.when(j > 0)
    def _():
        acc_scr[...] += part

    @pl.when(j == pl.num_programs(0) - 1)
    def _():
        _ffn_finish(acc_scr[...], x1_ref, g_ref, y_ref)


def _ffn_sample(h2, x1, ctx, w_up, w_conv, b_conv, w_down, g, *, tf):
    m, d = x1.shape
    nb = ctx.shape[1]
    d_ff = w_down.shape[0]
    nf = d_ff // tf
    col = lambda rows, off: pl.BlockSpec((rows, tf), lambda j: (0, j + off))
    full = lambda a: pl.BlockSpec(a.shape, lambda j: (0, 0))
    ctx_spec = lambda off: pl.BlockSpec((CONV_W - 1, nb, tf), lambda j: (0, 0, j + off))
    cs = pl.BlockSpec((CONV_W - 1, nb, tf), lambda j: (0, 0, j))
    cs_shape = jax.ShapeDtypeStruct((CONV_W - 1, nb, d_ff), F32)
    return pl.pallas_call(
        functools.partial(_ffn_sample_kernel, nb=nb),
        grid=(nf,),
        in_specs=[full(h2), full(x1), ctx_spec(0), ctx_spec(nf), col(d, 0), col(d, nf),
                  col(CONV_W, 0), col(CONV_W, nf), col(1, 0), col(1, nf),
                  pl.BlockSpec((tf, d), lambda j: (j, 0)), full(g)],
        out_specs=(full(x1), cs, cs),
        out_shape=(jax.ShapeDtypeStruct((m, d), F32), cs_shape, cs_shape),
        scratch_shapes=[pltpu.VMEM((m, d), F32)],
        compiler_params=_cparams("arbitrary"),
        name="ffn_sample",
    )(h2, x1, ctx, ctx, w_up, w_up, w_conv, w_conv, b_conv, b_conv, w_down, g)


def _pick(n, candidates):
    for c in candidates:
        if n % c == 0:
            return c
    raise ValueError(f"no tile in {candidates} divides {n}")


def _layer_weights(l, p):
    d, n_in = p['w_in'].shape[1:]
    n_attn = p['cache_heads'] * 2 * DH_A
    w = p['w_in'][l]
    o_gla = 3 * n_attn
    o_lr = o_gla + 2 * H_B * DK_B + 2 * H_B * DV_B
    o_gate = o_lr + GATE_RANK
    assert n_in == o_gate + 2 * d
    w_lead_t = w.T.astype(BF16)
    w_gate_t = w_lead_t[o_gate:]
    wg = jnp.concatenate([p['w_gla_gate'][l].astype(F32),
                          jnp.zeros((LANES - GATE_RANK, H_B * DK_B), F32)], axis=0)
    wg_hi = wg.astype(BF16)
    wg_lo = (wg - wg_hi.astype(F32)).astype(BF16)
    row = lambda a: a[l].reshape(1, -1).astype(F32)
    return dict(
        w_lead_t=w_lead_t, w_gate_t=w_gate_t, n_attn=n_attn, o_vr=o_gla + 2 * H_B * DK_B, o_lr=o_lr,
        wg_hi=wg_hi, wg_lo=wg_lo, bg=row(p['b_gla_gate']),
        g_pre_mix=row(p['g_pre_mix']), g_head_diff=row(p['g_head_diff']), g_head_gla=row(p['g_head_gla']),
        wpd=p['w_proj_diff'][l].astype(BF16), wpg=p['w_proj_gla'][l].astype(BF16),
        wo=p['w_out'][l].astype(BF16), g_post_mix=row(p['g_post_mix']), g_pre_ffn=row(p['g_pre_ffn']),
        w_up=p['w_up'][l].astype(BF16), w_conv=p['w_conv'][l].astype(F32), b_conv=row(p['b_conv']),
        w_down=p['w_down'][l].astype(BF16), g_post_ffn=row(p['g_post_ffn']))


def _token_stages(x, lw, *, tm_proj, tm_mix, transposed_v):
    h, *proj = _attn_proj(x, lw['g_pre_mix'], lw['w_lead_t'], lw['o_lr'], lw['wg_hi'], lw['wg_lo'],
                          lw['bg'], n_attn=lw['n_attn'], tm=tm_proj, transposed_v=transposed_v)
    nv = H_B * DV_B
    zb = _act_proj(h, lw['w_lead_t'], lw['o_vr'], 2 * nv, ('id',) * 4 + ('silu',) * 4,
                   tm=tm_mix, tn=2 * nv, name="gla_proj")
    zg = _act_proj(h, lw['w_gate_t'], 0, lw['w_gate_t'].shape[0], ('sigmoid',) * 4,
                   tm=tm_mix, tn=nv, name="gate_proj")
    return proj, zb, zg


def kernel(x_prompt, x_sample, cache_k, cache_v, page_table, state_gla, state_conv, rel_bias,
           g_pre_mix, w_in, lambda_q1, lambda_k1, lambda_q2, lambda_k2, g_head_diff,
           w_gla_gate, b_gla_gate, g_head_gla, w_proj_diff, w_proj_gla, w_out, g_post_mix,
           g_pre_ffn, w_up, w_conv, b_conv, w_down, g_post_ffn):
    batch, seq, d = x_prompt.shape
    nb, t_dec, _ = x_sample.shape
    depth, n_pool, page, n_heads, kw = cache_k.shape
    assert page == PAGE_SIZE and kw == 2 * DH_A and cache_v.shape[-1] == DV_A
    d_ff = w_down.shape[1]
    width = n_heads * DV_A
    params = dict(w_in=w_in, w_gla_gate=w_gla_gate, b_gla_gate=b_gla_gate, g_pre_mix=g_pre_mix,
                  g_head_diff=g_head_diff, g_head_gla=g_head_gla, w_proj_diff=w_proj_diff,
                  w_proj_gla=w_proj_gla, w_out=w_out, g_post_mix=g_post_mix, g_pre_ffn=g_pre_ffn,
                  w_up=w_up, w_conv=w_conv, b_conv=b_conv, w_down=w_down, g_post_ffn=g_post_ffn,
                  cache_heads=n_heads)

    mp = batch * seq
    ms = nb * t_dec
    tq = _pick(seq, (512, 256, 128))
    tm_proj = _pick(mp, (256, 128))
    tm_mix = _pick(mp, (1024, 512, 256, 128))
    tm_ffn = _pick(seq, (1024, 512, 256, 128))
    tf = _pick(d_ff, (512, 256, 128))
    chunk = _pick(seq, (GLA_CHUNK,))
    pps = _pick(page_table.shape[1], (16, 8, 4, 2, 1))
    prompt_steps = batch * n_heads * (seq // tq)
    pps_fused = next((c for c in (16, 8, 4, 2, 1) if page_table.shape[1] % c == 0
                      and nb * (page_table.shape[1] // c) == prompt_steps), 0)
    t_pad = SUBLANES_BF16

    page_table = page_table.astype(jnp.int32)
    eye_2 = jnp.eye(2, dtype=BF16)

    yp = x_prompt.reshape(mp, d)
    ys = x_sample.reshape(ms, d)
    outs = [[] for _ in range(8)]
    for l in range(depth):
        lam0 = _lambda_init(l)
        lw = _layer_weights(l, params)
        toe, sb_far, sb_last, sb_new, lam = _bias_tables(rel_bias, lambda_q1[l], lambda_k1[l], lambda_q2[l],
                                                 lambda_k2[l], n_heads=n_heads, t_dec=t_dec, lam_init=lam0)

        (qkv, k32, v32, zf, log_a, vt), zb, zg = _token_stages(yp, lw, tm_proj=tm_proj, tm_mix=tm_mix,
                                                               transposed_v=True)
        (qkv_s, k32_s, v32_s, zf_s, log_a_s), zb_s, zg_s = _token_stages(ys, lw, tm_proj=ms, tm_mix=ms,
                                                                         transposed_v=False)
        q5 = (qkv_s[:, :width].astype(F32) * LOG2_E).astype(BF16).reshape(nb, t_dec, n_heads, 2, DH_A)
        q_rows = jnp.einsum('bthnd,mn->bmthnd', q5, eye_2).reshape(nb, 2 * t_dec * n_heads, 2 * DH_A)
        per_head = lambda a: a.reshape(nb, t_dec * n_heads, -1)
        kn, vn = per_head(qkv_s[:, width:2 * width]), per_head(qkv_s[:, 2 * width:])
        if pps_fused:
            oa, oa_s = _attn_fused(qkv, vt, toe, q_rows, kn, vn, cache_k, cache_v, page_table, l,
                                   sb_far, sb_last, sb_new, lam, lw['g_head_diff'], batch=batch, seq=seq,
                                   n_heads=n_heads, tq=tq, pps=pps_fused, lam_init=lam0)
        else:
            oa = _attn_prompt(qkv, vt, toe, lam, lw['g_head_diff'], batch=batch, seq=seq,
                              n_heads=n_heads, tq=tq, lam_init=lam0)
            oa_s = _attn_sample(q_rows, kn, vn, cache_k, cache_v, page_table, l, sb_far, sb_last, sb_new,
                                lam, lw['g_head_diff'], pps=pps, lam_init=lam0)

        ob, s_p = _gla_prompt(zf, log_a, zb, lw['g_head_gla'], batch=batch, seq=seq, chunk=chunk,
                              sub=GLA_SUB)
        x1, h2 = _merge(oa, ob.reshape(mp, -1), zg, yp, lw['wpd'], lw['wpg'], lw['wo'],
                        lw['g_post_mix'], lw['g_pre_ffn'], tm=tm_proj)
        yp, csg, csv = _ffn_prompt(h2, x1, lw['w_up'], lw['w_conv'], lw['b_conv'], lw['w_down'],
                                   lw['g_post_ffn'], seq=seq, tm=tm_ffn, tf=tf)
        cs = jnp.concatenate([csg, csv], axis=1).reshape(batch, seq // tm_ffn, 8, 2 * d_ff)
        outs[0].append(k32.reshape(batch, seq, n_heads, 2 * DH_A))
        outs[1].append(v32.reshape(batch, seq, n_heads, DV_A))
        outs[2].append(s_p)
        outs[3].append(cs[:, -1, 8 - (CONV_W - 1):])

        pad_t = lambda a: jnp.pad(a.reshape(nb, t_dec, -1), ((0, 0), (0, t_pad - t_dec), (0, 0)))
        ob, s_s = _gla_sample(pad_t(zf_s), pad_t(log_a_s), pad_t(zb_s), lw['g_head_gla'], state_gla[l])
        x1, h2 = _merge(oa_s.reshape(ms, -1), ob[:, :t_dec].reshape(ms, -1), zg_s, ys, lw['wpd'],
                        lw['wpg'], lw['wo'], lw['g_post_mix'], lw['g_pre_ffn'], tm=ms)
        tmajor = lambda a: a.reshape(nb, t_dec, -1).transpose(1, 0, 2).reshape(ms, -1)
        y_t, csg, csv = _ffn_sample(tmajor(h2), tmajor(x1), state_conv[l].transpose(1, 0, 2),
                                    lw['w_up'], lw['w_conv'], lw['b_conv'], lw['w_down'],
                                    lw['g_post_ffn'], tf=tf)
        ys = y_t.reshape(t_dec, nb, d).transpose(1, 0, 2).reshape(ms, d)
        outs[4].append(k32_s.reshape(nb, t_dec, n_heads, 2 * DH_A))
        outs[5].append(v32_s.reshape(nb, t_dec, n_heads, DV_A))
        outs[6].append(s_s)
        outs[7].append(jnp.concatenate([csg, csv], axis=2).transpose(1, 0, 2))

    return (yp.reshape(batch, seq, d), ys.reshape(nb, t_dec, d)) + tuple(jnp.stack(o) for o in outs)
```

```python
import functools
import math

import numpy as np
import jax
import jax.numpy as jnp
from jax import lax
from jax.experimental import pallas as pl
from jax.experimental.pallas import tpu as pltpu

F32 = jnp.float32
BF16 = jnp.bfloat16

DH_A = 64
DV_A = 128
DK_B = 128
DV_B = 256
H_B = 4
GATE_RANK = 16
GATE_TAU = 16.0
NUM_BUCKETS = 32
MAX_DISTANCE = 128
CONV_W = 3
PAGE_SIZE = 128
EPS = 1e-6
NEG_INF = -1e30
LOG2_E = math.log2(math.e)

LANES = 128
SUBLANES_F32 = 8
SUBLANES_BF16 = 16
VT_ROWS = DV_A + SUBLANES_BF16
VMEM_LIMIT_BYTES = 56 * 1024 * 1024

SAMPLE_BLOCK_COLS = 1024
GLA_CHUNK = 64
GLA_SUB = 16


def _lambda_init(layer):
    return 0.8 - 0.6 * math.exp(-0.3 * layer)


def _cparams(*sem):
    return pltpu.CompilerParams(dimension_semantics=sem, vmem_limit_bytes=VMEM_LIMIT_BYTES)


def _const_spec(shape):
    nd = len(shape)
    return pl.BlockSpec(shape, lambda *_: (0,) * nd, pipeline_mode=pl.Buffered(1))


def _rms(xf, g):
    return xf * lax.rsqrt(jnp.mean(xf * xf, axis=-1, keepdims=True) + EPS) * g


def _dot(a, b):
    return jnp.dot(a, b, preferred_element_type=F32)


def _dot_nt(a, b):
    return lax.dot_general(a, b, (((1,), (1,)), ((), ())), preferred_element_type=F32)


def _dot_tn(a, b):
    return lax.dot_general(a, b, (((0,), (0,)), ((), ())), preferred_element_type=F32)


def _split_bf16(x):
    hi = x.astype(BF16)
    lo = (x - hi.astype(F32)).astype(BF16)
    return hi, lo


def _t5_bucket_np(rel):
    n = np.maximum(rel, 0)
    max_exact = NUM_BUCKETS // 2
    nf = np.maximum(n, 1).astype(np.float32)
    large = max_exact + (np.log(nf / max_exact) / math.log(MAX_DISTANCE / max_exact)
                         * (NUM_BUCKETS - max_exact)).astype(np.int32)
    large = np.minimum(large, NUM_BUCKETS - 1)
    return np.where(n < max_exact, n, large).astype(np.int32)


def _bias_tables_kernel(relb_ref, bk_toe_ref, bk_far_ref, bk_last_ref, bk_new_ref,
                        lq1_ref, lk1_ref, lq2_ref, lk2_ref,
                        toe_ref, sb_far_ref, sb_last_ref, sb_new_ref, lam_ref, *, n_heads, lam_init):
    far = NUM_BUCKETS - 1

    def lookup(bk, h):
        out = jnp.full(bk.shape, NEG_INF, F32)
        for b in range(NUM_BUCKETS):
            out = jnp.where(bk == b, relb_ref[b, h] - relb_ref[far, h], out)
        return out

    for h in range(n_heads):
        toe_ref[0, h] = lookup(bk_toe_ref[0], h)
        toe_ref[1, h] = lookup(bk_toe_ref[1], h)

    def per_row_head(bk_ref, out_ref):
        row_h = lax.broadcasted_iota(jnp.int32, bk_ref.shape, 0) % n_heads
        out = jnp.zeros(bk_ref.shape, F32)
        for h in range(n_heads):
            out = jnp.where(row_h == h, lookup(bk_ref[...], h), out)
        out_ref[...] = out * -LOG2_E

    per_row_head(bk_far_ref, sb_far_ref)
    per_row_head(bk_last_ref, sb_last_ref)
    per_row_head(bk_new_ref, sb_new_ref)
    d1 = jnp.sum(lq1_ref[...] * lk1_ref[...], axis=-1, keepdims=True)
    d2 = jnp.sum(lq2_ref[...] * lk2_ref[...], axis=-1, keepdims=True)
    lam = jnp.exp(d1) - jnp.exp(d2) + lam_init
    lam_ref[...] = jnp.broadcast_to(lam, lam_ref.shape)


def _bias_tables(rel_bias, lq1, lk1, lq2, lk2, *, n_heads, t_dec, lam_init):
    i = np.arange(LANES)[:, None]
    j = np.arange(LANES)[None, :]
    bk_toe = np.stack([np.where(j >= i, _t5_bucket_np(j - i), -1),
                       _t5_bucket_np(LANES + j - i)]).astype(np.int32)
    rows = 2 * t_dec * n_heads
    r = np.arange(rows)[:, None]
    t, h = (r // n_heads) % t_dec, r % n_heads
    c = np.arange(PAGE_SIZE * n_heads)[None, :]
    key, hk = c // n_heads, c % n_heads
    assert n_heads == SUBLANES_F32
    bk_far = np.where(hk == h, NUM_BUCKETS - 1, -1)[:n_heads, :LANES].astype(np.int32)
    bk_last = np.where(hk == h, _t5_bucket_np(PAGE_SIZE + t - key), -1).astype(np.int32)
    cn = c[:, :LANES]
    key, hk = cn // n_heads, cn % n_heads
    bk_new = np.where((hk == h) & (key <= t) & (key < t_dec), _t5_bucket_np(t - key), -1).astype(np.int32)
    vec = lambda a: a.reshape(1, -1).astype(F32)
    smem = pl.BlockSpec(memory_space=pltpu.SMEM)
    tab = lambda a: jax.ShapeDtypeStruct(a.shape, F32)
    return pl.pallas_call(
        functools.partial(_bias_tables_kernel, n_heads=n_heads, lam_init=lam_init),
        out_shape=(jax.ShapeDtypeStruct((2, n_heads, LANES, LANES), F32),
                   tab(bk_far), tab(bk_last), tab(bk_new),
                   jax.ShapeDtypeStruct((8, LANES), F32)),
        in_specs=[smem] + [pl.BlockSpec(memory_space=pltpu.VMEM)] * 8,
        name="bias_tables",
    )(rel_bias.astype(F32), jnp.asarray(bk_toe), jnp.asarray(bk_far), jnp.asarray(bk_last),
      jnp.asarray(bk_new), vec(lq1), vec(lk1), vec(lq2), vec(lk2))


def _attn_proj_kernel(x_ref, g_ref, w_ref, wlr_ref, wgh_ref, wgl_ref, bg_ref,
                      h_ref, qkv_ref, k32_ref, v32_ref, zf_ref, loga_ref, *maybe_vt_ref, n_attn, n_qk):
    h = _rms(x_ref[...], g_ref[...]).astype(BF16)
    h_ref[...] = h
    q = _dot_nt(h, w_ref[0:n_attn]) * DH_A ** -0.5
    qkv_ref[:, 0:n_attn] = q.astype(BF16)
    k = _dot_nt(h, w_ref[n_attn:2 * n_attn])
    k32_ref[...] = k
    qkv_ref[:, n_attn:2 * n_attn] = k.astype(BF16)
    v = _dot_nt(h, w_ref[2 * n_attn:3 * n_attn])
    v32_ref[...] = v
    qkv_ref[:, 2 * n_attn:3 * n_attn] = v.astype(BF16)
    for vt_ref in maybe_vt_ref:
        vt = v.T.astype(BF16)
        ones = jnp.ones((VT_ROWS - DV_A, vt.shape[1]), BF16)
        vt_ref[...] = jnp.concatenate(
            [part for h in range(n_attn // DV_A) for part in (vt[h * DV_A:(h + 1) * DV_A], ones)], axis=0)
    zf_ref[:, 0:n_qk] = _dot_nt(h, w_ref[3 * n_attn:3 * n_attn + n_qk]) * DK_B ** -0.5
    zf_ref[:, n_qk:2 * n_qk] = _dot_nt(h, w_ref[3 * n_attn + n_qk:3 * n_attn + 2 * n_qk])
    lr = _dot_nt(h, wlr_ref[...])
    lr_hi, lr_lo = _split_bf16(lr)
    pre = (_dot(lr_hi, wgh_ref[...]) + _dot(lr_hi, wgl_ref[...]) + _dot(lr_lo, wgh_ref[...])
           + bg_ref[...])
    log_sig = jnp.minimum(pre, 0.0) - jnp.log1p(jnp.exp(-jnp.abs(pre)))
    loga_ref[...] = log_sig / GATE_TAU


def _attn_proj(x, g, w_t, o_lr, wgh, wgl, bg, *, n_attn, tm, transposed_v):
    m, d = x.shape
    n_qk = H_B * DK_B
    n_lead = 3 * n_attn + 2 * n_qk
    assert o_lr % LANES == 0
    row = lambda n: pl.BlockSpec((tm, n), lambda i: (i, 0))
    out_specs = [row(d), row(3 * n_attn), row(n_attn), row(n_attn), row(2 * n_qk), row(n_qk)]
    out_shape = [jax.ShapeDtypeStruct((m, d), BF16),
                 jax.ShapeDtypeStruct((m, 3 * n_attn), BF16),
                 jax.ShapeDtypeStruct((m, n_attn), F32),
                 jax.ShapeDtypeStruct((m, n_attn), F32),
                 jax.ShapeDtypeStruct((m, 2 * n_qk), F32),
                 jax.ShapeDtypeStruct((m, n_qk), F32)]
    if transposed_v:
        vt_rows = n_attn // DV_A * VT_ROWS
        out_specs.append(pl.BlockSpec((vt_rows, tm), lambda i: (0, i)))
        out_shape.append(jax.ShapeDtypeStruct((vt_rows, m), BF16))
    once = pl.Buffered(1)
    return pl.pallas_call(
        functools.partial(_attn_proj_kernel, n_attn=n_attn, n_qk=n_qk),
        grid=(m // tm,),
        in_specs=[row(d), _const_spec(g.shape),
                  pl.BlockSpec((n_lead, d), lambda i: (0, 0), pipeline_mode=once),
                  pl.BlockSpec((LANES, d), lambda i: (o_lr // LANES, 0), pipeline_mode=once),
                  _const_spec(wgh.shape), _const_spec(wgl.shape), _const_spec(bg.shape)],
        out_specs=tuple(out_specs),
        out_shape=tuple(out_shape),
        compiler_params=_cparams("parallel"),
        name="attn_proj",
    )(x, g, w_t, w_t, wgh, wgl, bg)


def _sigmoid(x):
    return 0.5 * jnp.tanh(0.5 * x) + 0.5


_ACTS = {
    'id': lambda x: x,
    'silu': lambda x: x * _sigmoid(x),
    'sigmoid': _sigmoid,
}


def _act_proj_kernel(h_ref, w_ref, o_ref, *, acts, cw):
    h = h_ref[...]
    accs = [_dot_nt(h, w_ref[c * cw:(c + 1) * cw]) for c in range(len(acts))]
    for c, (act, acc) in enumerate(zip(acts, accs)):
        o_ref[:, c * cw:(c + 1) * cw] = _ACTS[act](acc).astype(BF16)


def _act_proj(h, w_t, row0, n, acts, *, tm, tn, name):
    m, d = h.shape
    cw = tn // len(acts)
    assert row0 % tn == 0 and n % tn == 0 and cw % LANES == 0
    j0 = row0 // tn
    nj = n // tn
    w_mode = dict(pipeline_mode=pl.Buffered(1)) if nj == 1 else {}
    return pl.pallas_call(
        functools.partial(_act_proj_kernel, acts=acts, cw=cw),
        grid=(m // tm, nj),
        in_specs=[pl.BlockSpec((tm, d), lambda i, j: (i, 0)),
                  pl.BlockSpec((tn, d), lambda i, j: (j0 + j, 0), **w_mode)],
        out_specs=pl.BlockSpec((tm, tn), lambda i, j: (i, j)),
        out_shape=jax.ShapeDtypeStruct((m, n), BF16),
        compiler_params=_cparams("parallel", "arbitrary"),
        name=name,
    )(h, w_t)


def _softmax_step(s, v, state):
    m_prev, l_prev, acc_prev = state
    m_new = jnp.maximum(m_prev, jnp.max(s, axis=-1, keepdims=True))
    alpha = jnp.exp2(m_prev - m_new)
    p = jnp.exp2(s - m_new)
    return (m_new, alpha * l_prev + jnp.sum(p, axis=-1, keepdims=True),
            alpha * acc_prev + _dot(p.astype(BF16), v))


def _head_norm(o, g, lam_init):
    return o * lax.rsqrt(jnp.mean(o * o, axis=-1, keepdims=True) + EPS) * g * (1.0 - lam_init)


def _attn_prompt_kernel(lam_ref, q_ref, k_ref, vt_ref, toe_ref, g_ref, o_ref,
                        bias_scr, *scr, tq, cb, ahead, lam_init):
    qi = pl.program_id(2)
    nb = tq // LANES

    @pl.when(qi == 0)
    def _():
        for r in range(nb):
            for c in range(nb):
                blk = (slice(r * LANES, (r + 1) * LANES), slice(c * LANES, (c + 1) * LANES))
                for t, d in ((0, tq + (c - r) * LANES), (1, (c - r) * LANES)):
                    if d < 0:
                        bias_scr[(t,) + blk] = jnp.full((LANES, LANES), -NEG_INF, F32)
                    elif d == 0:
                        bias_scr[(t,) + blk] = toe_ref[0] * -LOG2_E
                    elif d == LANES:
                        bias_scr[(t,) + blk] = toe_ref[1] * -LOG2_E
                    else:
                        bias_scr[(t,) + blk] = jnp.zeros((LANES, LANES), F32)

    n_cb = 2 * tq // cb
    chains = [scr[3 * c:3 * c + 3] for c in range(n_cb)]
    q = (q_ref[...].astype(F32) * LOG2_E).astype(BF16)
    lane = lax.broadcasted_iota(jnp.int32, (cb, q.shape[1]), 1)
    for c, (q2_scr, m_scr, acc_scr) in enumerate(chains):
        q_blk = q[(c * cb) % tq:(c * cb) % tq + cb]
        keep = (lane < DH_A) if c * cb < tq else (lane >= DH_A)
        q2_scr[...] = jnp.where(keep, q_blk, jnp.zeros_like(q_blk))
        m_scr[...] = jnp.full(m_scr.shape, NEG_INF, F32)
        acc_scr[...] = jnp.zeros(acc_scr.shape, F32)

    def kv_tiles(js, bias_idx):
        offs = [pl.multiple_of(j * tq, tq) for j in js]
        ks = [k_ref[pl.ds(off, tq), :] for off in offs]
        vts = [vt_ref[:, pl.ds(off, tq)] for off in offs]
        bias = None if bias_idx is None else [
            [bias_scr[bi, :, b0:b0 + cb] for b0 in range(0, tq, cb)] for bi in bias_idx]
        blocks = [(t, c) for t in range(len(js)) for c in range(n_cb)]
        score = lambda t, c: _dot_nt(ks[t], chains[c][0][...])
        pending = [score(t, c) for t, c in blocks[:ahead]]
        for n, (t, c) in enumerate(blocks):
            _, m_scr, acc_scr = chains[c]
            s = pending.pop(0)
            if n + ahead < len(blocks):
                pending.append(score(*blocks[n + ahead]))
            if bias is not None:
                s = s - bias[t][c % len(bias[t])]
            m_prev = m_scr[...]
            m_new = jnp.maximum(m_prev, jnp.max(s, axis=0, keepdims=True))
            alpha = jnp.exp2(m_prev - m_new)
            p = jnp.exp2((s - m_new).astype(BF16))
            acc_scr[...] = alpha * acc_scr[...] + _dot(vts[t], p)
            m_scr[...] = m_new

    def far_pair(i, carry):
        kv_tiles([2 * i, 2 * i + 1], None)
        return carry

    def far_tile(j, carry):
        kv_tiles([j], None)
        return carry

    n_far = jnp.maximum(qi - 1, 0)
    lax.fori_loop(0, n_far // 2, far_pair, 0)
    lax.fori_loop(n_far // 2 * 2, n_far, far_tile, 0)

    @pl.when(qi == 0)
    def _():
        kv_tiles([0], [1])

    @pl.when(qi > 0)
    def _():
        kv_tiles([qi - 1, qi], [0, 1])

    on = jnp.concatenate([acc_scr[0:DV_A] / acc_scr[DV_A:DV_A + 1] for _, _, acc_scr in chains], axis=1)
    lam = lam_ref[0:1, 0:1]
    o_t = on[:, 0:tq] - lam * on[:, tq:2 * tq]
    o_t = o_t * lax.rsqrt(jnp.mean(o_t * o_t, axis=0, keepdims=True) + EPS)
    o_ref[...] = (o_t.T * (g_ref[...] * (1.0 - lam_init))).astype(BF16)


def _attn_prompt(qkv, vt, toe, lam, g_head, *, batch, seq, n_heads, tq, lam_init):
    nq = seq // tq
    koff = n_heads
    cb = min(tq, 2 * LANES)
    return pl.pallas_call(
        functools.partial(_attn_prompt_kernel, tq=tq, cb=cb, ahead=4, lam_init=lam_init),
        grid=(batch, n_heads, nq),
        in_specs=[pl.BlockSpec(lam.shape, lambda b, h, i: (0, 0)),
                  pl.BlockSpec((tq, LANES), lambda b, h, i: (b * nq + i, h)),
                  pl.BlockSpec((seq, LANES), lambda b, h, i: (b, koff + h)),
                  pl.BlockSpec((VT_ROWS, seq), lambda b, h, i: (h, b)),
                  pl.BlockSpec((2, None, LANES, LANES), lambda b, h, i: (0, h, 0, 0)),
                  pl.BlockSpec(g_head.shape, lambda b, h, i: (0, 0))],
        out_specs=pl.BlockSpec((tq, LANES), lambda b, h, i: (b * nq + i, h)),
        out_shape=jax.ShapeDtypeStruct((batch * seq, n_heads * LANES), BF16),
        scratch_shapes=[pltpu.VMEM((2, tq, tq), F32)]
        + [pltpu.VMEM((cb, LANES), BF16), pltpu.VMEM((1, cb), F32),
           pltpu.VMEM((VT_ROWS, cb), F32)] * (2 * tq // cb),
        compiler_params=_cparams("parallel", "parallel", "arbitrary"),
        name="attn_prompt",
    )(lam, qkv, qkv, vt, toe, g_head)


def _attn_sample_kernel(pt_ref, lam_ref, q_ref, kn_ref, vn_ref, sbf_ref, sbl_ref, sbn_ref, g_ref, *rest,
                        pps, bc, lam_init, g=None, last=None):
    k_refs = rest[:pps]
    v_refs = rest[pps:2 * pps]
    o_ref = rest[2 * pps]
    m_scr, l_scr, acc_scr = rest[2 * pps + 1:]
    if g is None:
        g = pl.program_id(1)
        last = pl.num_programs(1) - 1

    @pl.when(g == 0)
    def _():
        m_scr[...] = jnp.full(m_scr.shape, NEG_INF, F32)
        l_scr[...] = jnp.zeros(l_scr.shape, F32)
        acc_scr[...] = jnp.zeros(acc_scr.shape, F32)

    q = q_ref[...]
    cols = sbl_ref.shape[1]
    blocks = [(p, c0) for p in range(pps) for c0 in range(0, cols, bc)]
    rows_of = lambda refs, blk: (
        refs[blk[0]][...].reshape(cols, refs[blk[0]].shape[-1])[blk[1]:blk[1] + bc].astype(BF16))
    ahead = 5
    pending = [_dot_nt(q, rows_of(k_refs, blk)) for blk in blocks[:ahead]]
    far = jnp.tile(sbf_ref[...], (q.shape[0] // sbf_ref.shape[0], bc // LANES))
    state = (m_scr[...], l_scr[...], acc_scr[...])
    for n, blk in enumerate(blocks):
        s = pending.pop(0)
        if n + ahead < len(blocks):
            pending.append(_dot_nt(q, rows_of(k_refs, blocks[n + ahead])))
        if blk[0] == pps - 1:
            s = s - jnp.where(g == last, sbl_ref[:, blk[1]:blk[1] + bc], far)
        else:
            s = s - far
        state = _softmax_step(s, rows_of(v_refs, blk), state)
    m_scr[...], l_scr[...], acc_scr[...] = state

    @pl.when(g == last)
    def _():
        pad = jnp.zeros((sbn_ref.shape[1] - kn_ref.shape[0], kn_ref.shape[1]), BF16)
        kn = jnp.concatenate([kn_ref[...], pad], axis=0)
        vn = jnp.concatenate([vn_ref[...], pad], axis=0)
        _, l_fin, acc_fin = _softmax_step(_dot_nt(q, kn) - sbn_ref[...], vn, state)
        on = acc_fin / l_fin
        half = on.shape[0] // 2
        o = on[0:half] - lam_ref[0:1, 0:1] * on[half:2 * half]
        o_ref[...] = _head_norm(o, g_ref[...], lam_init).astype(BF16)


def _attn_sample(q, kn, vn, cache_k, cache_v, page_table, layer, sb_far, sb_last, sb_new, lam, g_head,
                 *, pps, lam_init):
    nb, rows, feat = q.shape
    n_pages = page_table.shape[1]
    steps = n_pages // pps
    seq_spec = lambda a: pl.BlockSpec((None,) + a.shape[1:], lambda b, g, pt: (b, 0, 0))
    full = lambda a: pl.BlockSpec(a.shape, lambda b, g, pt: (0, 0))

    def page_spec(jj):
        return pl.BlockSpec((None, None) + cache_k.shape[2:],
                            lambda b, g, pt: (layer, pt[b, g * pps + jj], 0, 0, 0))

    grid_spec = pltpu.PrefetchScalarGridSpec(
        num_scalar_prefetch=1,
        grid=(nb, steps),
        in_specs=[full(lam), seq_spec(q), seq_spec(kn), seq_spec(vn), full(sb_far), full(sb_last),
                  full(sb_new), full(g_head)] + [page_spec(jj) for jj in range(pps)] * 2,
        out_specs=pl.BlockSpec((None, rows // 2, feat), lambda b, g, pt: (b, 0, 0)),
        scratch_shapes=[pltpu.VMEM((rows, 1), F32), pltpu.VMEM((rows, 1), F32),
                        pltpu.VMEM((rows, feat), F32)],
    )
    return pl.pallas_call(
        functools.partial(_attn_sample_kernel, pps=pps, bc=SAMPLE_BLOCK_COLS, lam_init=lam_init),
        grid_spec=grid_spec,
        out_shape=jax.ShapeDtypeStruct((nb, rows // 2, feat), BF16),
        compiler_params=_cparams("parallel", "arbitrary"),
        name="attn_sample",
    )(page_table, lam, q, kn, vn, sb_far, sb_last, sb_new, g_head, *([cache_k] * pps), *([cache_v] * pps))


def _attn_fused_kernel(pt_ref, lam_ref, q_ref, k_ref, vt_ref, toe_ref, g_ref,
                       qs_ref, kn_ref, vn_ref, sbf_ref, sbl_ref, sbn_ref, *rest,
                       pps, bc, steps_s, n_prompt_scr, tq, cb, ahead, lam_init):
    pages = rest[:2 * pps]
    o_ref, os_ref = rest[2 * pps:2 * pps + 2]
    scr = rest[2 * pps + 2:]
    step = (pl.program_id(0) * pl.num_programs(1) + pl.program_id(1)) * pl.num_programs(2) + pl.program_id(2)
    _attn_sample_kernel(pt_ref, lam_ref, qs_ref, kn_ref, vn_ref, sbf_ref, sbl_ref, sbn_ref, g_ref,
                        *pages, os_ref, *scr[n_prompt_scr:], pps=pps, bc=bc, lam_init=lam_init,
                        g=step % steps_s, last=steps_s - 1)
    _attn_prompt_kernel(lam_ref, q_ref, k_ref, vt_ref, toe_ref, g_ref, o_ref, *scr[:n_prompt_scr],
                        tq=tq, cb=cb, ahead=ahead, lam_init=lam_init)


def _attn_fused(qkv, vt, toe, qs, kn, vn, cache_k, cache_v, page_table, layer, sb_far, sb_last, sb_new,
                lam, g_head, *, batch, seq, n_heads, tq, pps, lam_init):
    nq = seq // tq
    koff = n_heads
    cb = min(tq, 2 * LANES)
    nb, rows, feat = qs.shape
    steps_s = page_table.shape[1] // pps
    assert batch * n_heads * nq == nb * steps_s
    flat = lambda b, h, i: (b * n_heads + h) * nq + i
    full = lambda a: pl.BlockSpec(a.shape, lambda b, h, i, pt: (0,) * a.ndim)
    seq_spec = lambda a: pl.BlockSpec((None,) + a.shape[1:], lambda b, h, i, pt: (flat(b, h, i) // steps_s, 0, 0))

    def page_spec(jj):
        def index(b, h, i, pt):
            s = flat(b, h, i)
            return (layer, pt[s // steps_s, (s % steps_s) * pps + jj], 0, 0, 0)
        return pl.BlockSpec((None, None) + cache_k.shape[2:], index)

    prompt_scr = ([pltpu.VMEM((2, tq, tq), F32)]
                  + [pltpu.VMEM((cb, LANES), BF16), pltpu.VMEM((1, cb), F32),
                     pltpu.VMEM((VT_ROWS, cb), F32)] * (2 * tq // cb))
    sample_scr = [pltpu.VMEM((rows, 1), F32), pltpu.VMEM((rows, 1), F32), pltpu.VMEM((rows, feat), F32)]
    grid_spec = pltpu.PrefetchScalarGridSpec(
        num_scalar_prefetch=1,
        grid=(batch, n_heads, nq),
        in_specs=[full(lam),
                  pl.BlockSpec((tq, LANES), lambda b, h, i, pt: (b * nq + i, h)),
                  pl.BlockSpec((seq, LANES), lambda b, h, i, pt: (b, koff + h)),
                  pl.BlockSpec((VT_ROWS, seq), lambda b, h, i, pt: (h, b)),
                  pl.BlockSpec((2, None, LANES, LANES), lambda b, h, i, pt: (0, h, 0, 0)),
                  full(g_head), seq_spec(qs), seq_spec(kn), seq_spec(vn), full(sb_far), full(sb_last),
                  full(sb_new)] + [page_spec(jj) for jj in range(pps)] * 2,
        out_specs=(pl.BlockSpec((tq, LANES), lambda b, h, i, pt: (b * nq + i, h)),
                   pl.BlockSpec((None, rows // 2, feat), lambda b, h, i, pt: (flat(b, h, i) // steps_s, 0, 0))),
        scratch_shapes=prompt_scr + sample_scr,
    )
    return pl.pallas_call(
        functools.partial(_attn_fused_kernel, pps=pps, bc=SAMPLE_BLOCK_COLS, steps_s=steps_s,
                          n_prompt_scr=len(prompt_scr), tq=tq, cb=cb, ahead=4, lam_init=lam_init),
        grid_spec=grid_spec,
        out_shape=(jax.ShapeDtypeStruct((batch * seq, n_heads * LANES), BF16),
                   jax.ShapeDtypeStruct((nb, rows // 2, feat), BF16)),
        compiler_params=_cparams("arbitrary", "arbitrary", "arbitrary"),
        name="attn_fused",
    )(page_table, lam, qkv, qkv, vt, toe, g_head, qs, kn, vn, sb_far, sb_last, sb_new,
      *([cache_k] * pps), *([cache_v] * pps))


def _gla_chunk(heads, *, sub):
    c_len = heads[0][0].shape[0]
    ri = lax.broadcasted_iota(jnp.int32, (c_len, c_len), 0)
    ci = lax.broadcasted_iota(jnp.int32, (c_len, c_len), 1)
    row = lax.broadcasted_iota(jnp.int32, (c_len, 1), 0)
    tril = (ri >= ci).astype(BF16)
    bs = []
    for _, _, log_a, _, _ in heads:
        a_hi, a_lo = _split_bf16(log_a)
        bs.append(_dot(tril, a_hi) + _dot(tril, a_lo))
    outs, states = [], []
    for (q, k, _, v, st), b in zip(heads, bs):
        b_last = b[c_len - 1:c_len]
        outs.append(_dot_nt((q * jnp.exp(b)).astype(BF16), st.astype(BF16)))
        k_out = (k * jnp.exp(b_last - b)).astype(BF16)
        states.append(st * jnp.exp(b_last) + _dot_tn(v, k_out))
    atts = []
    for (q, k, _, _, _), b in zip(heads, bs):
        rows = []
        for i in range(c_len // sub):
            lo, hi = i * sub, (i + 1) * sub
            b_ref = b[lo:lo + 1]
            qh = (q[lo:hi] * jnp.exp(b[lo:hi] - b_ref)).astype(BF16)
            kh = (k * jnp.exp(jnp.where(row < hi, b_ref - b, NEG_INF))).astype(BF16)
            rows.append(_dot_nt(qh, kh))
        att = jnp.concatenate(rows, axis=0) if len(rows) > 1 else rows[0]
        atts.append(jnp.where(ri >= ci, att, 0.0).astype(BF16))
    outs = [o + _dot(att, v) for o, att, (_, _, _, v, _) in zip(outs, atts, heads)]
    return list(zip(outs, states))


def _gla_out(o, g, r):
    return (o * lax.rsqrt(jnp.mean(o * o, axis=-1, keepdims=True) + EPS) * g * r.astype(F32)).astype(BF16)


def _gla_prompt_kernel(qk_ref, la_ref, v_ref, r_ref, g_ref, o_ref, s_ref, st_scr, *, batch, sub):
    ci = pl.program_id(0)

    @pl.when(ci == 0)
    def _():
        st_scr[...] = jnp.zeros(st_scr.shape, F32)

    nk = H_B * DK_B
    ids = [(b, h) for b in range(batch) for h in range(H_B)]
    ks = lambda h: slice(h * DK_B, (h + 1) * DK_B)
    vs = lambda h: slice(h * DV_B, (h + 1) * DV_B)
    heads = [(qk_ref[b, :, ks(h)], qk_ref[b, :, nk + h * DK_B:nk + (h + 1) * DK_B], la_ref[b, :, ks(h)],
              v_ref[b, :, vs(h)], st_scr[b, h]) for b, h in ids]
    for (b, h), (o, st_new) in zip(ids, _gla_chunk(heads, sub=sub)):
        st_scr[b, h] = st_new
        o_ref[b, :, vs(h)] = _gla_out(o, g_ref[...], r_ref[b, :, vs(h)])

    @pl.when(ci == pl.num_programs(0) - 1)
    def _():
        for b in range(batch):
            for h in range(H_B):
                s_ref[b, h] = st_scr[b, h].T


def _gla_prompt(zf, log_a, zb, g_head, *, batch, seq, chunk, sub):
    nqk = zf.shape[1]
    nv = H_B * DV_B
    zf3 = zf.reshape(batch, seq, nqk)
    la3 = log_a.reshape(batch, seq, log_a.shape[1])
    zb3 = zb.reshape(batch, seq, zb.shape[1])
    blk = lambda n, col: pl.BlockSpec((batch, chunk, n), lambda c: (0, c, col))
    return pl.pallas_call(
        functools.partial(_gla_prompt_kernel, batch=batch, sub=sub),
        grid=(seq // chunk,),
        in_specs=[blk(nqk, 0), blk(la3.shape[2], 0), blk(nv, 0), blk(nv, 1),
                  pl.BlockSpec(g_head.shape, lambda c: (0, 0))],
        out_specs=(blk(nv, 0),
                   pl.BlockSpec((batch, H_B, DK_B, DV_B), lambda c: (0, 0, 0, 0))),
        out_shape=(jax.ShapeDtypeStruct((batch, seq, nv), BF16),
                   jax.ShapeDtypeStruct((batch, H_B, DK_B, DV_B), F32)),
        scratch_shapes=[pltpu.VMEM((batch, H_B, DV_B, DK_B), F32)],
        compiler_params=_cparams("arbitrary"),
        name="gla_prompt",
    )(zf3, la3, zb3, zb3, g_head)


def _gla_sample_kernel(qk_ref, la_ref, v_ref, r_ref, g_ref, s0_ref, o_ref, s_ref, *, sub):
    nk = H_B * DK_B
    ks = lambda h: slice(h * DK_B, (h + 1) * DK_B)
    vs = lambda h: slice(h * DV_B, (h + 1) * DV_B)
    heads = [(qk_ref[:, ks(h)], qk_ref[:, nk + h * DK_B:nk + (h + 1) * DK_B], la_ref[:, ks(h)],
              v_ref[:, vs(h)], s0_ref[h].T) for h in range(H_B)]
    for h, (o, st_new) in enumerate(_gla_chunk(heads, sub=sub)):
        s_ref[h] = st_new.T
        o_ref[:, vs(h)] = _gla_out(o, g_ref[...], r_ref[:, vs(h)])


def _gla_sample(zf3, la3, zb3, g_head, s0):
    nb, tp, nqk = zf3.shape
    nv = H_B * DV_B
    blk = lambda n, col: pl.BlockSpec((None, tp, n), lambda b: (b, 0, col))
    st = pl.BlockSpec((None, H_B, DK_B, DV_B), lambda b: (b, 0, 0, 0))
    return pl.pallas_call(
        functools.partial(_gla_sample_kernel, sub=tp),
        grid=(nb,),
        in_specs=[blk(nqk, 0), blk(la3.shape[2], 0), blk(nv, 0), blk(nv, 1),
                  pl.BlockSpec(g_head.shape, lambda b: (0, 0)), st],
        out_specs=(blk(nv, 0), st),
        out_shape=(jax.ShapeDtypeStruct((nb, tp, nv), BF16),
                   jax.ShapeDtypeStruct(s0.shape, F32)),
        compiler_params=_cparams("parallel"),
        name="gla_sample",
    )(zf3, la3, zb3, zb3, g_head, s0)


def _merge_kernel(oa_ref, ob_ref, sga_ref, sgb_ref, x_ref, wpd_ref, wpg_ref, wo_ref, g1_ref, g2_ref,
                  x1_ref, h2_ref):
    m = (sga_ref[...].astype(F32) * _dot(oa_ref[...], wpd_ref[...])
         + sgb_ref[...].astype(F32) * _dot(ob_ref[...], wpg_ref[...]))
    mix = _dot(m.astype(BF16), wo_ref[...])
    x1 = x_ref[...] + _rms(mix, g1_ref[...])
    x1_ref[...] = x1
    h2_ref[...] = _rms(x1, g2_ref[...]).astype(BF16)


def _merge(oa, ob, zg, x, wpd, wpg, wo, g1, g2, *, tm):
    m, d = x.shape
    na, nb_ = oa.shape[1], ob.shape[1]
    row = lambda n, col=0: pl.BlockSpec((tm, n), lambda i: (i, col))
    return pl.pallas_call(
        _merge_kernel,
        grid=(m // tm,),
        in_specs=[row(na), row(nb_), row(d, 0), row(d, 1), row(d),
                  _const_spec(wpd.shape), _const_spec(wpg.shape), _const_spec(wo.shape),
                  _const_spec(g1.shape), _const_spec(g2.shape)],
        out_specs=(row(d), row(d)),
        out_shape=(jax.ShapeDtypeStruct((m, d), F32), jax.ShapeDtypeStruct((m, d), BF16)),
        compiler_params=_cparams("parallel"),
        name="merge",
    )(oa, ob, zg, zg, x, wpd, wpg, wo, g1, g2)


def _geglu(cg, cv):
    return (jax.nn.gelu(cg, approximate=True) * cv).astype(BF16)


def _ffn_finish(acc, x1_ref, g_ref, y_ref):
    y_ref[...] = x1_ref[...] + _rms(acc, g_ref[...])


def _ffn_prompt_kernel(hp_ref, h_ref, x1_ref, wug_ref, wuv_ref, wcg_ref, wcv_ref, bcg_ref, bcv_ref,
                       wd_ref, g_ref, y_ref, csg_ref, csv_ref, *, tm, rb, halo, cw, seq_tiles):
    i = pl.program_id(0)
    j = pl.program_id(1)

    @pl.when(j == 0)
    def _():
        y_ref[...] = jnp.zeros(y_ref.shape, F32)

    hp = jnp.where(i % seq_tiles == 0, jnp.zeros_like(hp_ref[...]), hp_ref[...])
    n_rb = tm // rb
    lhs = [jnp.concatenate([hp, h_ref[0:rb]], axis=0)] + [h_ref[r * rb:(r + 1) * rb] for r in range(1, n_rb)]
    chunks = [slice(c0, c0 + cw) for c0 in range(0, wug_ref.shape[1], cw)]

    ups = [[(_dot(x, wug_ref[:, cols]), _dot(x, wuv_ref[:, cols])) for cols in chunks] for x in lhs]

    def conv(u, cols, wc_ref, bc_ref):
        return (bc_ref[:, cols] + wc_ref[0:1, cols] * u[halo - 2:halo - 2 + rb]
                + wc_ref[1:2, cols] * u[halo - 1:halo - 1 + rb] + wc_ref[2:3, cols] * u[halo:halo + rb])

    for r in range(n_rb):
        acts = []
        for c, cols in enumerate(chunks):
            ug, uv = ups[r][c]
            if r > 0:
                top = rb if r > 1 else rb + halo
                ug = jnp.concatenate([ups[r - 1][c][0][top - halo:top], ug], axis=0)
                uv = jnp.concatenate([ups[r - 1][c][1][top - halo:top], uv], axis=0)
            if r == n_rb - 1:
                csg_ref[:, cols] = ug[halo + rb - 8:halo + rb]
                csv_ref[:, cols] = uv[halo + rb - 8:halo + rb]
            acts.append(_geglu(conv(ug, cols, wcg_ref, bcg_ref), conv(uv, cols, wcv_ref, bcv_ref)))
        a = jnp.concatenate(acts, axis=1) if len(acts) > 1 else acts[0]
        y_ref[r * rb:(r + 1) * rb] += _dot(a, wd_ref[...])

    @pl.when(j == pl.num_programs(1) - 1)
    def _():
        _ffn_finish(y_ref[...], x1_ref, g_ref, y_ref)


def _ffn_prompt(h2, x1, w_up, w_conv, b_conv, w_down, g, *, seq, tm, tf):
    m, d = x1.shape
    d_ff = w_down.shape[0]
    nf = d_ff // tf
    halo = SUBLANES_BF16
    hb = tm // halo
    col = lambda rows, off: pl.BlockSpec((rows, tf), lambda i, j: (0, j + off))
    tok = lambda n: pl.BlockSpec((tm, n), lambda i, j: (i, 0))
    cs = pl.BlockSpec((8, tf), lambda i, j: (i, j))
    cs_shape = jax.ShapeDtypeStruct((m // tm * 8, d_ff), F32)
    cw = min(tf, 2 * LANES)
    rb = min(tm, 2 * LANES)
    x1_spec = pl.BlockSpec((tm, d), lambda i, j: (i, 0), pipeline_mode=pl.Buffered(1))
    return pl.pallas_call(
        functools.partial(_ffn_prompt_kernel, tm=tm, rb=rb, halo=halo, cw=cw, seq_tiles=seq // tm),
        grid=(m // tm, nf),
        in_specs=[pl.BlockSpec((halo, d), lambda i, j: (jnp.maximum(i * hb - 1, 0), 0)),
                  tok(d), x1_spec, col(d, 0), col(d, nf), col(CONV_W, 0), col(CONV_W, nf),
                  col(1, 0), col(1, nf), pl.BlockSpec((tf, d), lambda i, j: (j, 0)),
                  pl.BlockSpec(g.shape, lambda i, j: (0, 0))],
        out_specs=(tok(d), cs, cs),
        out_shape=(jax.ShapeDtypeStruct((m, d), F32), cs_shape, cs_shape),
        compiler_params=_cparams("parallel", "arbitrary"),
        name="ffn_prompt",
    )(h2, h2, x1, w_up, w_up, w_conv, w_conv, b_conv, b_conv, w_down, g)


def _ffn_sample_kernel(h_ref, x1_ref, ctxg_ref, ctxv_ref, wug_ref, wuv_ref, wcg_ref, wcv_ref,
                       bcg_ref, bcv_ref, wd_ref, g_ref, y_ref, csg_ref, csv_ref, acc_scr, *, nb):
    j = pl.program_id(0)
    m = h_ref.shape[0]
    h = h_ref[...]

    def conv(w_up_ref, ctx_ref, wc_ref, bc_ref, cs_ref):
        u = _dot(h, w_up_ref[...])
        up = jnp.concatenate([ctx_ref[0], ctx_ref[1], u], axis=0)
        cs_ref[0] = u[m - 2 * nb:m - nb]
        cs_ref[1] = u[m - nb:m]
        return (bc_ref[...] + wc_ref[0:1] * up[0:m] + wc_ref[1:2] * up[nb:nb + m]
                + wc_ref[2:3] * up[2 * nb:2 * nb + m])

    a = _geglu(conv(wug_ref, ctxg_ref, wcg_ref, bcg_ref, csg_ref),
               conv(wuv_ref, ctxv_ref, wcv_ref, bcv_ref, csv_ref))
    part = _dot(a, wd_ref[...])

    @pl.when(j == 0)
    def _():
        acc_scr[...] = part

    @pl.when(j > 0)
    def _():
        acc_scr[...] += part

    @pl.when(j == pl.num_programs(0) - 1)
    def _():
        _ffn_finish(acc_scr[...], x1_ref, g_ref, y_ref)


def _ffn_sample(h2, x1, ctx, w_up, w_conv, b_conv, w_down, g, *, tf):
    m, d = x1.shape
    nb = ctx.shape[1]
    d_ff = w_down.shape[0]
    nf = d_ff // tf
    col = lambda rows, off: pl.BlockSpec((rows, tf), lambda j: (0, j + off))
    full = lambda a: pl.BlockSpec(a.shape, lambda j: (0, 0))
    ctx_spec = lambda off: pl.BlockSpec((CONV_W - 1, nb, tf), lambda j: (0, 0, j + off))
    cs = pl.BlockSpec((CONV_W - 1, nb, tf), lambda j: (0, 0, j))
    cs_shape = jax.ShapeDtypeStruct((CONV_W - 1, nb, d_ff), F32)
    return pl.pallas_call(
        functools.partial(_ffn_sample_kernel, nb=nb),
        grid=(nf,),
        in_specs=[full(h2), full(x1), ctx_spec(0), ctx_spec(nf), col(d, 0), col(d, nf),
                  col(CONV_W, 0), col(CONV_W, nf), col(1, 0), col(1, nf),
                  pl.BlockSpec((tf, d), lambda j: (j, 0)), full(g)],
        out_specs=(full(x1), cs, cs),
        out_shape=(jax.ShapeDtypeStruct((m, d), F32), cs_shape, cs_shape),
        scratch_shapes=[pltpu.VMEM((m, d), F32)],
        compiler_params=_cparams("arbitrary"),
        name="ffn_sample",
    )(h2, x1, ctx, ctx, w_up, w_up, w_conv, w_conv, b_conv, b_conv, w_down, g)


def _pick(n, candidates):
    for c in candidates:
        if n % c == 0:
            return c
    raise ValueError(f"no tile in {candidates} divides {n}")


def _layer_weights(l, p):
    d, n_in = p['w_in'].shape[1:]
    n_attn = p['cache_heads'] * 2 * DH_A
    w = p['w_in'][l]
    o_gla = 3 * n_attn
    o_lr = o_gla + 2 * H_B * DK_B + 2 * H_B * DV_B
    o_gate = o_lr + GATE_RANK
    assert n_in == o_gate + 2 * d
    w_lead_t = w.T.astype(BF16)
    w_gate_t = w_lead_t[o_gate:]
    wg = jnp.concatenate([p['w_gla_gate'][l].astype(F32),
                          jnp.zeros((LANES - GATE_RANK, H_B * DK_B), F32)], axis=0)
    wg_hi = wg.astype(BF16)
    wg_lo = (wg - wg_hi.astype(F32)).astype(BF16)
    row = lambda a: a[l].reshape(1, -1).astype(F32)
    return dict(
        w_lead_t=w_lead_t, w_gate_t=w_gate_t, n_attn=n_attn, o_vr=o_gla + 2 * H_B * DK_B, o_lr=o_lr,
        wg_hi=wg_hi, wg_lo=wg_lo, bg=row(p['b_gla_gate']),
        g_pre_mix=row(p['g_pre_mix']), g_head_diff=row(p['g_head_diff']), g_head_gla=row(p['g_head_gla']),
        wpd=p['w_proj_diff'][l].astype(BF16), wpg=p['w_proj_gla'][l].astype(BF16),
        wo=p['w_out'][l].astype(BF16), g_post_mix=row(p['g_post_mix']), g_pre_ffn=row(p['g_pre_ffn']),
        w_up=p['w_up'][l].astype(BF16), w_conv=p['w_conv'][l].astype(F32), b_conv=row(p['b_conv']),
        w_down=p['w_down'][l].astype(BF16), g_post_ffn=row(p['g_post_ffn']))


def _token_stages(x, lw, *, tm_proj, tm_mix, transposed_v):
    h, *proj = _attn_proj(x, lw['g_pre_mix'], lw['w_lead_t'], lw['o_lr'], lw['wg_hi'], lw['wg_lo'],
                          lw['bg'], n_attn=lw['n_attn'], tm=tm_proj, transposed_v=transposed_v)
    nv = H_B * DV_B
    zb = _act_proj(h, lw['w_lead_t'], lw['o_vr'], 2 * nv, ('id',) * 4 + ('silu',) * 4,
                   tm=tm_mix, tn=2 * nv, name="gla_proj")
    zg = _act_proj(h, lw['w_gate_t'], 0, lw['w_gate_t'].shape[0], ('sigmoid',) * 4,
                   tm=tm_mix, tn=nv, name="gate_proj")
    return proj, zb, zg


def kernel(x_prompt, x_sample, cache_k, cache_v, page_table, state_gla, state_conv, rel_bias,
           g_pre_mix, w_in, lambda_q1, lambda_k1, lambda_q2, lambda_k2, g_head_diff,
           w_gla_gate, b_gla_gate, g_head_gla, w_proj_diff, w_proj_gla, w_out, g_post_mix,
           g_pre_ffn, w_up, w_conv, b_conv, w_down, g_post_ffn):
    batch, seq, d = x_prompt.shape
    nb, t_dec, _ = x_sample.shape
    depth, n_pool, page, n_heads, kw = cache_k.shape
    assert page == PAGE_SIZE and kw == 2 * DH_A and cache_v.shape[-1] == DV_A
    d_ff = w_down.shape[1]
    width = n_heads * DV_A
    params = dict(w_in=w_in, w_gla_gate=w_gla_gate, b_gla_gate=b_gla_gate, g_pre_mix=g_pre_mix,
                  g_head_diff=g_head_diff, g_head_gla=g_head_gla, w_proj_diff=w_proj_diff,
                  w_proj_gla=w_proj_gla, w_out=w_out, g_post_mix=g_post_mix, g_pre_ffn=g_pre_ffn,
                  w_up=w_up, w_conv=w_conv, b_conv=b_conv, w_down=w_down, g_post_ffn=g_post_ffn,
                  cache_heads=n_heads)

    mp = batch * seq
    ms = nb * t_dec
    tq = _pick(seq, (512, 256, 128))
    tm_proj = _pick(mp, (256, 128))
    tm_mix = _pick(mp, (1024, 512, 256, 128))
    tm_ffn = _pick(seq, (1024, 512, 256, 128))
    tf = _pick(d_ff, (512, 256, 128))
    chunk = _pick(seq, (GLA_CHUNK,))
    pps = _pick(page_table.shape[1], (16, 8, 4, 2, 1))
    prompt_steps = batch * n_heads * (seq // tq)
    pps_fused = next((c for c in (16, 8, 4, 2, 1) if page_table.shape[1] % c == 0
                      and nb * (page_table.shape[1] // c) == prompt_steps), 0)
    t_pad = SUBLANES_BF16

    page_table = page_table.astype(jnp.int32)
    eye_2 = jnp.eye(2, dtype=BF16)

    yp = x_prompt.reshape(mp, d)
    ys = x_sample.reshape(ms, d)
    outs = [[] for _ in range(8)]
    for l in range(depth):
        lam0 = _lambda_init(l)
        lw = _layer_weights(l, params)
        toe, sb_far, sb_last, sb_new, lam = _bias_tables(rel_bias, lambda_q1[l], lambda_k1[l], lambda_q2[l],
                                                 lambda_k2[l], n_heads=n_heads, t_dec=t_dec, lam_init=lam0)

        (qkv, k32, v32, zf, log_a, vt), zb, zg = _token_stages(yp, lw, tm_proj=tm_proj, tm_mix=tm_mix,
                                                               transposed_v=True)
        (qkv_s, k32_s, v32_s, zf_s, log_a_s), zb_s, zg_s = _token_stages(ys, lw, tm_proj=ms, tm_mix=ms,
                                                                         transposed_v=False)
        q5 = (qkv_s[:, :width].astype(F32) * LOG2_E).astype(BF16).reshape(nb, t_dec, n_heads, 2, DH_A)
        q_rows = jnp.einsum('bthnd,mn->bmthnd', q5, eye_2).reshape(nb, 2 * t_dec * n_heads, 2 * DH_A)
        per_head = lambda a: a.reshape(nb, t_dec * n_heads, -1)
        kn, vn = per_head(qkv_s[:, width:2 * width]), per_head(qkv_s[:, 2 * width:])
        if pps_fused:
            oa, oa_s = _attn_fused(qkv, vt, toe, q_rows, kn, vn, cache_k, cache_v, page_table, l,
                                   sb_far, sb_last, sb_new, lam, lw['g_head_diff'], batch=batch, seq=seq,
                                   n_heads=n_heads, tq=tq, pps=pps_fused, lam_init=lam0)
        else:
            oa = _attn_prompt(qkv, vt, toe, lam, lw['g_head_diff'], batch=batch, seq=seq,
                              n_heads=n_heads, tq=tq, lam_init=lam0)
            oa_s = _attn_sample(q_rows, kn, vn, cache_k, cache_v, page_table, l, sb_far, sb_last, sb_new,
                                lam, lw['g_head_diff'], pps=pps, lam_init=lam0)

        ob, s_p = _gla_prompt(zf, log_a, zb, lw['g_head_gla'], batch=batch, seq=seq, chunk=chunk,
                              sub=GLA_SUB)
        x1, h2 = _merge(oa, ob.reshape(mp, -1), zg, yp, lw['wpd'], lw['wpg'], lw['wo'],
                        lw['g_post_mix'], lw['g_pre_ffn'], tm=tm_proj)
        yp, csg, csv = _ffn_prompt(h2, x1, lw['w_up'], lw['w_conv'], lw['b_conv'], lw['w_down'],
                                   lw['g_post_ffn'], seq=seq, tm=tm_ffn, tf=tf)
        cs = jnp.concatenate([csg, csv], axis=1).reshape(batch, seq // tm_ffn, 8, 2 * d_ff)
        outs[0].append(k32.reshape(batch, seq, n_heads, 2 * DH_A))
        outs[1].append(v32.reshape(batch, seq, n_heads, DV_A))
        outs[2].append(s_p)
        outs[3].append(cs[:, -1, 8 - (CONV_W - 1):])

        pad_t = lambda a: jnp.pad(a.reshape(nb, t_dec, -1), ((0, 0), (0, t_pad - t_dec), (0, 0)))
        ob, s_s = _gla_sample(pad_t(zf_s), pad_t(log_a_s), pad_t(zb_s), lw['g_head_gla'], state_gla[l])
        x1, h2 = _merge(oa_s.reshape(ms, -1), ob[:, :t_dec].reshape(ms, -1), zg_s, ys, lw['wpd'],
                        lw['wpg'], lw['wo'], lw['g_post_mix'], lw['g_pre_ffn'], tm=ms)
        tmajor = lambda a: a.reshape(nb, t_dec, -1).transpose(1, 0, 2).reshape(ms, -1)
        y_t, csg, csv = _ffn_sample(tmajor(h2), tmajor(x1), state_conv[l].transpose(1, 0, 2),
                                    lw['w_up'], lw['w_conv'], lw['b_conv'], lw['w_down'],
                                    lw['g_post_ffn'], tf=tf)
        ys = y_t.reshape(t_dec, nb, d).transpose(1, 0, 2).reshape(ms, d)
        outs[4].append(k32_s.reshape(nb, t_dec, n_heads, 2 * DH_A))
        outs[5].append(v32_s.reshape(nb, t_dec, n_heads, DV_A))
        outs[6].append(s_s)
        outs[7].append(jnp.concatenate([csg, csv], axis=2).transpose(1, 0, 2))

    return (yp.reshape(batch, seq, d), ys.reshape(nb, t_dec, d)) + tuple(jnp.stack(o) for o in outs)
```
